```python
import jax, jax.numpy as jnp
from jax import lax
import numpy as np

D_MODEL = 2048
BATCH = 4
SEQ = 4096
DEPTH = 1

CHUNK = 64
QBLOCK = 128
FOX_HEADS = 8
FOX_HEAD_DIM = D_MODEL // 16
FOX_WIDTH = FOX_HEADS * FOX_HEAD_DIM
MLSTM_HEADS = 4
MLSTM_V_DIM = D_MODEL // 8
MLSTM_QK_DIM = MLSTM_V_DIM // 2
MLSTM_WIDTH = MLSTM_HEADS * MLSTM_V_DIM
MLSTM_QK_WIDTH = MLSTM_HEADS * MLSTM_QK_DIM
CONV_WIDTH = 4
N_MEM = 256
XATTN_HEADS = 4
XATTN_HEAD_DIM = D_MODEL // XATTN_HEADS
D_FF = 4 * D_MODEL
EPS = 1e-6

IN_SPLITS = (FOX_WIDTH, FOX_WIDTH, FOX_WIDTH, FOX_HEADS,
             2 * MLSTM_QK_WIDTH, MLSTM_WIDTH, MLSTM_HEADS, MLSTM_HEADS,
             MLSTM_WIDTH)
IN_WIDTH = sum(IN_SPLITS)

kernel_name = "fox_mlstm_hybrid_streaming_block"


def _rms(x, g):
    x32 = x.astype(jnp.float32)
    y = x32 * lax.rsqrt(jnp.mean(x32 * x32, axis=-1, keepdims=True) + EPS)
    return (y * g.astype(jnp.float32)).astype(x.dtype)


def _to_heads(t, h):
    b, s, w = t.shape
    return t.reshape(b, s, h, w // h).transpose(0, 2, 1, 3)


def _from_heads(t):
    b, h, s, d = t.shape
    return t.transpose(0, 2, 1, 3)


def _causal_dwconv(u, w, b):
    c = u.shape[-1]
    y = lax.conv_general_dilated(u, w.astype(u.dtype)[:, None, :], window_strides=(1,),
                                 padding=[(CONV_WIDTH - 1, 0)],
                                 dimension_numbers=("NWC", "WIO", "NWC"),
                                 feature_group_count=c)
    return y + b.astype(u.dtype)


def _fox_attention(q, k, v, logf):
    s_len, d = q.shape[2], q.shape[3]
    scale = d ** -0.5
    c = lax.cumsum(logf, axis=2)
    outs = []
    for blk in range(s_len // QBLOCK):
        q0, q1 = blk * QBLOCK, (blk + 1) * QBLOCK
        qb, kb, vb = q[:, :, q0:q1], k[:, :, :q1], v[:, :, :q1]
        sc = jnp.einsum("bhqd,bhkd->bhqk", qb, kb).astype(jnp.float32) * scale
        sc = sc + c[:, :, q0:q1, None] - c[:, :, None, :q1]
        qpos = q0 + jnp.arange(QBLOCK)
        kpos = jnp.arange(q1)
        sc = jnp.where(qpos[:, None] >= kpos[None, :], sc, -jnp.inf)
        p = jax.nn.softmax(sc, axis=-1).astype(v.dtype)
        outs.append(jnp.einsum("bhqk,bhkd->bhqd", p, vb))
    return jnp.concatenate(outs, axis=2)


def _mlstm_chunkwise(q, k, v, ig, lf):
    bsz, nh, s_len, dk = q.shape
    dv = v.shape[-1]
    nc = s_len // CHUNK
    f32 = jnp.float32
    q = q.astype(f32)
    k = k.astype(f32) * (dk ** -0.5)
    v = v.astype(f32)

    def chunks(t):
        return jnp.moveaxis(t.reshape(bsz, nh, nc, CHUNK, *t.shape[3:]), 2, 0)

    tril = jnp.tril(jnp.ones((CHUNK, CHUNK), dtype=bool))

    def step(carry, inp):
        C, n, m = carry
        qc, kc, vc, ic, fc = inp
        b = jnp.cumsum(fc, axis=-1)
        dmat = b[..., :, None] - b[..., None, :] + ic[..., None, :]
        dmat = jnp.where(tril, dmat, -jnp.inf)
        inter = b + m[..., None]
        m_t = jnp.maximum(inter, jnp.max(dmat, axis=-1))
        w = jnp.exp(dmat - m_t[..., None])
        g = jnp.exp(inter - m_t)
        sqk = jnp.einsum("bhtd,bhsd->bhts", qc, kc) * w
        num = jnp.einsum("bhts,bhsv->bhtv", sqk, vc) + g[..., None] * jnp.einsum("bhvd,bhtd->bhtv", C, qc)
        den = jnp.sum(sqk, axis=-1) + g * jnp.einsum("bhd,bhtd->bht", n, qc)
        h = num / jnp.maximum(jnp.abs(den), jnp.exp(-m_t))[..., None]
        a = b[..., -1:] - b + ic
        m_new = jnp.maximum(b[..., -1] + m, jnp.max(a, axis=-1))
        wa = jnp.exp(a - m_new[..., None])
        decay = jnp.exp(b[..., -1] + m - m_new)
        C_new = decay[..., None, None] * C + jnp.einsum("bhs,bhsv,bhsd->bhvd", wa, vc, kc)
        n_new = decay[..., None] * n + jnp.einsum("bhs,bhsd->bhd", wa, kc)
        return (C_new, n_new, m_new), h

    init = (jnp.zeros((bsz, nh, dv, dk), f32), jnp.zeros((bsz, nh, dk), f32), jnp.zeros((bsz, nh), f32))
    _, hs = lax.scan(step, init, (chunks(q), chunks(k), chunks(v), chunks(ig), chunks(lf)))
    return jnp.moveaxis(hs, 0, 2).reshape(bsz, nh, s_len, dv)


def setup_inputs(seed: int = 0) -> dict:
    key = jax.random.key(seed)
    ks = jax.random.split(key, 24)
    nrm = jax.random.normal
    def gain(k, shape):
        return 1.0 + 0.02 * nrm(k, shape, jnp.float32)
    return {
        "x": nrm(ks[0], (BATCH, SEQ, D_MODEL), jnp.float32),
        "mem": nrm(ks[1], (BATCH, N_MEM, D_MODEL), jnp.float32),
        "mixer_norm": gain(ks[2], (DEPTH, D_MODEL)),
        "w_in": nrm(ks[3], (DEPTH, D_MODEL, IN_WIDTH), jnp.float32) * D_MODEL ** -0.5,
        "fox_f_bias": jax.random.uniform(ks[4], (DEPTH, FOX_HEADS), jnp.float32, 1.0, 6.0),
        "mlstm_i_bias": 0.1 * nrm(ks[5], (DEPTH, MLSTM_HEADS), jnp.float32),
        "mlstm_f_bias": jax.random.uniform(ks[6], (DEPTH, MLSTM_HEADS), jnp.float32, 3.0, 6.0),
        "conv_w": nrm(ks[7], (DEPTH, CONV_WIDTH, 2 * MLSTM_QK_WIDTH), jnp.float32) * CONV_WIDTH ** -0.5,
        "conv_b": 0.02 * nrm(ks[8], (DEPTH, 2 * MLSTM_QK_WIDTH), jnp.float32),
        "fox_q_norm": gain(ks[9], (DEPTH, FOX_HEAD_DIM)),
        "fox_k_norm": gain(ks[10], (DEPTH, FOX_HEAD_DIM)),
        "fox_out_norm": gain(ks[11], (DEPTH, FOX_WIDTH)),
        "mlstm_out_norm": gain(ks[12], (DEPTH, MLSTM_WIDTH)),
        "w_out": nrm(ks[13], (DEPTH, FOX_WIDTH + MLSTM_WIDTH, D_MODEL), jnp.float32) * (FOX_WIDTH + MLSTM_WIDTH) ** -0.5,
        "xattn_norm": gain(ks[14], (DEPTH, D_MODEL)),
        "mem_norm": gain(ks[15], (DEPTH, D_MODEL)),
        "w_xq": nrm(ks[16], (DEPTH, D_MODEL, D_MODEL), jnp.float32) * D_MODEL ** -0.5,
        "w_xkv": nrm(ks[17], (DEPTH, D_MODEL, 2 * D_MODEL), jnp.float32) * D_MODEL ** -0.5,
        "xq_norm": gain(ks[18], (DEPTH, XATTN_HEAD_DIM)),
        "xk_norm": gain(ks[19], (DEPTH, XATTN_HEAD_DIM)),
        "w_xo": nrm(ks[20], (DEPTH, D_MODEL, D_MODEL), jnp.float32) * D_MODEL ** -0.5,
        "mlp_norm": gain(ks[21], (DEPTH, D_MODEL)),
        "w_up": nrm(ks[22], (DEPTH, D_MODEL, D_FF), jnp.float32) * D_MODEL ** -0.5,
        "w_down": nrm(ks[23], (DEPTH, D_FF, D_MODEL), jnp.float32) * D_FF ** -0.5,
    }


def reference(x, mem, mixer_norm, w_in, fox_f_bias, mlstm_i_bias, mlstm_f_bias, conv_w, conv_b,
              fox_q_norm, fox_k_norm, fox_out_norm, mlstm_out_norm, w_out, xattn_norm, mem_norm,
              w_xq, w_xkv, xq_norm, xk_norm, w_xo, mlp_norm, w_up, w_down):
    bsz, s_len, _ = x.shape
    f32 = jnp.float32
    split_idx = list(np.cumsum(IN_SPLITS)[:-1])
    for l in range(DEPTH):
        xn = _rms(x, mixer_norm[l])
        proj = xn @ w_in[l]
        fq, fk, fv, ff, mqk, mv, mi, mf, mo = jnp.split(proj, split_idx, axis=-1)

        fq = _rms(_to_heads(fq, FOX_HEADS), fox_q_norm[l])
        fk = _rms(_to_heads(fk, FOX_HEADS), fox_k_norm[l])
        fv = _to_heads(fv, FOX_HEADS)
        logf = jax.nn.log_sigmoid(ff.astype(f32) + fox_f_bias[l].astype(f32)).transpose(0, 2, 1)
        fo = _from_heads(_fox_attention(fq, fk, fv, logf))
        fo = _rms(fo, fox_out_norm[l].reshape(FOX_HEADS, FOX_HEAD_DIM)).reshape(bsz, s_len, FOX_WIDTH)

        mqk = jax.nn.silu(_causal_dwconv(mqk, conv_w[l], conv_b[l]))
        mq, mk = jnp.split(mqk, 2, axis=-1)
        ig = (mi.astype(f32) + mlstm_i_bias[l].astype(f32)).transpose(0, 2, 1)
        lf = jax.nn.log_sigmoid(mf.astype(f32) + mlstm_f_bias[l].astype(f32)).transpose(0, 2, 1)
        mh = _mlstm_chunkwise(_to_heads(mq, MLSTM_HEADS), _to_heads(mk, MLSTM_HEADS),
                              _to_heads(mv, MLSTM_HEADS), ig, lf).astype(x.dtype)
        mh = _rms(_from_heads(mh), mlstm_out_norm[l].reshape(MLSTM_HEADS, MLSTM_V_DIM))
        mh = mh.reshape(bsz, s_len, MLSTM_WIDTH) * jax.nn.sigmoid(mo)

        x = x + jnp.concatenate([fo, mh], axis=-1) @ w_out[l]

        xn = _rms(x, xattn_norm[l])
        mn = _rms(mem, mem_norm[l])
        q = _rms(_to_heads(xn @ w_xq[l], XATTN_HEADS), xq_norm[l])
        mk_, mv_ = jnp.split(mn @ w_xkv[l], 2, axis=-1)
        k = _rms(_to_heads(mk_, XATTN_HEADS), xk_norm[l])
        v = _to_heads(mv_, XATTN_HEADS)
        sc = jnp.einsum("bhqd,bhkd->bhqk", q, k).astype(f32) * XATTN_HEAD_DIM ** -0.5
        p = jax.nn.softmax(sc, axis=-1).astype(v.dtype)
        co = _from_heads(jnp.einsum("bhqk,bhkd->bhqd", p, v)).reshape(bsz, s_len, D_MODEL)
        x = x + co @ w_xo[l]

        xn = _rms(x, mlp_norm[l])
        x = x + jnp.square(jax.nn.relu(xn @ w_up[l])) @ w_down[l]
    return x
```

```python
import functools

import jax
import jax.numpy as jnp
from jax import lax
from jax.experimental import pallas as pl
from jax.experimental.pallas import tpu as pltpu

F32 = jnp.float32
BF16 = jnp.bfloat16

EPS = 1e-6
FOX_HEADS = 8
FOX_HEAD_DIM = 128
MLSTM_HEADS = 4
MLSTM_QK_DIM = 128
MLSTM_V_DIM = 256
CONV_WIDTH = 4
XATTN_HEADS = 4
N_GATES = 16
LANES = 128
SUBLANES = 8
NEG_BIG = -1e30

MLSTM_CHUNK = 256
FOX_BLOCK = 512
VMEM_LIMIT = 56 * 1024 * 1024


def _cparams(sem):
    return pltpu.CompilerParams(dimension_semantics=sem, vmem_limit_bytes=VMEM_LIMIT)


def _rms_rows(x, gain):
    ms = jnp.mean(x * x, axis=-1, keepdims=True)
    return x * lax.rsqrt(ms + EPS) * gain


def _sigmoid(x):
    return 1.0 / (1.0 + jnp.exp(-x))


def _norm_proj_kernel(x_ref, g_ref, w_ref, hg_ref, *rest, head_dim, n_norm_tiles, with_gates):
    if with_gates:
        wg_ref, out_ref, gates_ref, xn_ref = rest
    else:
        out_ref, xn_ref = rest
    j = pl.program_id(1)

    @pl.when(j == 0)
    def _():
        xn_ref[...] = _rms_rows(x_ref[...], g_ref[...]).astype(BF16)
        if with_gates:
            gates_ref[...] = jnp.dot(xn_ref[...], wg_ref[...], preferred_element_type=F32)

    acc = jnp.dot(xn_ref[...], w_ref[...], preferred_element_type=F32)
    tn = acc.shape[1]

    @pl.when(j < n_norm_tiles)
    def _():
        gain = hg_ref[0]
        for h in range(tn // head_dim):
            sl = slice(h * head_dim, (h + 1) * head_dim)
            out_ref[:, sl] = _rms_rows(acc[:, sl], gain).astype(out_ref.dtype)

    @pl.when(j >= n_norm_tiles)
    def _():
        out_ref[...] = acc.astype(out_ref.dtype)


def _norm_proj(x, gain, w, head_gains, *, tm, tn, head_dim, n_norm_tiles, w_gates=None):
    m, k = x.shape
    n = w.shape[1]
    with_gates = w_gates is not None
    n_hg = head_gains.shape[0]
    in_specs = [
        pl.BlockSpec((tm, k), lambda i, j: (i, 0)),
        pl.BlockSpec((1, k), lambda i, j: (0, 0)),
        pl.BlockSpec((k, tn), lambda i, j: (0, j)),
        pl.BlockSpec((1, 1, head_dim), lambda i, j: (jnp.minimum(j, n_hg - 1), 0, 0)),
    ]
    args = [x, gain.reshape(1, k), w, head_gains.reshape(n_hg, 1, head_dim)]
    out_shape = [jax.ShapeDtypeStruct((m, n), BF16)]
    out_specs = [pl.BlockSpec((tm, tn), lambda i, j: (i, j))]
    if with_gates:
        in_specs.append(pl.BlockSpec((k, LANES), lambda i, j: (0, 0)))
        args.append(w_gates)
        out_shape.append(jax.ShapeDtypeStruct((m, LANES), F32))
        out_specs.append(pl.BlockSpec((tm, LANES), lambda i, j: (i, 0)))
    res = pl.pallas_call(
        functools.partial(_norm_proj_kernel, head_dim=head_dim, n_norm_tiles=n_norm_tiles, with_gates=with_gates),
        grid=(m // tm, n // tn),
        in_specs=in_specs,
        out_specs=out_specs,
        out_shape=out_shape,
        scratch_shapes=[pltpu.VMEM((tm, k), BF16)],
        compiler_params=_cparams(("parallel", "arbitrary")),
        name="norm_proj_gates" if with_gates else "norm_proj",
    )(*args)
    return res if with_gates else res[0]


def _proj_residual_kernel(*refs, n_lhs):
    lhs = refs[:n_lhs]
    ws = refs[n_lhs:2 * n_lhs]
    res_ref, out_ref = refs[2 * n_lhs:]
    acc = res_ref[...]
    for a, w in zip(lhs, ws):
        acc = acc + jnp.dot(a[...], w[...], preferred_element_type=F32)
    out_ref[...] = acc


def _proj_residual(lhs_list, w, residual, *, tm, tn):
    m, n = residual.shape
    n_lhs = len(lhs_list)
    kp = lhs_list[0].shape[1]
    in_specs = [pl.BlockSpec((tm, kp), lambda i, j: (i, 0)) for _ in lhs_list]
    in_specs += [pl.BlockSpec((kp, tn), functools.partial(lambda i, j, p: (p, j), p=p)) for p in range(n_lhs)]
    in_specs += [pl.BlockSpec((tm, tn), lambda i, j: (i, j))]
    return pl.pallas_call(
        functools.partial(_proj_residual_kernel, n_lhs=n_lhs),
        grid=(m // tm, n // tn),
        in_specs=in_specs,
        out_specs=pl.BlockSpec((tm, tn), lambda i, j: (i, j)),
        out_shape=jax.ShapeDtypeStruct((m, n), F32),
        compiler_params=_cparams(("parallel", "parallel")),
        name="proj_residual",
    )(*lhs_list, *([w] * n_lhs), residual)


def _gates_kernel(g_ref, bias_ref, row_ref, col_ref):
    s_len = g_ref.shape[0]
    z = g_ref[...].T[0:N_GATES, :] + bias_ref[...]
    row = lax.broadcasted_iota(jnp.int32, z.shape, 0)
    is_forget = (row < FOX_HEADS) | (row >= FOX_HEADS + MLSTM_HEADS)
    logsig = jnp.minimum(z, 0.0) - jnp.log1p(jnp.exp(-jnp.abs(z)))
    c = jnp.where(is_forget, logsig, 0.0)
    lane = lax.broadcasted_iota(jnp.int32, z.shape, 1)
    shift = 1
    while shift < s_len:
        c = c + jnp.where(lane >= shift, pltpu.roll(c, shift, axis=1), 0.0)
        shift *= 2
    vals = jnp.where(is_forget, c, z)
    row_ref[...] = vals
    padded = jnp.concatenate([vals, jnp.zeros((LANES - N_GATES, s_len), F32)], axis=0)
    col_ref[...] = padded.T


def _gates(gates_raw, bias, *, batch, s_len):
    return pl.pallas_call(
        _gates_kernel,
        grid=(batch,),
        in_specs=[pl.BlockSpec((s_len, LANES), lambda b: (b, 0)),
                  pl.BlockSpec((N_GATES, 1), lambda b: (0, 0))],
        out_specs=[pl.BlockSpec((None, N_GATES, s_len), lambda b: (b, 0, 0)),
                   pl.BlockSpec((s_len, LANES), lambda b: (b, 0))],
        out_shape=[jax.ShapeDtypeStruct((batch, N_GATES, s_len), F32),
                   jax.ShapeDtypeStruct((batch * s_len, LANES), F32)],
        compiler_params=_cparams(("parallel",)),
        name="gates",
    )(gates_raw, bias)


def _fox_kernel(q_ref, k_ref, v_ref, gcol_ref, grow_ref, on_ref, o_ref, m_ref, l_ref, acc_ref):
    qi = pl.program_id(1)
    ki = pl.program_id(2)
    tq, tk = q_ref.shape[0], k_ref.shape[0]
    d = FOX_HEAD_DIM
    scale = d ** -0.5

    @pl.when(ki == 0)
    def _():
        m_ref[...] = jnp.full(m_ref.shape, NEG_BIG, F32)
        l_ref[...] = jnp.zeros(l_ref.shape, F32)
        acc_ref[...] = jnp.zeros(acc_ref.shape, F32)

    def step(diagonal):
        if diagonal:
            keep = (lax.broadcasted_iota(jnp.int32, (tq, tk), 0)
                    >= lax.broadcasted_iota(jnp.int32, (tq, tk), 1))
        for h in range(FOX_HEADS):
            sl = slice(h * d, (h + 1) * d)
            s = lax.dot_general(q_ref[:, sl], k_ref[:, sl], (((1,), (1,)), ((), ())),
                                preferred_element_type=F32) * scale
            s = s + (gcol_ref[:, h:h + 1] - grow_ref[h:h + 1, :])
            if diagonal:
                s = jnp.where(keep, s, NEG_BIG)
            m_prev = m_ref[:, h:h + 1]
            m_new = jnp.maximum(m_prev, jnp.max(s, axis=-1, keepdims=True))
            alpha = jnp.exp(m_prev - m_new)
            p = jnp.exp(s - m_new)
            l_ref[:, h:h + 1] = alpha * l_ref[:, h:h + 1] + jnp.sum(p, axis=-1, keepdims=True)
            acc_ref[:, sl] = alpha * acc_ref[:, sl] + jnp.dot(p.astype(BF16), v_ref[:, sl],
                                                              preferred_element_type=F32)
            m_ref[:, h:h + 1] = m_new

    @pl.when(ki < qi)
    def _():
        step(False)

    @pl.when(ki == qi)
    def _():
        step(True)
        for h in range(FOX_HEADS):
            sl = slice(h * d, (h + 1) * d)
            o = acc_ref[:, sl] / l_ref[:, h:h + 1]
            o_ref[:, sl] = _rms_rows(o, on_ref[:, sl]).astype(o_ref.dtype)


def _fox_attention(proj, gcol, grow, out_norm, *, batch, s_len):
    t = FOX_BLOCK
    nq = s_len // t
    width = FOX_HEADS * FOX_HEAD_DIM

    def kv_map(col):
        return lambda b, qi, ki: (b * nq + jnp.minimum(ki, qi), col)

    return pl.pallas_call(
        _fox_kernel,
        grid=(batch, nq, nq),
        in_specs=[
            pl.BlockSpec((t, width), lambda b, qi, ki: (b * nq + qi, 0)),
            pl.BlockSpec((t, width), kv_map(1)),
            pl.BlockSpec((t, width), kv_map(2)),
            pl.BlockSpec((t, LANES), lambda b, qi, ki: (b * nq + qi, 0)),
            pl.BlockSpec((None, N_GATES, t), lambda b, qi, ki: (b, 0, jnp.minimum(ki, qi))),
            pl.BlockSpec((1, width), lambda b, qi, ki: (0, 0)),
        ],
        out_specs=pl.BlockSpec((t, width), lambda b, qi, ki: (b * nq + qi, 0)),
        out_shape=jax.ShapeDtypeStruct((batch * s_len, width), BF16),
        scratch_shapes=[pltpu.VMEM((t, LANES), F32), pltpu.VMEM((t, LANES), F32), pltpu.VMEM((t, width), F32)],
        compiler_params=_cparams(("parallel", "parallel", "arbitrary")),
        name="fox_attention",
    )(proj, proj, proj, gcol, grow, out_norm.reshape(1, width))


def _mlstm_kernel(mqk_ref, mv_ref, mo_ref, gcol_ref, grow_ref, cw_ref, cb_ref, on_ref, out_ref,
                  ct_ref, n_ref, mg_ref, ext_ref):
    c = pl.program_id(1)
    length = mqk_ref.shape[0]
    dk, dv = MLSTM_QK_DIM, MLSTM_V_DIM
    tail = SUBLANES

    @pl.when(c == 0)
    def _():
        ct_ref[...] = jnp.zeros(ct_ref.shape, F32)
        n_ref[...] = jnp.zeros(n_ref.shape, F32)
        mg_ref[...] = jnp.zeros(mg_ref.shape, F32)
        ext_ref[0:tail, :] = jnp.zeros((tail, ext_ref.shape[1]), F32)

    u = mqk_ref[...].astype(F32)
    ext_ref[tail:, :] = u
    y = cb_ref[...]
    for j in range(CONV_WIDTH):
        start = tail - (CONV_WIDTH - 1) + j
        y = y + cw_ref[j:j + 1, :] * ext_ref[start:start + length, :]
    ext_ref[0:tail, :] = u[length - tail:, :]
    qk = y * _sigmoid(y)

    keep = (lax.broadcasted_iota(jnp.int32, (length, length), 0)
            >= lax.broadcasted_iota(jnp.int32, (length, length), 1))
    n_qk = MLSTM_HEADS * dk
    for h in range(MLSTM_HEADS):
        q = qk[:, h * dk:(h + 1) * dk]
        k = qk[:, n_qk + h * dk:n_qk + (h + 1) * dk] * (dk ** -0.5)
        qb, kb = q.astype(BF16), k.astype(BF16)
        v = mv_ref[:, h * dv:(h + 1) * dv]
        gi, gf = FOX_HEADS + h, FOX_HEADS + MLSTM_HEADS + h
        u_row = grow_ref[gi:gi + 1, :] - grow_ref[gf:gf + 1, :]
        f_col = gcol_ref[:, gf:gf + 1]
        u_col = gcol_ref[:, gi:gi + 1] - f_col
        mg = mg_ref[h:h + 1, 0:1]

        um = jnp.where(keep, u_row, NEG_BIG)
        a = jnp.maximum(jnp.max(um, axis=-1, keepdims=True), mg)
        w = jnp.exp(um - a)
        g = jnp.exp(mg - a)
        sqk = lax.dot_general(qb, kb, (((1,), (1,)), ((), ())), preferred_element_type=F32) * w
        inter = jnp.dot(qb, ct_ref[h].astype(BF16), preferred_element_type=F32)
        num = jnp.dot(sqk.astype(BF16), v, preferred_element_type=F32) + g * inter
        n_row = n_ref[h:h + 1, :]
        den = jnp.sum(sqk, axis=-1, keepdims=True) + g * jnp.sum(q * n_row, axis=-1, keepdims=True)
        hid = num / jnp.maximum(jnp.abs(den), jnp.exp(-(f_col + a)))

        sl = slice(h * dv, (h + 1) * dv)
        gate = _sigmoid(mo_ref[:, sl].astype(F32))
        out_ref[:, sl] = (_rms_rows(hid, on_ref[:, sl]) * gate).astype(out_ref.dtype)

        mg_new = jnp.maximum(mg, jnp.max(u_row, axis=-1, keepdims=True))
        decay = jnp.exp(mg - mg_new)
        wa = jnp.exp(u_col - mg_new)
        wav = (wa * v.astype(F32)).astype(BF16)
        ct_ref[h] = decay * ct_ref[h] + jnp.dot(k.T.astype(BF16), wav, preferred_element_type=F32)
        n_ref[h:h + 1, :] = decay * n_row + jnp.sum(wa * k, axis=0, keepdims=True)
        mg_ref[h:h + 1, :] = jnp.broadcast_to(mg_new, (1, LANES))


def _mlstm(proj, gcol, grow, conv_w, conv_b, out_norm, *, batch, s_len):
    length = MLSTM_CHUNK
    nc = s_len // length
    width = MLSTM_HEADS * MLSTM_V_DIM
    qk_width = 2 * MLSTM_HEADS * MLSTM_QK_DIM
    return pl.pallas_call(
        _mlstm_kernel,
        grid=(batch, nc),
        in_specs=[
            pl.BlockSpec((length, qk_width), lambda b, c: (b * nc + c, 3)),
            pl.BlockSpec((length, width), lambda b, c: (b * nc + c, 4)),
            pl.BlockSpec((length, width), lambda b, c: (b * nc + c, 5)),
            pl.BlockSpec((length, LANES), lambda b, c: (b * nc + c, 0)),
            pl.BlockSpec((None, N_GATES, length), lambda b, c: (b, 0, c)),
            pl.BlockSpec((CONV_WIDTH, qk_width), lambda b, c: (0, 0)),
            pl.BlockSpec((1, qk_width), lambda b, c: (0, 0)),
            pl.BlockSpec((1, width), lambda b, c: (0, 0)),
        ],
        out_specs=pl.BlockSpec((length, width), lambda b, c: (b * nc + c, 0)),
        out_shape=jax.ShapeDtypeStruct((batch * s_len, width), BF16),
        scratch_shapes=[
            pltpu.VMEM((MLSTM_HEADS, MLSTM_QK_DIM, MLSTM_V_DIM), F32),
            pltpu.VMEM((SUBLANES, MLSTM_QK_DIM), F32),
            pltpu.VMEM((SUBLANES, LANES), F32),
            pltpu.VMEM((length + SUBLANES, qk_width), F32),
        ],
        compiler_params=_cparams(("parallel", "arbitrary")),
        name="mlstm",
    )(proj, proj, proj, gcol, grow, conv_w, conv_b.reshape(1, qk_width), out_norm.reshape(1, width))


def _xattn_kernel(q_ref, k_ref, v_ref, o_ref):
    d = q_ref.shape[1] // XATTN_HEADS
    scale = d ** -0.5
    for h in range(XATTN_HEADS):
        sl = slice(h * d, (h + 1) * d)
        s = lax.dot_general(q_ref[:, sl], k_ref[:, sl], (((1,), (1,)), ((), ())),
                            preferred_element_type=F32) * scale
        p = jnp.exp(s - jnp.max(s, axis=-1, keepdims=True))
        p = p / jnp.sum(p, axis=-1, keepdims=True)
        o_ref[:, sl] = jnp.dot(p.astype(BF16), v_ref[:, sl], preferred_element_type=F32).astype(o_ref.dtype)


def _xattn(q, kv, *, batch, s_len, n_mem, tq):
    d_model = q.shape[1]
    nq = s_len // tq
    return pl.pallas_call(
        _xattn_kernel,
        grid=(batch, nq),
        in_specs=[
            pl.BlockSpec((tq, d_model), lambda b, i: (b * nq + i, 0)),
            pl.BlockSpec((n_mem, d_model), lambda b, i: (b, 0)),
            pl.BlockSpec((n_mem, d_model), lambda b, i: (b, 1)),
        ],
        out_specs=pl.BlockSpec((tq, d_model), lambda b, i: (b * nq + i, 0)),
        out_shape=jax.ShapeDtypeStruct(q.shape, BF16),
        compiler_params=_cparams(("parallel", "parallel")),
        name="xattn",
    )(q, kv, kv)


def _mlp_kernel(x_ref, g_ref, wu_ref, wd_ref, o_ref, xn_ref):
    f = pl.program_id(1)

    @pl.when(f == 0)
    def _():
        x = x_ref[...]
        xn_ref[...] = _rms_rows(x, g_ref[...]).astype(BF16)
        o_ref[...] = x

    hid = jnp.dot(xn_ref[...], wu_ref[...], preferred_element_type=F32)
    hid = jnp.square(jnp.maximum(hid, 0.0)).astype(BF16)
    o_ref[...] += jnp.dot(hid, wd_ref[...], preferred_element_type=F32)


def _mlp(x, gain, w_up, w_down, *, tm, tf):
    m, d = x.shape
    d_ff = w_up.shape[1]
    return pl.pallas_call(
        _mlp_kernel,
        grid=(m // tm, d_ff // tf),
        in_specs=[
            pl.BlockSpec((tm, d), lambda i, f: (i, 0)),
            pl.BlockSpec((1, d), lambda i, f: (0, 0)),
            pl.BlockSpec((d, tf), lambda i, f: (0, f)),
            pl.BlockSpec((tf, d), lambda i, f: (f, 0)),
        ],
        out_specs=pl.BlockSpec((tm, d), lambda i, f: (i, 0)),
        out_shape=jax.ShapeDtypeStruct((m, d), F32),
        scratch_shapes=[pltpu.VMEM((tm, d), BF16)],
        compiler_params=_cparams(("parallel", "arbitrary")),
        name="mlp",
    )(x, gain.reshape(1, d), w_up, w_down)


def _layer(x, mem, mixer_norm, w_in, fox_f_bias, mlstm_i_bias, mlstm_f_bias, conv_w, conv_b, fox_q_norm,
           fox_k_norm, fox_out_norm, mlstm_out_norm, w_out, xattn_norm, mem_norm, w_xq, w_xkv, xq_norm,
           xk_norm, w_xo, mlp_norm, w_up, w_down, *, batch, s_len, n_mem):
    fox_w = FOX_HEADS * FOX_HEAD_DIM
    mqk_w = 2 * MLSTM_HEADS * MLSTM_QK_DIM
    mv_w = MLSTM_HEADS * MLSTM_V_DIM
    d_model = x.shape[1]

    o_ff = 3 * fox_w
    o_mqk = o_ff + FOX_HEADS
    o_mv = o_mqk + mqk_w
    o_mi = o_mv + mv_w
    o_mo = o_mi + 2 * MLSTM_HEADS
    w_main = jnp.concatenate([w_in[:, :o_ff], w_in[:, o_mqk:o_mi], w_in[:, o_mo:]], axis=1).astype(BF16)
    w_gates = jnp.concatenate([w_in[:, o_ff:o_mqk], w_in[:, o_mi:o_mo],
                               jnp.zeros((d_model, LANES - N_GATES), F32)], axis=1).astype(BF16)
    gate_bias = jnp.concatenate([fox_f_bias, mlstm_i_bias, mlstm_f_bias]).reshape(N_GATES, 1)

    proj, gates_raw = _norm_proj(x, mixer_norm, w_main, jnp.stack([fox_q_norm, fox_k_norm]),
                                 tm=1024, tn=fox_w, head_dim=FOX_HEAD_DIM, n_norm_tiles=2, w_gates=w_gates)
    grow, gcol = _gates(gates_raw, gate_bias, batch=batch, s_len=s_len)
    fo = _fox_attention(proj, gcol, grow, fox_out_norm, batch=batch, s_len=s_len)
    mh = _mlstm(proj, gcol, grow, conv_w, conv_b, mlstm_out_norm, batch=batch, s_len=s_len)
    x = _proj_residual([fo, mh], w_out.astype(BF16), x, tm=1024, tn=1024)

    xd = d_model // XATTN_HEADS
    q = _norm_proj(x, xattn_norm, w_xq.astype(BF16), xq_norm.reshape(1, xd),
                   tm=1024, tn=xd, head_dim=xd, n_norm_tiles=XATTN_HEADS)
    kv = _norm_proj(mem, mem_norm, w_xkv.astype(BF16), xk_norm.reshape(1, xd),
                    tm=batch * n_mem, tn=xd, head_dim=xd, n_norm_tiles=XATTN_HEADS)
    co = _xattn(q, kv, batch=batch, s_len=s_len, n_mem=n_mem, tq=1024)
    x = _proj_residual([co], w_xo.astype(BF16), x, tm=1024, tn=1024)

    return _mlp(x, mlp_norm, w_up.astype(BF16), w_down.astype(BF16), tm=512, tf=1024)


def kernel(x, mem, mixer_norm, w_in, fox_f_bias, mlstm_i_bias, mlstm_f_bias, conv_w, conv_b, fox_q_norm,
           fox_k_norm, fox_out_norm, mlstm_out_norm, w_out, xattn_norm, mem_norm, w_xq, w_xkv, xq_norm,
           xk_norm, w_xo, mlp_norm, w_up, w_down):
    batch, s_len, d_model = x.shape
    n_mem = mem.shape[1]
    depth = w_in.shape[0]
    h = x.reshape(batch * s_len, d_model)
    mem2 = mem.reshape(batch * n_mem, d_model)
    for l in range(depth):
        h = _layer(h, mem2, mixer_norm[l], w_in[l], fox_f_bias[l], mlstm_i_bias[l], mlstm_f_bias[l],
                   conv_w[l], conv_b[l], fox_q_norm[l], fox_k_norm[l], fox_out_norm[l], mlstm_out_norm[l],
                   w_out[l], xattn_norm[l], mem_norm[l], w_xq[l], w_xkv[l], xq_norm[l], xk_norm[l], w_xo[l],
                   mlp_norm[l], w_up[l], w_down[l], batch=batch, s_len=s_len, n_mem=n_mem)
    return h.reshape(batch, s_len, d_model)
```

```python
import functools

import jax
import jax.numpy as jnp
from jax import lax
from jax.experimental import pallas as pl
from jax.experimental.pallas import tpu as pltpu

F32 = jnp.float32
BF16 = jnp.bfloat16

EPS = 1e-6
FOX_HEADS = 8
FOX_HEAD_DIM = 128
MLSTM_HEADS = 4
MLSTM_QK_DIM = 128
MLSTM_V_DIM = 256
CONV_WIDTH = 4
XATTN_HEADS = 4
N_GATES = 16
LANES = 128
SUBLANES = 8
NEG_BIG = -1e30
LOG2E = 1.4426950408889634
FOX_AUG_DEPTH = 256

MLSTM_CHUNK = 256
FOX_BLOCK = 512
VMEM_LIMIT = 56 * 1024 * 1024


def _cparams(sem):
    return pltpu.CompilerParams(dimension_semantics=sem, vmem_limit_bytes=VMEM_LIMIT)


def _rms_rows(x, gain):
    ms = jnp.mean(x * x, axis=-1, keepdims=True)
    return x * lax.rsqrt(ms + EPS) * gain


def _sigmoid(x):
    return 1.0 / (1.0 + jnp.exp(-x))


def _split3(x):
    hi = x.astype(BF16).astype(F32)
    r = x - hi
    mid = r.astype(BF16).astype(F32)
    lo = (r - mid).astype(BF16).astype(F32)
    return hi, mid, lo


def _in_proj_kernel(x_ref, g_ref, w_ref, qk_gain_ref, wg_ref, out_ref, vt_ref, gates_ref, xn_ref):
    j = pl.program_id(1)

    @pl.when(j == 0)
    def _():
        xn_ref[...] = _rms_rows(x_ref[...], g_ref[...]).astype(BF16)
        gates_ref[...] = jnp.dot(xn_ref[...], wg_ref[...], preferred_element_type=F32)

    acc = jnp.dot(xn_ref[...], w_ref[...], preferred_element_type=F32)

    @pl.when(j == 0)
    def _():
        vt_ref[...] = acc.T.astype(vt_ref.dtype)

    def head_norm(post_scale):
        gain = qk_gain_ref[0]
        for h in range(FOX_HEADS):
            sl = slice(h * FOX_HEAD_DIM, (h + 1) * FOX_HEAD_DIM)
            out_ref[:, sl] = (_rms_rows(acc[:, sl], gain) * post_scale).astype(out_ref.dtype)

    @pl.when(j == 1)
    def _():
        head_norm(FOX_HEAD_DIM ** -0.5 * LOG2E)

    @pl.when(j == 2)
    def _():
        head_norm(1.0)

    @pl.when(j >= 3)
    def _():
        out_ref[...] = acc.astype(out_ref.dtype)


def _in_proj(x, gain, w, qk_gains, w_gates, *, tm):
    m, k = x.shape
    tn = FOX_HEADS * FOX_HEAD_DIM
    n = w.shape[1]
    return pl.pallas_call(
        _in_proj_kernel,
        grid=(m // tm, n // tn),
        in_specs=[
            pl.BlockSpec((tm, k), lambda i, j: (i, 0)),
            pl.BlockSpec((1, k), lambda i, j: (0, 0)),
            pl.BlockSpec((k, tn), lambda i, j: (0, j)),
            pl.BlockSpec((1, 1, FOX_HEAD_DIM), lambda i, j: (jnp.clip(j - 1, 0, 1), 0, 0)),
            pl.BlockSpec((k, LANES), lambda i, j: (0, 0)),
        ],
        out_specs=[
            pl.BlockSpec((tm, tn), lambda i, j: (i, jnp.maximum(j - 1, 0))),
            pl.BlockSpec((tn, tm), lambda i, j: (0, i)),
            pl.BlockSpec((tm, LANES), lambda i, j: (i, 0)),
        ],
        out_shape=[
            jax.ShapeDtypeStruct((m, n - tn), BF16),
            jax.ShapeDtypeStruct((tn, m), BF16),
            jax.ShapeDtypeStruct((m, LANES), F32),
        ],
        scratch_shapes=[pltpu.VMEM((tm, k), BF16)],
        compiler_params=_cparams(("parallel", "arbitrary")),
        name="in_proj",
    )(x, gain.reshape(1, k), w, qk_gains.reshape(2, 1, FOX_HEAD_DIM), w_gates)


def _norm_proj_kernel(x_ref, g_ref, w_ref, hg_ref, out_ref, xn_ref, *, n_norm_tiles):
    j = pl.program_id(1)

    @pl.when(j == 0)
    def _():
        xn_ref[...] = _rms_rows(x_ref[...], g_ref[...]).astype(BF16)

    acc = jnp.dot(xn_ref[...], w_ref[...], preferred_element_type=F32)

    @pl.when(j < n_norm_tiles)
    def _():
        out_ref[...] = _rms_rows(acc, hg_ref[...]).astype(out_ref.dtype)

    @pl.when(j >= n_norm_tiles)
    def _():
        out_ref[...] = acc.astype(out_ref.dtype)


def _norm_proj(x, gain, w, head_gain, *, tm, n_norm_tiles):
    m, k = x.shape
    n = w.shape[1]
    tn = head_gain.shape[0]
    return pl.pallas_call(
        functools.partial(_norm_proj_kernel, n_norm_tiles=n_norm_tiles),
        grid=(m // tm, n // tn),
        in_specs=[
            pl.BlockSpec((tm, k), lambda i, j: (i, 0)),
            pl.BlockSpec((1, k), lambda i, j: (0, 0)),
            pl.BlockSpec((k, tn), lambda i, j: (0, j)),
            pl.BlockSpec((1, tn), lambda i, j: (0, 0)),
        ],
        out_specs=pl.BlockSpec((tm, tn), lambda i, j: (i, j)),
        out_shape=jax.ShapeDtypeStruct((m, n), BF16),
        scratch_shapes=[pltpu.VMEM((tm, k), BF16)],
        compiler_params=_cparams(("parallel", "arbitrary")),
        name="norm_proj",
    )(x, gain.reshape(1, k), w, head_gain.reshape(1, tn))


def _proj_residual_kernel(*refs, n_lhs):
    lhs = refs[:n_lhs]
    ws = refs[n_lhs:2 * n_lhs]
    res_ref, out_ref = refs[2 * n_lhs:]
    acc = res_ref[...]
    for a, w in zip(lhs, ws):
        acc = acc + jnp.dot(a[...], w[...], preferred_element_type=F32)
    out_ref[...] = acc


def _proj_residual(lhs_list, w, residual, *, tm, tn):
    m, n = residual.shape
    n_lhs = len(lhs_list)
    kp = lhs_list[0].shape[1]
    in_specs = [pl.BlockSpec((tm, kp), lambda i, j: (i, 0)) for _ in lhs_list]
    in_specs += [pl.BlockSpec((kp, tn), functools.partial(lambda i, j, p: (p, j), p=p)) for p in range(n_lhs)]
    in_specs += [pl.BlockSpec((tm, tn), lambda i, j: (i, j))]
    return pl.pallas_call(
        functools.partial(_proj_residual_kernel, n_lhs=n_lhs),
        grid=(m // tm, n // tn),
        in_specs=in_specs,
        out_specs=pl.BlockSpec((tm, tn), lambda i, j: (i, j)),
        out_shape=jax.ShapeDtypeStruct((m, n), F32),
        compiler_params=_cparams(("parallel", "parallel")),
        name="proj_residual",
    )(*lhs_list, *([w] * n_lhs), residual)


def _gates_kernel(g_ref, bias_ref, row_ref, col_ref, ka_ref):
    s_len = g_ref.shape[0]
    z = g_ref[...].T[0:N_GATES, :] + bias_ref[...]
    row = lax.broadcasted_iota(jnp.int32, z.shape, 0)
    is_forget = (row < FOX_HEADS) | (row >= FOX_HEADS + MLSTM_HEADS)
    logsig = jnp.minimum(z, 0.0) - jnp.log1p(jnp.exp(-jnp.abs(z)))
    c = jnp.where(is_forget, logsig, 0.0)
    lane = lax.broadcasted_iota(jnp.int32, z.shape, 1)
    shift = 1
    while shift < s_len:
        c = c + jnp.where(lane >= shift, pltpu.roll(c, shift, axis=1), 0.0)
        shift *= 2
    vals = jnp.where(is_forget, c, z)
    row_ref[...] = vals
    padded = jnp.concatenate([vals, jnp.zeros((LANES - N_GATES, s_len), F32)], axis=0)
    col_ref[...] = padded.T
    hi, mid, lo = _split3(c[0:FOX_HEADS, :] * LOG2E)
    aug = jnp.concatenate([jnp.ones((FOX_HEADS, s_len), F32), -hi, -mid, -lo,
                           jnp.zeros((LANES - 4 * FOX_HEADS, s_len), F32)], axis=0)
    ka_ref[...] = aug.T.astype(ka_ref.dtype)


def _gates(gates_raw, bias, *, batch, s_len):
    return pl.pallas_call(
        _gates_kernel,
        grid=(batch,),
        in_specs=[pl.BlockSpec((s_len, LANES), lambda b: (b, 0)),
                  pl.BlockSpec((N_GATES, 1), lambda b: (0, 0))],
        out_specs=[pl.BlockSpec((None, N_GATES, s_len), lambda b: (b, 0, 0)),
                   pl.BlockSpec((s_len, LANES), lambda b: (b, 0)),
                   pl.BlockSpec((s_len, LANES), lambda b: (b, 0))],
        out_shape=[jax.ShapeDtypeStruct((batch, N_GATES, s_len), F32),
                   jax.ShapeDtypeStruct((batch * s_len, LANES), F32),
                   jax.ShapeDtypeStruct((batch * s_len, LANES), BF16)],
        compiler_params=_cparams(("parallel",)),
        name="gates",
    )(gates_raw, bias)


def _fox_kernel(q_ref, k_ref, vt_ref, ka_ref, grow_ref, on_ref, o_ref, qt_ref, m_ref, l_ref, acc_ref):
    qi = pl.program_id(1)
    ki = pl.program_id(2)
    tq, tk = q_ref.shape[0], k_ref.shape[0]
    d = FOX_HEAD_DIM

    @pl.when(ki == 0)
    def _():
        m_ref[...] = jnp.full(m_ref.shape, NEG_BIG, F32)
        l_ref[...] = jnp.zeros(l_ref.shape, F32)
        acc_ref[...] = jnp.zeros(acc_ref.shape, F32)
        sub = lax.broadcasted_iota(jnp.int32, (SUBLANES, tq), 0)
        for h in range(FOX_HEADS):
            hi, mid, lo = _split3(grow_ref[h:h + 1, :] * LOG2E)
            c_rows = jnp.where(sub == 0, hi, jnp.where(sub == 1, mid, jnp.where(sub == 2, lo, 0.0)))
            pick = jnp.where(sub == h, 1.0, 0.0)
            qt = q_ref[:, h * d:(h + 1) * d].astype(F32).T
            pad = jnp.zeros((FOX_AUG_DEPTH - d - 4 * SUBLANES, tq), F32)
            qt_ref[h] = jnp.concatenate([qt, c_rows, pick, pick, pick, pad], axis=0).astype(BF16)

    def step(diagonal):
        if diagonal:
            keep = (lax.broadcasted_iota(jnp.int32, (tk, tq), 0)
                    <= lax.broadcasted_iota(jnp.int32, (tk, tq), 1))
        ka = ka_ref[...]
        for h in range(FOX_HEADS):
            sl = slice(h * d, (h + 1) * d)
            k_aug = jnp.concatenate([k_ref[:, sl], ka], axis=1)
            st = jnp.dot(k_aug, qt_ref[h], preferred_element_type=F32)
            if diagonal:
                st = jnp.where(keep, st, NEG_BIG)
            m_prev = m_ref[h:h + 1, :]
            m_new = jnp.maximum(m_prev, jnp.max(st, axis=0, keepdims=True))
            alpha = jnp.exp2(m_prev - m_new)
            p = jnp.exp2(st - m_new)
            l_ref[h:h + 1, :] = alpha * l_ref[h:h + 1, :] + jnp.sum(p, axis=0, keepdims=True)
            acc_ref[h] = alpha * acc_ref[h] + jnp.dot(vt_ref[sl, :], p.astype(BF16),
                                                      preferred_element_type=F32)
            m_ref[h:h + 1, :] = m_new

    @pl.when(ki < qi)
    def _():
        step(False)

    @pl.when(ki == qi)
    def _():
        step(True)
        for h in range(FOX_HEADS):
            sl = slice(h * d, (h + 1) * d)
            o = (acc_ref[h] / l_ref[h:h + 1, :]).T
            o_ref[:, sl] = _rms_rows(o, on_ref[:, sl]).astype(o_ref.dtype)


def _fox_attention(proj, vt, ka, grow, out_norm, *, batch, s_len):
    t = FOX_BLOCK
    nq = s_len // t
    width = FOX_HEADS * FOX_HEAD_DIM

    def k_row(b, qi, ki):
        return b * nq + jnp.minimum(ki, qi)

    return pl.pallas_call(
        _fox_kernel,
        grid=(batch, nq, nq),
        in_specs=[
            pl.BlockSpec((t, width), lambda b, qi, ki: (b * nq + qi, 0)),
            pl.BlockSpec((t, width), lambda b, qi, ki: (k_row(b, qi, ki), 1)),
            pl.BlockSpec((width, t), lambda b, qi, ki: (0, k_row(b, qi, ki))),
            pl.BlockSpec((t, LANES), lambda b, qi, ki: (k_row(b, qi, ki), 0)),
            pl.BlockSpec((None, N_GATES, t), lambda b, qi, ki: (b, 0, qi)),
            pl.BlockSpec((1, width), lambda b, qi, ki: (0, 0)),
        ],
        out_specs=pl.BlockSpec((t, width), lambda b, qi, ki: (b * nq + qi, 0)),
        out_shape=jax.ShapeDtypeStruct((batch * s_len, width), BF16),
        scratch_shapes=[
            pltpu.VMEM((FOX_HEADS, FOX_AUG_DEPTH, t), BF16),
            pltpu.VMEM((FOX_HEADS, t), F32),
            pltpu.VMEM((FOX_HEADS, t), F32),
            pltpu.VMEM((FOX_HEADS, FOX_HEAD_DIM, t), F32),
        ],
        compiler_params=_cparams(("parallel", "parallel", "arbitrary")),
        name="fox_attention",
    )(proj, proj, vt, ka, grow, out_norm.reshape(1, width))


def _mlstm_kernel(mqk_ref, mv_ref, mo_ref, gcol_ref, grow_ref, cw_ref, cb_ref, on_ref, out_ref,
                  ct_ref, n_ref, mg_ref, ext_ref):
    c = pl.program_id(1)
    length = mqk_ref.shape[0]
    dk, dv = MLSTM_QK_DIM, MLSTM_V_DIM
    tail = SUBLANES

    @pl.when(c == 0)
    def _():
        ct_ref[...] = jnp.zeros(ct_ref.shape, F32)
        n_ref[...] = jnp.zeros(n_ref.shape, F32)
        mg_ref[...] = jnp.zeros(mg_ref.shape, F32)
        ext_ref[0:tail, :] = jnp.zeros((tail, ext_ref.shape[1]), F32)

    u = mqk_ref[...].astype(F32)
    ext_ref[tail:, :] = u
    y = cb_ref[...]
    for j in range(CONV_WIDTH):
        start = tail - (CONV_WIDTH - 1) + j
        y = y + cw_ref[j:j + 1, :] * ext_ref[start:start + length, :]
    ext_ref[0:tail, :] = u[length - tail:, :]
    qk = y * _sigmoid(y)

    keep = (lax.broadcasted_iota(jnp.int32, (length, length), 0)
            >= lax.broadcasted_iota(jnp.int32, (length, length), 1))
    n_qk = MLSTM_HEADS * dk
    for h in range(MLSTM_HEADS):
        q = qk[:, h * dk:(h + 1) * dk]
        k = qk[:, n_qk + h * dk:n_qk + (h + 1) * dk] * (dk ** -0.5)
        qb, kb = q.astype(BF16), k.astype(BF16)
        v = mv_ref[:, h * dv:(h + 1) * dv]
        gi, gf = FOX_HEADS + h, FOX_HEADS + MLSTM_HEADS + h
        u_row = grow_ref[gi:gi + 1, :] - grow_ref[gf:gf + 1, :]
        f_col = gcol_ref[:, gf:gf + 1]
        u_col = gcol_ref[:, gi:gi + 1] - f_col
        mg = mg_ref[h:h + 1, 0:1]

        um = jnp.where(keep, u_row, NEG_BIG)
        a = jnp.maximum(jnp.max(um, axis=-1, keepdims=True), mg)
        w = jnp.exp(um - a)
        g = jnp.exp(mg - a)
        sqk = lax.dot_general(qb, kb, (((1,), (1,)), ((), ())), preferred_element_type=F32) * w
        inter = jnp.dot(qb, ct_ref[h].astype(BF16), preferred_element_type=F32)
        num = jnp.dot(sqk.astype(BF16), v, preferred_element_type=F32) + g * inter
        n_row = n_ref[h:h + 1, :]
        den = jnp.sum(sqk, axis=-1, keepdims=True) + g * jnp.sum(q * n_row, axis=-1, keepdims=True)
        hid = num / jnp.maximum(jnp.abs(den), jnp.exp(-(f_col + a)))

        sl = slice(h * dv, (h + 1) * dv)
        gate = _sigmoid(mo_ref[:, sl].astype(F32))
        out_ref[:, sl] = (_rms_rows(hid, on_ref[:, sl]) * gate).astype(out_ref.dtype)

        mg_new = jnp.maximum(mg, jnp.max(u_row, axis=-1, keepdims=True))
        decay = jnp.exp(mg - mg_new)
        wa = jnp.exp(u_col - mg_new)
        wav = (wa * v.astype(F32)).astype(BF16)
        ct_ref[h] = decay * ct_ref[h] + jnp.dot(k.T.astype(BF16), wav, preferred_element_type=F32)
        n_ref[h:h + 1, :] = decay * n_row + jnp.sum(wa * k, axis=0, keepdims=True)
        mg_ref[h:h + 1, :] = jnp.broadcast_to(mg_new, (1, LANES))


def _mlstm(proj, gcol, grow, conv_w, conv_b, out_norm, *, batch, s_len):
    length = MLSTM_CHUNK
    nc = s_len // length
    width = MLSTM_HEADS * MLSTM_V_DIM
    qk_width = 2 * MLSTM_HEADS * MLSTM_QK_DIM
    return pl.pallas_call(
        _mlstm_kernel,
        grid=(batch, nc),
        in_specs=[
            pl.BlockSpec((length, qk_width), lambda b, c: (b * nc + c, 2)),
            pl.BlockSpec((length, width), lambda b, c: (b * nc + c, 3)),
            pl.BlockSpec((length, width), lambda b, c: (b * nc + c, 4)),
            pl.BlockSpec((length, LANES), lambda b, c: (b * nc + c, 0)),
            pl.BlockSpec((None, N_GATES, length), lambda b, c: (b, 0, c)),
            pl.BlockSpec((CONV_WIDTH, qk_width), lambda b, c: (0, 0)),
            pl.BlockSpec((1, qk_width), lambda b, c: (0, 0)),
            pl.BlockSpec((1, width), lambda b, c: (0, 0)),
        ],
        out_specs=pl.BlockSpec((length, width), lambda b, c: (b * nc + c, 0)),
        out_shape=jax.ShapeDtypeStruct((batch * s_len, width), BF16),
        scratch_shapes=[
            pltpu.VMEM((MLSTM_HEADS, MLSTM_QK_DIM, MLSTM_V_DIM), F32),
            pltpu.VMEM((SUBLANES, MLSTM_QK_DIM), F32),
            pltpu.VMEM((SUBLANES, LANES), F32),
            pltpu.VMEM((length + SUBLANES, qk_width), F32),
        ],
        compiler_params=_cparams(("parallel", "arbitrary")),
        name="mlstm",
    )(proj, proj, proj, gcol, grow, conv_w, conv_b.reshape(1, qk_width), out_norm.reshape(1, width))


def _xattn_kernel(q_ref, k_ref, v_ref, o_ref):
    d = q_ref.shape[1] // XATTN_HEADS
    scale = d ** -0.5
    for h in range(XATTN_HEADS):
        sl = slice(h * d, (h + 1) * d)
        s = lax.dot_general(q_ref[:, sl], k_ref[:, sl], (((1,), (1,)), ((), ())),
                            preferred_element_type=F32) * scale
        p = jnp.exp(s - jnp.max(s, axis=-1, keepdims=True))
        p = p / jnp.sum(p, axis=-1, keepdims=True)
        o_ref[:, sl] = jnp.dot(p.astype(BF16), v_ref[:, sl], preferred_element_type=F32).astype(o_ref.dtype)


def _xattn(q, kv, *, batch, s_len, n_mem, tq):
    d_model = q.shape[1]
    nq = s_len // tq
    return pl.pallas_call(
        _xattn_kernel,
        grid=(batch, nq),
        in_specs=[
            pl.BlockSpec((tq, d_model), lambda b, i: (b * nq + i, 0)),
            pl.BlockSpec((n_mem, d_model), lambda b, i: (b, 0)),
            pl.BlockSpec((n_mem, d_model), lambda b, i: (b, 1)),
        ],
        out_specs=pl.BlockSpec((tq, d_model), lambda b, i: (b * nq + i, 0)),
        out_shape=jax.ShapeDtypeStruct(q.shape, BF16),
        compiler_params=_cparams(("parallel", "parallel")),
        name="xattn",
    )(q, kv, kv)


def _mlp_kernel(x_ref, g_ref, wu_ref, wd_ref, o_ref, xn_ref):
    f = pl.program_id(1)

    @pl.when(f == 0)
    def _():
        x = x_ref[...]
        xn_ref[...] = _rms_rows(x, g_ref[...]).astype(BF16)
        o_ref[...] = x

    hid = jnp.dot(xn_ref[...], wu_ref[...], preferred_element_type=F32)
    hid = jnp.square(jnp.maximum(hid, 0.0)).astype(BF16)
    o_ref[...] += jnp.dot(hid, wd_ref[...], preferred_element_type=F32)


def _mlp(x, gain, w_up, w_down, *, tm, tf):
    m, d = x.shape
    d_ff = w_up.shape[1]
    return pl.pallas_call(
        _mlp_kernel,
        grid=(m // tm, d_ff // tf),
        in_specs=[
            pl.BlockSpec((tm, d), lambda i, f: (i, 0)),
            pl.BlockSpec((1, d), lambda i, f: (0, 0)),
            pl.BlockSpec((d, tf), lambda i, f: (0, f)),
            pl.BlockSpec((tf, d), lambda i, f: (f, 0)),
        ],
        out_specs=pl.BlockSpec((tm, d), lambda i, f: (i, 0)),
        out_shape=jax.ShapeDtypeStruct((m, d), F32),
        scratch_shapes=[pltpu.VMEM((tm, d), BF16)],
        compiler_params=_cparams(("parallel", "arbitrary")),
        name="mlp",
    )(x, gain.reshape(1, d), w_up, w_down)


def _layer(x, mem, mixer_norm, w_in, fox_f_bias, mlstm_i_bias, mlstm_f_bias, conv_w, conv_b, fox_q_norm,
           fox_k_norm, fox_out_norm, mlstm_out_norm, w_out, xattn_norm, mem_norm, w_xq, w_xkv, xq_norm,
           xk_norm, w_xo, mlp_norm, w_up, w_down, *, batch, s_len, n_mem):
    fox_w = FOX_HEADS * FOX_HEAD_DIM
    mqk_w = 2 * MLSTM_HEADS * MLSTM_QK_DIM
    mv_w = MLSTM_HEADS * MLSTM_V_DIM
    d_model = x.shape[1]

    o_ff = 3 * fox_w
    o_mqk = o_ff + FOX_HEADS
    o_mv = o_mqk + mqk_w
    o_mi = o_mv + mv_w
    o_mo = o_mi + 2 * MLSTM_HEADS
    w_main = jnp.concatenate([w_in[:, 2 * fox_w:o_ff], w_in[:, :2 * fox_w], w_in[:, o_mqk:o_mi], w_in[:, o_mo:]],
                             axis=1).astype(BF16)
    w_gates = jnp.concatenate([w_in[:, o_ff:o_mqk], w_in[:, o_mi:o_mo],
                               jnp.zeros((d_model, LANES - N_GATES), F32)], axis=1).astype(BF16)
    gate_bias = jnp.concatenate([fox_f_bias, mlstm_i_bias, mlstm_f_bias]).reshape(N_GATES, 1)

    proj, vt, gates_raw = _in_proj(x, mixer_norm, w_main, jnp.stack([fox_q_norm, fox_k_norm]), w_gates, tm=1024)
    grow, gcol, ka = _gates(gates_raw, gate_bias, batch=batch, s_len=s_len)
    fo = _fox_attention(proj, vt, ka, grow, fox_out_norm, batch=batch, s_len=s_len)
    mh = _mlstm(proj, gcol, grow, conv_w, conv_b, mlstm_out_norm, batch=batch, s_len=s_len)
    x = _proj_residual([fo, mh], w_out.astype(BF16), x, tm=1024, tn=1024)

    q = _norm_proj(x, xattn_norm, w_xq.astype(BF16), xq_norm, tm=1024, n_norm_tiles=XATTN_HEADS)
    kv = _norm_proj(mem, mem_norm, w_xkv.astype(BF16), xk_norm, tm=batch * n_mem, n_norm_tiles=XATTN_HEADS)
    co = _xattn(q, kv, batch=batch, s_len=s_len, n_mem=n_mem, tq=1024)
    x = _proj_residual([co], w_xo.astype(BF16), x, tm=1024, tn=1024)

    return _mlp(x, mlp_norm, w_up.astype(BF16), w_down.astype(BF16), tm=512, tf=1024)


def kernel(x, mem, mixer_norm, w_in, fox_f_bias, mlstm_i_bias, mlstm_f_bias, conv_w, conv_b, fox_q_norm,
           fox_k_norm, fox_out_norm, mlstm_out_norm, w_out, xattn_norm, mem_norm, w_xq, w_xkv, xq_norm,
           xk_norm, w_xo, mlp_norm, w_up, w_down):
    batch, s_len, d_model = x.shape
    n_mem = mem.shape[1]
    depth = w_in.shape[0]
    h = x.reshape(batch * s_len, d_model)
    mem2 = mem.reshape(batch * n_mem, d_model)
    for l in range(depth):
        h = _layer(h, mem2, mixer_norm[l], w_in[l], fox_f_bias[l], mlstm_i_bias[l], mlstm_f_bias[l],
                   conv_w[l], conv_b[l], fox_q_norm[l], fox_k_norm[l], fox_out_norm[l], mlstm_out_norm[l],
                   w_out[l], xattn_norm[l], mem_norm[l], w_xq[l], w_xkv[l], xq_norm[l], xk_norm[l], w_xo[l],
                   mlp_norm[l], w_up[l], w_down[l], batch=batch, s_len=s_len, n_mem=n_mem)
    return h.reshape(batch, s_len, d_model)
```

```python
import functools

import jax
import jax.numpy as jnp
from jax import lax
from jax.experimental import pallas as pl
from jax.experimental.pallas import tpu as pltpu

F32 = jnp.float32
BF16 = jnp.bfloat16

EPS = 1e-6
FOX_HEADS = 8
FOX_HEAD_DIM = 128
MLSTM_HEADS = 4
MLSTM_QK_DIM = 128
MLSTM_V_DIM = 256
CONV_WIDTH = 4
XATTN_HEADS = 4
N_GATES = 16
LANES = 128
SUBLANES = 8
NEG_BIG = -1e30
LOG2E = 1.4426950408889634
FOX_AUG_DEPTH = 256

MLSTM_CHUNK = 256
FOX_BLOCK = 512
VMEM_LIMIT = 56 * 1024 * 1024


def _cparams(sem):
    return pltpu.CompilerParams(dimension_semantics=sem, vmem_limit_bytes=VMEM_LIMIT)


def _rms_rows(x, gain):
    ms = jnp.mean(x * x, axis=-1, keepdims=True)
    return x * lax.rsqrt(ms + EPS) * gain


def _sigmoid(x):
    return 1.0 / (1.0 + jnp.exp(-x))


def _split3(x):
    hi = x.astype(BF16).astype(F32)
    r = x - hi
    mid = r.astype(BF16).astype(F32)
    lo = (r - mid).astype(BF16).astype(F32)
    return hi, mid, lo


def _resident(block_shape):
    return pl.BlockSpec(block_shape, lambda i: (0,) * len(block_shape), pipeline_mode=pl.Buffered(1))


def _tile_pipeline(n_tiles, matmul, epilogue):
    pending = matmul(0)
    for j in range(1, n_tiles):
        nxt = matmul(j)
        epilogue(j - 1, pending)
        pending = nxt
    epilogue(n_tiles - 1, pending)


def _in_proj_kernel(x_ref, g_ref, w_ref, qk_gain_ref, wg_ref, out_ref, vt_ref, gates_ref):
    tn = FOX_HEADS * FOX_HEAD_DIM
    xn = _rms_rows(x_ref[...], g_ref[...]).astype(BF16)

    def matmul(j):
        return jnp.dot(xn, w_ref[:, j * tn:(j + 1) * tn], preferred_element_type=F32)

    def head_norm(acc, gain, post_scale, col0):
        for h in range(FOX_HEADS):
            sl = slice(h * FOX_HEAD_DIM, (h + 1) * FOX_HEAD_DIM)
            y = _rms_rows(acc[:, sl], gain) * post_scale
            out_ref[:, col0 + h * FOX_HEAD_DIM:col0 + (h + 1) * FOX_HEAD_DIM] = y.astype(out_ref.dtype)

    def epilogue(j, acc):
        if j == 0:
            vt_ref[...] = acc.T.astype(vt_ref.dtype)
        elif j == 1:
            head_norm(acc, qk_gain_ref[0], FOX_HEAD_DIM ** -0.5 * LOG2E, 0)
        elif j == 2:
            head_norm(acc, qk_gain_ref[1], 1.0, tn)
        else:
            out_ref[:, (j - 1) * tn:j * tn] = acc.astype(out_ref.dtype)

    gates_ref[...] = jnp.dot(xn, wg_ref[...], preferred_element_type=F32)
    _tile_pipeline(w_ref.shape[1] // tn, matmul, epilogue)


def _in_proj(x, gain, w, qk_gains, w_gates, *, tm):
    m, k = x.shape
    tn = FOX_HEADS * FOX_HEAD_DIM
    n = w.shape[1]
    return pl.pallas_call(
        _in_proj_kernel,
        grid=(m // tm,),
        in_specs=[
            pl.BlockSpec((tm, k), lambda i: (i, 0)),
            _resident((1, k)),
            _resident((k, n)),
            _resident((2, 1, FOX_HEAD_DIM)),
            _resident((k, LANES)),
        ],
        out_specs=[
            pl.BlockSpec((tm, n - tn), lambda i: (i, 0)),
            pl.BlockSpec((tn, tm), lambda i: (0, i)),
            pl.BlockSpec((tm, LANES), lambda i: (i, 0)),
        ],
        out_shape=[
            jax.ShapeDtypeStruct((m, n - tn), BF16),
            jax.ShapeDtypeStruct((tn, m), BF16),
            jax.ShapeDtypeStruct((m, LANES), F32),
        ],
        compiler_params=_cparams(("parallel",)),
        name="in_proj",
    )(x, gain.reshape(1, k), w, qk_gains.reshape(2, 1, FOX_HEAD_DIM), w_gates)


def _norm_proj_kernel(x_ref, g_ref, w_ref, hg_ref, out_ref, *, n_norm_tiles):
    tn = hg_ref.shape[1]
    xn = _rms_rows(x_ref[...], g_ref[...]).astype(BF16)

    def matmul(j):
        return jnp.dot(xn, w_ref[:, j * tn:(j + 1) * tn], preferred_element_type=F32)

    def epilogue(j, acc):
        y = _rms_rows(acc, hg_ref[...]) if j < n_norm_tiles else acc
        out_ref[:, j * tn:(j + 1) * tn] = y.astype(out_ref.dtype)

    _tile_pipeline(w_ref.shape[1] // tn, matmul, epilogue)


def _norm_proj(x, gain, w, head_gain, *, tm, n_norm_tiles):
    m, k = x.shape
    n = w.shape[1]
    tn = head_gain.shape[0]
    return pl.pallas_call(
        functools.partial(_norm_proj_kernel, n_norm_tiles=n_norm_tiles),
        grid=(m // tm,),
        in_specs=[
            pl.BlockSpec((tm, k), lambda i: (i, 0)),
            _resident((1, k)),
            _resident((k, n)),
            _resident((1, tn)),
        ],
        out_specs=pl.BlockSpec((tm, n), lambda i: (i, 0)),
        out_shape=jax.ShapeDtypeStruct((m, n), BF16),
        compiler_params=_cparams(("parallel",)),
        name="norm_proj",
    )(x, gain.reshape(1, k), w, head_gain.reshape(1, tn))


def _proj_residual_kernel(*refs, n_lhs, tn):
    lhs = refs[:n_lhs]
    w_ref, res_ref, out_ref = refs[n_lhs:]
    kp = lhs[0].shape[1]

    def matmul(j):
        cols = slice(j * tn, (j + 1) * tn)
        acc = jnp.dot(lhs[0][...], w_ref[0:kp, cols], preferred_element_type=F32)
        for p in range(1, n_lhs):
            acc = acc + jnp.dot(lhs[p][...], w_ref[p * kp:(p + 1) * kp, cols], preferred_element_type=F32)
        return acc

    def epilogue(j, acc):
        cols = slice(j * tn, (j + 1) * tn)
        out_ref[:, cols] = res_ref[:, cols] + acc

    _tile_pipeline(w_ref.shape[1] // tn, matmul, epilogue)


def _proj_residual(lhs_list, w, residual, *, tm, tn):
    m, n = residual.shape
    n_lhs = len(lhs_list)
    kp = lhs_list[0].shape[1]
    in_specs = [pl.BlockSpec((tm, kp), lambda i: (i, 0)) for _ in lhs_list]
    in_specs += [_resident(w.shape), pl.BlockSpec((tm, n), lambda i: (i, 0))]
    return pl.pallas_call(
        functools.partial(_proj_residual_kernel, n_lhs=n_lhs, tn=tn),
        grid=(m // tm,),
        in_specs=in_specs,
        out_specs=pl.BlockSpec((tm, n), lambda i: (i, 0)),
        out_shape=jax.ShapeDtypeStruct((m, n), F32),
        compiler_params=_cparams(("parallel",)),
        name="proj_residual",
    )(*lhs_list, w, residual)


def _gates_kernel(g_ref, bias_ref, row_ref, col_ref, ka_ref):
    s_len = g_ref.shape[0]
    z = g_ref[...].T[0:N_GATES, :] + bias_ref[...]
    row = lax.broadcasted_iota(jnp.int32, z.shape, 0)
    is_forget = (row < FOX_HEADS) | (row >= FOX_HEADS + MLSTM_HEADS)
    logsig = jnp.minimum(z, 0.0) - jnp.log1p(jnp.exp(-jnp.abs(z)))
    c = jnp.where(is_forget, logsig, 0.0)
    lane = lax.broadcasted_iota(jnp.int32, z.shape, 1)
    shift = 1
    while shift < s_len:
        c = c + jnp.where(lane >= shift, pltpu.roll(c, shift, axis=1), 0.0)
        shift *= 2
    vals = jnp.where(is_forget, c, z)
    row_ref[...] = vals
    padded = jnp.concatenate([vals, jnp.zeros((LANES - N_GATES, s_len), F32)], axis=0)
    col_ref[...] = padded.T
    hi, mid, lo = _split3(c[0:FOX_HEADS, :] * LOG2E)
    aug = jnp.concatenate([jnp.ones((FOX_HEADS, s_len), F32), -hi, -mid, -lo,
                           jnp.zeros((LANES - 4 * FOX_HEADS, s_len), F32)], axis=0)
    ka_ref[...] = aug.T.astype(ka_ref.dtype)


def _gates(gates_raw, bias, *, batch, s_len):
    return pl.pallas_call(
        _gates_kernel,
        grid=(batch,),
        in_specs=[pl.BlockSpec((s_len, LANES), lambda b: (b, 0)),
                  pl.BlockSpec((N_GATES, 1), lambda b: (0, 0))],
        out_specs=[pl.BlockSpec((None, N_GATES, s_len), lambda b: (b, 0, 0)),
                   pl.BlockSpec((s_len, LANES), lambda b: (b, 0)),
                   pl.BlockSpec((s_len, LANES), lambda b: (b, 0))],
        out_shape=[jax.ShapeDtypeStruct((batch, N_GATES, s_len), F32),
                   jax.ShapeDtypeStruct((batch * s_len, LANES), F32),
                   jax.ShapeDtypeStruct((batch * s_len, LANES), BF16)],
        compiler_params=_cparams(("parallel",)),
        name="gates",
    )(gates_raw, bias)


def _fox_kernel(qi_ref, ki_ref, q_ref, k_ref, vt_ref, ka_ref, grow_ref, on_ref, o_ref, qt_ref, m_ref, l_ref,
                acc_ref):
    pair = pl.program_id(1)
    qi = qi_ref[pair]
    ki = ki_ref[pair]
    tq, tk = q_ref.shape[0], k_ref.shape[0]
    d = FOX_HEAD_DIM

    @pl.when(ki == 0)
    def _():
        m_ref[...] = jnp.full(m_ref.shape, NEG_BIG, F32)
        l_ref[...] = jnp.zeros(l_ref.shape, F32)
        acc_ref[...] = jnp.zeros(acc_ref.shape, F32)
        sub = lax.broadcasted_iota(jnp.int32, (SUBLANES, tq), 0)
        for h in range(FOX_HEADS):
            hi, mid, lo = _split3(grow_ref[h:h + 1, :] * LOG2E)
            c_rows = jnp.where(sub == 0, hi, jnp.where(sub == 1, mid, jnp.where(sub == 2, lo, 0.0)))
            pick = jnp.where(sub == h, 1.0, 0.0)
            qt = q_ref[:, h * d:(h + 1) * d].astype(F32).T
            pad = jnp.zeros((FOX_AUG_DEPTH - d - 4 * SUBLANES, tq), F32)
            qt_ref[h] = jnp.concatenate([qt, c_rows, pick, pick, pick, pad], axis=0).astype(BF16)

    def step(diagonal):
        if diagonal:
            keep = (lax.broadcasted_iota(jnp.int32, (tk, tq), 0)
                    <= lax.broadcasted_iota(jnp.int32, (tk, tq), 1))
        ka = ka_ref[...]

        def scores(h):
            k_aug = jnp.concatenate([k_ref[:, h * d:(h + 1) * d], ka], axis=1)
            st = jnp.dot(k_aug, qt_ref[h], preferred_element_type=F32)
            if diagonal:
                st = jnp.where(keep, st, NEG_BIG)
            m_prev = m_ref[h:h + 1, :]
            m_new = jnp.maximum(m_prev, jnp.max(st, axis=0, keepdims=True))
            m_ref[h:h + 1, :] = m_new
            return st, m_new, jnp.exp2(m_prev - m_new)

        def accumulate(h, st, m_new, alpha):
            p = jnp.exp2(st - m_new)
            l_ref[h:h + 1, :] = alpha * l_ref[h:h + 1, :] + jnp.sum(p, axis=0, keepdims=True)
            acc_ref[h] = alpha * acc_ref[h] + jnp.dot(vt_ref[h * d:(h + 1) * d, :], p.astype(BF16),
                                                      preferred_element_type=F32)

        lookahead = 2
        pending = [scores(h) for h in range(lookahead)]
        for h in range(FOX_HEADS):
            if h + lookahead < FOX_HEADS:
                pending.append(scores(h + lookahead))
            accumulate(h, *pending.pop(0))

    @pl.when(ki < qi)
    def _():
        step(False)

    @pl.when(ki == qi)
    def _():
        step(True)
        for h in range(FOX_HEADS):
            sl = slice(h * d, (h + 1) * d)
            o = (acc_ref[h] / l_ref[h:h + 1, :]).T
            o_ref[:, sl] = _rms_rows(o, on_ref[:, sl]).astype(o_ref.dtype)


def _fox_attention(proj, vt, ka, grow, out_norm, *, batch, s_len):
    t = FOX_BLOCK
    nq = s_len // t
    width = FOX_HEADS * FOX_HEAD_DIM
    pairs = [(qi, ki) for qi in range(nq) for ki in range(qi + 1)]
    qi_tab = jnp.asarray([p[0] for p in pairs], jnp.int32)
    ki_tab = jnp.asarray([p[1] for p in pairs], jnp.int32)

    def q_row(b, p, qi_tab, ki_tab):
        return b * nq + qi_tab[p]

    def k_row(b, p, qi_tab, ki_tab):
        return b * nq + ki_tab[p]

    grid_spec = pltpu.PrefetchScalarGridSpec(
        num_scalar_prefetch=2,
        grid=(batch, len(pairs)),
        in_specs=[
            pl.BlockSpec((t, width), lambda *a: (q_row(*a), 0)),
            pl.BlockSpec((t, width), lambda *a: (k_row(*a), 1)),
            pl.BlockSpec((width, t), lambda *a: (0, k_row(*a))),
            pl.BlockSpec((t, LANES), lambda *a: (k_row(*a), 0)),
            pl.BlockSpec((None, N_GATES, t), lambda b, p, qi_tab, ki_tab: (b, 0, qi_tab[p])),
            pl.BlockSpec((1, width), lambda *a: (0, 0)),
        ],
        out_specs=pl.BlockSpec((t, width), lambda *a: (q_row(*a), 0)),
        scratch_shapes=[
            pltpu.VMEM((FOX_HEADS, FOX_AUG_DEPTH, t), BF16),
            pltpu.VMEM((FOX_HEADS, t), F32),
            pltpu.VMEM((FOX_HEADS, t), F32),
            pltpu.VMEM((FOX_HEADS, FOX_HEAD_DIM, t), F32),
        ],
    )
    return pl.pallas_call(
        _fox_kernel,
        grid_spec=grid_spec,
        out_shape=jax.ShapeDtypeStruct((batch * s_len, width), BF16),
        compiler_params=_cparams(("parallel", "arbitrary")),
        name="fox_attention",
    )(qi_tab, ki_tab, proj, proj, vt, ka, grow, out_norm.reshape(1, width))


def _mlstm_kernel(mqk_ref, mv_ref, mo_ref, gcol_ref, grow_ref, cw_ref, cb_ref, on_ref, out_ref,
                  ct_ref, n_ref, mg_ref, ext_ref):
    c = pl.program_id(1)
    length = mqk_ref.shape[0]
    dk, dv = MLSTM_QK_DIM, MLSTM_V_DIM
    tail = SUBLANES

    @pl.when(c == 0)
    def _():
        ct_ref[...] = jnp.zeros(ct_ref.shape, F32)
        n_ref[...] = jnp.zeros(n_ref.shape, F32)
        mg_ref[...] = jnp.zeros(mg_ref.shape, F32)
        ext_ref[0:tail, :] = jnp.zeros((tail, ext_ref.shape[1]), F32)

    u = mqk_ref[...].astype(F32)
    ext_ref[tail:, :] = u
    y = cb_ref[...]
    for j in range(CONV_WIDTH):
        start = tail - (CONV_WIDTH - 1) + j
        y = y + cw_ref[j:j + 1, :] * ext_ref[start:start + length, :]
    ext_ref[0:tail, :] = u[length - tail:, :]
    qk = y * _sigmoid(y)

    keep = (lax.broadcasted_iota(jnp.int32, (length, length), 0)
            >= lax.broadcasted_iota(jnp.int32, (length, length), 1))
    n_qk = MLSTM_HEADS * dk
    for h in range(MLSTM_HEADS):
        q = qk[:, h * dk:(h + 1) * dk]
        k = qk[:, n_qk + h * dk:n_qk + (h + 1) * dk] * (dk ** -0.5)
        qb, kb = q.astype(BF16), k.astype(BF16)
        v = mv_ref[:, h * dv:(h + 1) * dv]
        gi, gf = FOX_HEADS + h, FOX_HEADS + MLSTM_HEADS + h
        u_row = grow_ref[gi:gi + 1, :] - grow_ref[gf:gf + 1, :]
        f_col = gcol_ref[:, gf:gf + 1]
        u_col = gcol_ref[:, gi:gi + 1] - f_col
        mg = mg_ref[h:h + 1, 0:1]

        um = jnp.where(keep, u_row, NEG_BIG)
        a = jnp.maximum(jnp.max(um, axis=-1, keepdims=True), mg)
        w = jnp.exp(um - a)
        g = jnp.exp(mg - a)
        sqk = lax.dot_general(qb, kb, (((1,), (1,)), ((), ())), preferred_element_type=F32) * w
        inter = jnp.dot(qb, ct_ref[h].astype(BF16), preferred_element_type=F32)
        num = jnp.dot(sqk.astype(BF16), v, preferred_element_type=F32) + g * inter
        n_row = n_ref[h:h + 1, :]
        den = jnp.sum(sqk, axis=-1, keepdims=True) + g * jnp.sum(q * n_row, axis=-1, keepdims=True)
        hid = num / jnp.maximum(jnp.abs(den), jnp.exp(-(f_col + a)))

        sl = slice(h * dv, (h + 1) * dv)
        gate = _sigmoid(mo_ref[:, sl].astype(F32))
        out_ref[:, sl] = (_rms_rows(hid, on_ref[:, sl]) * gate).astype(out_ref.dtype)

        mg_new = jnp.maximum(mg, jnp.max(u_row, axis=-1, keepdims=True))
        decay = jnp.exp(mg - mg_new)
        wa = jnp.exp(u_col - mg_new)
        wav = (wa * v.astype(F32)).astype(BF16)
        ct_ref[h] = decay * ct_ref[h] + jnp.dot(k.T.astype(BF16), wav, preferred_element_type=F32)
        n_ref[h:h + 1, :] = decay * n_row + jnp.sum(wa * k, axis=0, keepdims=True)
        mg_ref[h:h + 1, :] = jnp.broadcast_to(mg_new, (1, LANES))


def _mlstm(proj, gcol, grow, conv_w, conv_b, out_norm, *, batch, s_len):
    length = MLSTM_CHUNK
    nc = s_len // length
    width = MLSTM_HEADS * MLSTM_V_DIM
    qk_width = 2 * MLSTM_HEADS * MLSTM_QK_DIM
    return pl.pallas_call(
        _mlstm_kernel,
        grid=(batch, nc),
        in_specs=[
            pl.BlockSpec((length, qk_width), lambda b, c: (b * nc + c, 2)),
            pl.BlockSpec((length, width), lambda b, c: (b * nc + c, 3)),
            pl.BlockSpec((length, width), lambda b, c: (b * nc + c, 4)),
            pl.BlockSpec((length, LANES), lambda b, c: (b * nc + c, 0)),
            pl.BlockSpec((None, N_GATES, length), lambda b, c: (b, 0, c)),
            pl.BlockSpec((CONV_WIDTH, qk_width), lambda b, c: (0, 0)),
            pl.BlockSpec((1, qk_width), lambda b, c: (0, 0)),
            pl.BlockSpec((1, width), lambda b, c: (0, 0)),
        ],
        out_specs=pl.BlockSpec((length, width), lambda b, c: (b * nc + c, 0)),
        out_shape=jax.ShapeDtypeStruct((batch * s_len, width), BF16),
        scratch_shapes=[
            pltpu.VMEM((MLSTM_HEADS, MLSTM_QK_DIM, MLSTM_V_DIM), F32),
            pltpu.VMEM((SUBLANES, MLSTM_QK_DIM), F32),
            pltpu.VMEM((SUBLANES, LANES), F32),
            pltpu.VMEM((length + SUBLANES, qk_width), F32),
        ],
        compiler_params=_cparams(("parallel", "arbitrary")),
        name="mlstm",
    )(proj, proj, proj, gcol, grow, conv_w, conv_b.reshape(1, qk_width), out_norm.reshape(1, width))


def _xattn_kernel(q_ref, k_ref, v_ref, o_ref):
    d = q_ref.shape[1] // XATTN_HEADS
    scale = d ** -0.5
    for h in range(XATTN_HEADS):
        sl = slice(h * d, (h + 1) * d)
        s = lax.dot_general(q_ref[:, sl], k_ref[:, sl], (((1,), (1,)), ((), ())),
                            preferred_element_type=F32) * scale
        p = jnp.exp(s - jnp.max(s, axis=-1, keepdims=True))
        p = p / jnp.sum(p, axis=-1, keepdims=True)
        o_ref[:, sl] = jnp.dot(p.astype(BF16), v_ref[:, sl], preferred_element_type=F32).astype(o_ref.dtype)


def _xattn(q, kv, *, batch, s_len, n_mem, tq):
    d_model = q.shape[1]
    nq = s_len // tq
    return pl.pallas_call(
        _xattn_kernel,
        grid=(batch, nq),
        in_specs=[
            pl.BlockSpec((tq, d_model), lambda b, i: (b * nq + i, 0)),
            pl.BlockSpec((n_mem, d_model), lambda b, i: (b, 0)),
            pl.BlockSpec((n_mem, d_model), lambda b, i: (b, 1)),
        ],
        out_specs=pl.BlockSpec((tq, d_model), lambda b, i: (b * nq + i, 0)),
        out_shape=jax.ShapeDtypeStruct(q.shape, BF16),
        compiler_params=_cparams(("parallel", "parallel")),
        name="xattn",
    )(q, kv, kv)


def _mlp_kernel(x_ref, g_ref, wu_ref, wd_ref, o_ref, xn_ref):
    f = pl.program_id(1)

    @pl.when(f == 0)
    def _():
        x = x_ref[...]
        xn_ref[...] = _rms_rows(x, g_ref[...]).astype(BF16)
        o_ref[...] = x

    hid = jnp.dot(xn_ref[...], wu_ref[...], preferred_element_type=F32)
    hid = jnp.square(jnp.maximum(hid, 0.0)).astype(BF16)
    o_ref[...] += jnp.dot(hid, wd_ref[...], preferred_element_type=F32)


def _mlp(x, gain, w_up, w_down, *, tm, tf):
    m, d = x.shape
    d_ff = w_up.shape[1]
    return pl.pallas_call(
        _mlp_kernel,
        grid=(m // tm, d_ff // tf),
        in_specs=[
            pl.BlockSpec((tm, d), lambda i, f: (i, 0)),
            pl.BlockSpec((1, d), lambda i, f: (0, 0)),
            pl.BlockSpec((d, tf), lambda i, f: (0, f)),
            pl.BlockSpec((tf, d), lambda i, f: (f, 0)),
        ],
        out_specs=pl.BlockSpec((tm, d), lambda i, f: (i, 0)),
        out_shape=jax.ShapeDtypeStruct((m, d), F32),
        scratch_shapes=[pltpu.VMEM((tm, d), BF16)],
        compiler_params=_cparams(("parallel", "arbitrary")),
        name="mlp",
    )(x, gain.reshape(1, d), w_up, w_down)


def _layer(x, mem, mixer_norm, w_in, fox_f_bias, mlstm_i_bias, mlstm_f_bias, conv_w, conv_b, fox_q_norm,
           fox_k_norm, fox_out_norm, mlstm_out_norm, w_out, xattn_norm, mem_norm, w_xq, w_xkv, xq_norm,
           xk_norm, w_xo, mlp_norm, w_up, w_down, *, batch, s_len, n_mem):
    fox_w = FOX_HEADS * FOX_HEAD_DIM
    mqk_w = 2 * MLSTM_HEADS * MLSTM_QK_DIM
    mv_w = MLSTM_HEADS * MLSTM_V_DIM
    d_model = x.shape[1]

    o_ff = 3 * fox_w
    o_mqk = o_ff + FOX_HEADS
    o_mv = o_mqk + mqk_w
    o_mi = o_mv + mv_w
    o_mo = o_mi + 2 * MLSTM_HEADS
    w_main = jnp.concatenate([w_in[:, 2 * fox_w:o_ff], w_in[:, :2 * fox_w], w_in[:, o_mqk:o_mi], w_in[:, o_mo:]],
                             axis=1).astype(BF16)
    w_gates = jnp.concatenate([w_in[:, o_ff:o_mqk], w_in[:, o_mi:o_mo],
                               jnp.zeros((d_model, LANES - N_GATES), F32)], axis=1).astype(BF16)
    gate_bias = jnp.concatenate([fox_f_bias, mlstm_i_bias, mlstm_f_bias]).reshape(N_GATES, 1)

    proj, vt, gates_raw = _in_proj(x, mixer_norm, w_main, jnp.stack([fox_q_norm, fox_k_norm]), w_gates, tm=256)
    grow, gcol, ka = _gates(gates_raw, gate_bias, batch=batch, s_len=s_len)
    fo = _fox_attention(proj, vt, ka, grow, fox_out_norm, batch=batch, s_len=s_len)
    mh = _mlstm(proj, gcol, grow, conv_w, conv_b, mlstm_out_norm, batch=batch, s_len=s_len)
    x = _proj_residual([fo, mh], w_out.astype(BF16), x, tm=512, tn=512)

    q = _norm_proj(x, xattn_norm, w_xq.astype(BF16), xq_norm, tm=512, n_norm_tiles=XATTN_HEADS)
    kv = _norm_proj(mem, mem_norm, w_xkv.astype(BF16), xk_norm, tm=256, n_norm_tiles=XATTN_HEADS)
    co = _xattn(q, kv, batch=batch, s_len=s_len, n_mem=n_mem, tq=1024)
    x = _proj_residual([co], w_xo.astype(BF16), x, tm=512, tn=512)

    return _mlp(x, mlp_norm, w_up.astype(BF16), w_down.astype(BF16), tm=512, tf=1024)


def kernel(x, mem, mixer_norm, w_in, fox_f_bias, mlstm_i_bias, mlstm_f_bias, conv_w, conv_b, fox_q_norm,
           fox_k_norm, fox_out_norm, mlstm_out_norm, w_out, xattn_norm, mem_norm, w_xq, w_xkv, xq_norm,
           xk_norm, w_xo, mlp_norm, w_up, w_down):
    batch, s_len, d_model = x.shape
    n_mem = mem.shape[1]
    depth = w_in.shape[0]
    h = x.reshape(batch * s_len, d_model)
    mem2 = mem.reshape(batch * n_mem, d_model)
    for l in range(depth):
        h = _layer(h, mem2, mixer_norm[l], w_in[l], fox_f_bias[l], mlstm_i_bias[l], mlstm_f_bias[l],
                   conv_w[l], conv_b[l], fox_q_norm[l], fox_k_norm[l], fox_out_norm[l], mlstm_out_norm[l],
                   w_out[l], xattn_norm[l], mem_norm[l], w_xq[l], w_xkv[l], xq_norm[l], xk_norm[l], w_xo[l],
                   mlp_norm[l], w_up[l], w_down[l], batch=batch, s_len=s_len, n_mem=n_mem)
    return h.reshape(batch, s_len, d_model)
```

```python
import functools

import jax
import jax.numpy as jnp
from jax import lax
from jax.experimental import pallas as pl
from jax.experimental.pallas import tpu as pltpu

F32 = jnp.float32
BF16 = jnp.bfloat16

EPS = 1e-6
FOX_HEADS = 8
FOX_HEAD_DIM = 128
MLSTM_HEADS = 4
MLSTM_QK_DIM = 128
MLSTM_V_DIM = 256
CONV_WIDTH = 4
XATTN_HEADS = 4
N_GATES = 16
LANES = 128
SUBLANES = 8
NEG_BIG = -1e30
LOG2E = 1.4426950408889634
FOX_AUG_DEPTH = 256
FOX_ONES_ROWS = 16

MLSTM_CHUNK = 256
FOX_BLOCK = 512
IN_PROJ_TILE = 512
VMEM_LIMIT = 56 * 1024 * 1024


def _cparams(sem):
    return pltpu.CompilerParams(dimension_semantics=sem, vmem_limit_bytes=VMEM_LIMIT)


def _rms_rows(x, gain):
    ms = jnp.mean(x * x, axis=-1, keepdims=True)
    return x * lax.rsqrt(ms + EPS) * gain


def _sigmoid(x):
    return 1.0 / (1.0 + jnp.exp(-x))


def _split3(x):
    hi = x.astype(BF16).astype(F32)
    r = x - hi
    mid = r.astype(BF16).astype(F32)
    lo = (r - mid).astype(BF16).astype(F32)
    return hi, mid, lo


def _resident(block_shape):
    return pl.BlockSpec(block_shape, lambda i: (0,) * len(block_shape), pipeline_mode=pl.Buffered(1))


def _tile_pipeline(n_tiles, matmul, epilogue):
    pending = matmul(0)
    for j in range(1, n_tiles):
        nxt = matmul(j)
        epilogue(j - 1, pending)
        pending = nxt
    epilogue(n_tiles - 1, pending)


def _in_proj_kernel(x_ref, g_ref, w_ref, qk_gain_ref, wg_ref, cw_ref, cb_ref, out_ref, vt_ref, gates_ref, ext_ref, *,
                    tiles_per_seq):
    seg_w = FOX_HEADS * FOX_HEAD_DIM
    tn = IN_PROJ_TILE
    tm = x_ref.shape[0]
    tail = SUBLANES
    xn = _rms_rows(x_ref[...], g_ref[...]).astype(BF16)

    @pl.when(pl.program_id(0) % tiles_per_seq == 0)
    def _():
        ext_ref[0:tail, :] = jnp.zeros((tail, ext_ref.shape[1]), F32)

    def conv_silu(acc, off):
        cols = slice(off, off + tn)
        ext_ref[tail:, cols] = acc
        y = cb_ref[:, cols]
        for j in range(CONV_WIDTH):
            start = tail - (CONV_WIDTH - 1) + j
            y = y + cw_ref[j:j + 1, cols] * ext_ref[start:start + tm, cols]
        ext_ref[0:tail, cols] = acc[tm - tail:, :]
        qk = y * _sigmoid(y)
        if off >= MLSTM_HEADS * MLSTM_QK_DIM:
            qk = qk * MLSTM_QK_DIM ** -0.5
        out_ref[:, 2 * seg_w + off:2 * seg_w + off + tn] = qk.astype(out_ref.dtype)

    def matmul(j):
        return jnp.dot(xn, w_ref[:, j * tn:(j + 1) * tn], preferred_element_type=F32)

    def head_norm(acc, gain, post_scale, col0):
        for h in range(tn // FOX_HEAD_DIM):
            sl = slice(h * FOX_HEAD_DIM, (h + 1) * FOX_HEAD_DIM)
            y = _rms_rows(acc[:, sl], gain) * post_scale
            out_ref[:, col0 + h * FOX_HEAD_DIM:col0 + (h + 1) * FOX_HEAD_DIM] = y.astype(out_ref.dtype)

    def epilogue(j, acc):
        seg, off = divmod(j * tn, seg_w)
        if seg == 0:
            vt_ref[off:off + tn, :] = acc.T.astype(vt_ref.dtype)
        elif seg == 1:
            head_norm(acc, qk_gain_ref[0], FOX_HEAD_DIM ** -0.5 * LOG2E, off)
        elif seg == 2:
            head_norm(acc, qk_gain_ref[1], 1.0, seg_w + off)
        elif seg == 3:
            conv_silu(acc, off)
        else:
            out_ref[:, (seg - 1) * seg_w + off:(seg - 1) * seg_w + off + tn] = acc.astype(out_ref.dtype)

    gates_ref[...] = jnp.dot(xn, wg_ref[...], preferred_element_type=F32)
    _tile_pipeline(w_ref.shape[1] // tn, matmul, epilogue)


def _in_proj(x, gain, w, qk_gains, w_gates, conv_w, conv_b, *, tm, s_len):
    m, k = x.shape
    tn = FOX_HEADS * FOX_HEAD_DIM
    n = w.shape[1]
    return pl.pallas_call(
        functools.partial(_in_proj_kernel, tiles_per_seq=s_len // tm),
        grid=(m // tm,),
        in_specs=[
            pl.BlockSpec((tm, k), lambda i: (i, 0)),
            _resident((1, k)),
            _resident((k, n)),
            _resident((2, 1, FOX_HEAD_DIM)),
            _resident((k, LANES)),
            _resident((CONV_WIDTH, tn)),
            _resident((1, tn)),
        ],
        out_specs=[
            pl.BlockSpec((tm, n - tn), lambda i: (i, 0)),
            pl.BlockSpec((tn, tm), lambda i: (0, i)),
            pl.BlockSpec((tm, LANES), lambda i: (i, 0)),
        ],
        out_shape=[
            jax.ShapeDtypeStruct((m, n - tn), BF16),
            jax.ShapeDtypeStruct((tn, m), BF16),
            jax.ShapeDtypeStruct((m, LANES), F32),
        ],
        scratch_shapes=[pltpu.VMEM((tm + SUBLANES, tn), F32)],
        compiler_params=_cparams(("arbitrary",)),
        name="in_proj",
    )(x, gain.reshape(1, k), w, qk_gains.reshape(2, 1, FOX_HEAD_DIM), w_gates, conv_w, conv_b.reshape(1, tn))


def _norm_proj_kernel(x_ref, g_ref, w_ref, hg_ref, out_ref, *, n_norm_tiles):
    tn = hg_ref.shape[1]
    xn = _rms_rows(x_ref[...], g_ref[...]).astype(BF16)

    def matmul(j):
        return jnp.dot(xn, w_ref[:, j * tn:(j + 1) * tn], preferred_element_type=F32)

    def epilogue(j, acc):
        y = _rms_rows(acc, hg_ref[...]) if j < n_norm_tiles else acc
        out_ref[:, j * tn:(j + 1) * tn] = y.astype(out_ref.dtype)

    _tile_pipeline(w_ref.shape[1] // tn, matmul, epilogue)


def _norm_proj(x, gain, w, head_gain, *, tm, n_norm_tiles):
    m, k = x.shape
    n = w.shape[1]
    tn = head_gain.shape[0]
    return pl.pallas_call(
        functools.partial(_norm_proj_kernel, n_norm_tiles=n_norm_tiles),
        grid=(m // tm,),
        in_specs=[
            pl.BlockSpec((tm, k), lambda i: (i, 0)),
            _resident((1, k)),
            _resident((k, n)),
            _resident((1, tn)),
        ],
        out_specs=pl.BlockSpec((tm, n), lambda i: (i, 0)),
        out_shape=jax.ShapeDtypeStruct((m, n), BF16),
        compiler_params=_cparams(("parallel",)),
        name="norm_proj",
    )(x, gain.reshape(1, k), w, head_gain.reshape(1, tn))


def _proj_residual_kernel(*refs, n_lhs, tn):
    lhs = refs[:n_lhs]
    w_ref, res_ref, out_ref = refs[n_lhs:]
    kp = lhs[0].shape[1]

    def matmul(j):
        cols = slice(j * tn, (j + 1) * tn)
        acc = jnp.dot(lhs[0][...], w_ref[0:kp, cols], preferred_element_type=F32)
        for p in range(1, n_lhs):
            acc = acc + jnp.dot(lhs[p][...], w_ref[p * kp:(p + 1) * kp, cols], preferred_element_type=F32)
        return acc

    def epilogue(j, acc):
        cols = slice(j * tn, (j + 1) * tn)
        out_ref[:, cols] = res_ref[:, cols] + acc

    _tile_pipeline(w_ref.shape[1] // tn, matmul, epilogue)


def _proj_residual(lhs_list, w, residual, *, tm, tn):
    m, n = residual.shape
    n_lhs = len(lhs_list)
    kp = lhs_list[0].shape[1]
    in_specs = [pl.BlockSpec((tm, kp), lambda i: (i, 0)) for _ in lhs_list]
    in_specs += [_resident(w.shape), pl.BlockSpec((tm, n), lambda i: (i, 0))]
    return pl.pallas_call(
        functools.partial(_proj_residual_kernel, n_lhs=n_lhs, tn=tn),
        grid=(m // tm,),
        in_specs=in_specs,
        out_specs=pl.BlockSpec((tm, n), lambda i: (i, 0)),
        out_shape=jax.ShapeDtypeStruct((m, n), F32),
        compiler_params=_cparams(("parallel",)),
        name="proj_residual",
    )(*lhs_list, w, residual)


def _gates_kernel(g_ref, bias_ref, row_ref, col_ref, ka_ref):
    s_len = g_ref.shape[0]
    z = g_ref[...].T[0:N_GATES, :] + bias_ref[...]
    row = lax.broadcasted_iota(jnp.int32, z.shape, 0)
    is_forget = (row < FOX_HEADS) | (row >= FOX_HEADS + MLSTM_HEADS)
    logsig = jnp.minimum(z, 0.0) - jnp.log1p(jnp.exp(-jnp.abs(z)))
    c = jnp.where(is_forget, logsig, 0.0)
    lane = lax.broadcasted_iota(jnp.int32, z.shape, 1)
    shift = 1
    while shift < s_len:
        c = c + jnp.where(lane >= shift, pltpu.roll(c, shift, axis=1), 0.0)
        shift *= 2
    vals = jnp.where(is_forget, c, z)
    row_ref[...] = vals
    padded = jnp.concatenate([vals, jnp.zeros((LANES - N_GATES, s_len), F32)], axis=0)
    col_ref[...] = padded.T
    hi, mid, lo = _split3(c[0:FOX_HEADS, :] * LOG2E)
    aug = jnp.concatenate([jnp.ones((FOX_HEADS, s_len), F32), -hi, -mid, -lo,
                           jnp.zeros((LANES - 4 * FOX_HEADS, s_len), F32)], axis=0)
    ka_ref[...] = aug.T.astype(ka_ref.dtype)


def _gates(gates_raw, bias, *, batch, s_len):
    return pl.pallas_call(
        _gates_kernel,
        grid=(batch,),
        in_specs=[pl.BlockSpec((s_len, LANES), lambda b: (b, 0)),
                  pl.BlockSpec((N_GATES, 1), lambda b: (0, 0))],
        out_specs=[pl.BlockSpec((None, N_GATES, s_len), lambda b: (b, 0, 0)),
                   pl.BlockSpec((s_len, LANES), lambda b: (b, 0)),
                   pl.BlockSpec((s_len, LANES), lambda b: (b, 0))],
        out_shape=[jax.ShapeDtypeStruct((batch, N_GATES, s_len), F32),
                   jax.ShapeDtypeStruct((batch * s_len, LANES), F32),
                   jax.ShapeDtypeStruct((batch * s_len, LANES), BF16)],
        compiler_params=_cparams(("parallel",)),
        name="gates",
    )(gates_raw, bias)


def _fox_kernel(qi_ref, ki_ref, q_ref, k_ref, vt_ref, ka_ref, grow_ref, on_ref, o_ref, qt_ref, m_ref, acc_ref):
    pair = pl.program_id(1)
    qi = qi_ref[pair]
    ki = ki_ref[pair]
    tq, tk = q_ref.shape[0], k_ref.shape[0]
    d = FOX_HEAD_DIM

    @pl.when(ki == 0)
    def _():
        m_ref[...] = jnp.full(m_ref.shape, NEG_BIG, F32)
        acc_ref[...] = jnp.zeros(acc_ref.shape, F32)
        sub = lax.broadcasted_iota(jnp.int32, (SUBLANES, tq), 0)
        for h in range(FOX_HEADS):
            hi, mid, lo = _split3(grow_ref[h:h + 1, :] * LOG2E)
            c_rows = jnp.where(sub == 0, hi, jnp.where(sub == 1, mid, jnp.where(sub == 2, lo, 0.0)))
            pick = jnp.where(sub == h, 1.0, 0.0)
            qt = q_ref[:, h * d:(h + 1) * d].astype(F32).T
            pad = jnp.zeros((FOX_AUG_DEPTH - d - 4 * SUBLANES, tq), F32)
            qt_ref[h] = jnp.concatenate([qt, c_rows, pick, pick, pick, pad], axis=0).astype(BF16)

    def step(diagonal):
        if diagonal:
            keep = (lax.broadcasted_iota(jnp.int32, (tk, tq), 0)
                    <= lax.broadcasted_iota(jnp.int32, (tk, tq), 1))
        ka = ka_ref[...]
        ones = jnp.ones((FOX_ONES_ROWS, tk), BF16)

        def scores(h):
            k_aug = jnp.concatenate([k_ref[:, h * d:(h + 1) * d], ka], axis=1)
            st = jnp.dot(k_aug, qt_ref[h], preferred_element_type=F32)
            if diagonal:
                st = jnp.where(keep, st, NEG_BIG)
            m_prev = m_ref[h:h + 1, :]
            m_new = jnp.maximum(m_prev, jnp.max(st, axis=0, keepdims=True))
            m_ref[h:h + 1, :] = m_new
            return st, m_new, jnp.exp2(m_prev - m_new)

        def accumulate(h, st, m_new, alpha):
            p = jnp.exp2(st - m_new)
            vt_aug = jnp.concatenate([vt_ref[h * d:(h + 1) * d, :], ones], axis=0)
            acc_ref[h] = alpha * acc_ref[h] + jnp.dot(vt_aug, p.astype(BF16), preferred_element_type=F32)

        lookahead = 2
        pending = [scores(h) for h in range(lookahead)]
        for h in range(FOX_HEADS):
            if h + lookahead < FOX_HEADS:
                pending.append(scores(h + lookahead))
            accumulate(h, *pending.pop(0))

    @pl.when(ki < qi)
    def _():
        step(False)

    @pl.when(ki == qi)
    def _():
        step(True)
        for h in range(FOX_HEADS):
            sl = slice(h * d, (h + 1) * d)
            o = (acc_ref[h, 0:d, :] / acc_ref[h, d:d + 1, :]).T
            o_ref[:, sl] = _rms_rows(o, on_ref[:, sl]).astype(o_ref.dtype)


def _fox_attention(proj, vt, ka, grow, out_norm, *, batch, s_len):
    t = FOX_BLOCK
    nq = s_len // t
    width = FOX_HEADS * FOX_HEAD_DIM
    pairs = [(qi, ki) for qi in range(nq) for ki in range(qi + 1)]
    qi_tab = jnp.asarray([p[0] for p in pairs], jnp.int32)
    ki_tab = jnp.asarray([p[1] for p in pairs], jnp.int32)

    def q_row(b, p, qi_tab, ki_tab):
        return b * nq + qi_tab[p]

    def k_row(b, p, qi_tab, ki_tab):
        return b * nq + ki_tab[p]

    grid_spec = pltpu.PrefetchScalarGridSpec(
        num_scalar_prefetch=2,
        grid=(batch, len(pairs)),
        in_specs=[
            pl.BlockSpec((t, width), lambda *a: (q_row(*a), 0)),
            pl.BlockSpec((t, width), lambda *a: (k_row(*a), 1)),
            pl.BlockSpec((width, t), lambda *a: (0, k_row(*a))),
            pl.BlockSpec((t, LANES), lambda *a: (k_row(*a), 0)),
            pl.BlockSpec((None, N_GATES, t), lambda b, p, qi_tab, ki_tab: (b, 0, qi_tab[p])),
            pl.BlockSpec((1, width), lambda *a: (0, 0)),
        ],
        out_specs=pl.BlockSpec((t, width), lambda *a: (q_row(*a), 0)),
        scratch_shapes=[
            pltpu.VMEM((FOX_HEADS, FOX_AUG_DEPTH, t), BF16),
            pltpu.VMEM((FOX_HEADS, t), F32),
            pltpu.VMEM((FOX_HEADS, FOX_HEAD_DIM + FOX_ONES_ROWS, t), F32),
        ],
    )
    return pl.pallas_call(
        _fox_kernel,
        grid_spec=grid_spec,
        out_shape=jax.ShapeDtypeStruct((batch * s_len, width), BF16),
        compiler_params=_cparams(("parallel", "arbitrary")),
        name="fox_attention",
    )(qi_tab, ki_tab, proj, proj, vt, ka, grow, out_norm.reshape(1, width))


def _mlstm_kernel(mqk_ref, mv_ref, mo_ref, gcol_ref, grow_ref, on_ref, out_ref, ct_ref, mg_ref):
    c = pl.program_id(1)
    length = mqk_ref.shape[0]
    dk, dv = MLSTM_QK_DIM, MLSTM_V_DIM
    n_qk = MLSTM_HEADS * dk

    @pl.when(c == 0)
    def _():
        ct_ref[...] = jnp.zeros(ct_ref.shape, F32)
        mg_ref[...] = jnp.zeros(mg_ref.shape, F32)

    keep = (lax.broadcasted_iota(jnp.int32, (length, length), 0)
            >= lax.broadcasted_iota(jnp.int32, (length, length), 1))
    ones = jnp.ones((length, LANES), BF16)

    def scores(h):
        qb = mqk_ref[:, h * dk:(h + 1) * dk]
        kb = mqk_ref[:, n_qk + h * dk:n_qk + (h + 1) * dk]
        gi, gf = FOX_HEADS + h, FOX_HEADS + MLSTM_HEADS + h
        u_row = grow_ref[gi:gi + 1, :] - grow_ref[gf:gf + 1, :]
        mg = mg_ref[h:h + 1, 0:1]
        um = jnp.where(keep, u_row, NEG_BIG)
        a = jnp.maximum(jnp.max(um, axis=-1, keepdims=True), mg)
        sqk = lax.dot_general(qb, kb, (((1,), (1,)), ((), ())), preferred_element_type=F32) * jnp.exp(um - a)
        mg_new = jnp.maximum(mg, jnp.max(u_row, axis=-1, keepdims=True))
        return sqk, a, mg, mg_new

    def outputs(h, sqk, a, mg, mg_new):
        qb = mqk_ref[:, h * dk:(h + 1) * dk]
        kb = mqk_ref[:, n_qk + h * dk:n_qk + (h + 1) * dk]
        sl = slice(h * dv, (h + 1) * dv)
        v_aug = jnp.concatenate([mv_ref[:, sl], ones], axis=1)
        gi, gf = FOX_HEADS + h, FOX_HEADS + MLSTM_HEADS + h
        f_col = gcol_ref[:, gf:gf + 1]
        u_col = gcol_ref[:, gi:gi + 1] - f_col
        ct = ct_ref[h]
        inter = jnp.dot(qb, ct.astype(BF16), preferred_element_type=F32)
        nd = jnp.dot(sqk.astype(BF16), v_aug, preferred_element_type=F32) + jnp.exp(mg - a) * inter
        den = jnp.maximum(jnp.abs(nd[:, dv:dv + 1]), jnp.exp(-(f_col + a)))
        hid = nd[:, :dv] * (1.0 / den)
        gate = _sigmoid(mo_ref[:, sl].astype(F32))
        out_ref[:, sl] = (_rms_rows(hid, on_ref[:, sl]) * gate).astype(out_ref.dtype)

        wav = (jnp.exp(u_col - mg_new) * v_aug.astype(F32)).astype(BF16)
        kt = kb.astype(F32).T.astype(BF16)
        ct_ref[h] = jnp.exp(mg - mg_new) * ct + jnp.dot(kt, wav, preferred_element_type=F32)
        mg_ref[h:h + 1, :] = jnp.broadcast_to(mg_new, (1, LANES))

    pending = scores(0)
    for h in range(1, MLSTM_HEADS):
        nxt = scores(h)
        outputs(h - 1, *pending)
        pending = nxt
    outputs(MLSTM_HEADS - 1, *pending)


def _mlstm(proj, gcol, grow, out_norm, *, batch, s_len):
    length = MLSTM_CHUNK
    nc = s_len // length
    width = MLSTM_HEADS * MLSTM_V_DIM
    qk_width = 2 * MLSTM_HEADS * MLSTM_QK_DIM
    return pl.pallas_call(
        _mlstm_kernel,
        grid=(batch, nc),
        in_specs=[
            pl.BlockSpec((length, qk_width), lambda b, c: (b * nc + c, 2)),
            pl.BlockSpec((length, width), lambda b, c: (b * nc + c, 3)),
            pl.BlockSpec((length, width), lambda b, c: (b * nc + c, 4)),
            pl.BlockSpec((length, LANES), lambda b, c: (b * nc + c, 0)),
            pl.BlockSpec((None, N_GATES, length), lambda b, c: (b, 0, c)),
            pl.BlockSpec((1, width), lambda b, c: (0, 0)),
        ],
        out_specs=pl.BlockSpec((length, width), lambda b, c: (b * nc + c, 0)),
        out_shape=jax.ShapeDtypeStruct((batch * s_len, width), BF16),
        scratch_shapes=[
            pltpu.VMEM((MLSTM_HEADS, MLSTM_QK_DIM, MLSTM_V_DIM + LANES), F32),
            pltpu.VMEM((SUBLANES, LANES), F32),
        ],
        compiler_params=_cparams(("parallel", "arbitrary")),
        name="mlstm",
    )(proj, proj, proj, gcol, grow, out_norm.reshape(1, width))


def _xattn_kernel(q_ref, k_ref, v_ref, o_ref):
    d = q_ref.shape[1] // XATTN_HEADS
    scale = d ** -0.5
    for h in range(XATTN_HEADS):
        sl = slice(h * d, (h + 1) * d)
        s = lax.dot_general(q_ref[:, sl], k_ref[:, sl], (((1,), (1,)), ((), ())),
                            preferred_element_type=F32) * scale
        p = jnp.exp(s - jnp.max(s, axis=-1, keepdims=True))
        p = p / jnp.sum(p, axis=-1, keepdims=True)
        o_ref[:, sl] = jnp.dot(p.astype(BF16), v_ref[:, sl], preferred_element_type=F32).astype(o_ref.dtype)


def _xattn(q, kv, *, batch, s_len, n_mem, tq):
    d_model = q.shape[1]
    nq = s_len // tq
    return pl.pallas_call(
        _xattn_kernel,
        grid=(batch, nq),
        in_specs=[
            pl.BlockSpec((tq, d_model), lambda b, i: (b * nq + i, 0)),
            pl.BlockSpec((n_mem, d_model), lambda b, i: (b, 0)),
            pl.BlockSpec((n_mem, d_model), lambda b, i: (b, 1)),
        ],
        out_specs=pl.BlockSpec((tq, d_model), lambda b, i: (b * nq + i, 0)),
        out_shape=jax.ShapeDtypeStruct(q.shape, BF16),
        compiler_params=_cparams(("parallel", "parallel")),
        name="xattn",
    )(q, kv, kv)


def _mlp_kernel(x_ref, g_ref, wu_ref, wd_ref, o_ref, xn_ref):
    f = pl.program_id(1)

    @pl.when(f == 0)
    def _():
        x = x_ref[...]
        xn_ref[...] = _rms_rows(x, g_ref[...]).astype(BF16)
        o_ref[...] = x

    hid = jnp.dot(xn_ref[...], wu_ref[...], preferred_element_type=F32)
    hid = jnp.square(jnp.maximum(hid, 0.0)).astype(BF16)
    o_ref[...] += jnp.dot(hid, wd_ref[...], preferred_element_type=F32)


def _mlp(x, gain, w_up, w_down, *, tm, tf):
    m, d = x.shape
    d_ff = w_up.shape[1]
    return pl.pallas_call(
        _mlp_kernel,
        grid=(m // tm, d_ff // tf),
        in_specs=[
            pl.BlockSpec((tm, d), lambda i, f: (i, 0)),
            pl.BlockSpec((1, d), lambda i, f: (0, 0)),
            pl.BlockSpec((d, tf), lambda i, f: (0, f)),
            pl.BlockSpec((tf, d), lambda i, f: (f, 0)),
        ],
        out_specs=pl.BlockSpec((tm, d), lambda i, f: (i, 0)),
        out_shape=jax.ShapeDtypeStruct((m, d), F32),
        scratch_shapes=[pltpu.VMEM((tm, d), BF16)],
        compiler_params=_cparams(("parallel", "arbitrary")),
        name="mlp",
    )(x, gain.reshape(1, d), w_up, w_down)


def _layer(x, mem, mixer_norm, w_in, fox_f_bias, mlstm_i_bias, mlstm_f_bias, conv_w, conv_b, fox_q_norm,
           fox_k_norm, fox_out_norm, mlstm_out_norm, w_out, xattn_norm, mem_norm, w_xq, w_xkv, xq_norm,
           xk_norm, w_xo, mlp_norm, w_up, w_down, *, batch, s_len, n_mem):
    fox_w = FOX_HEADS * FOX_HEAD_DIM
    mqk_w = 2 * MLSTM_HEADS * MLSTM_QK_DIM
    mv_w = MLSTM_HEADS * MLSTM_V_DIM
    d_model = x.shape[1]

    o_ff = 3 * fox_w
    o_mqk = o_ff + FOX_HEADS
    o_mv = o_mqk + mqk_w
    o_mi = o_mv + mv_w
    o_mo = o_mi + 2 * MLSTM_HEADS
    w_main = jnp.concatenate([w_in[:, 2 * fox_w:o_ff], w_in[:, :2 * fox_w], w_in[:, o_mqk:o_mi], w_in[:, o_mo:]],
                             axis=1).astype(BF16)
    w_gates = jnp.concatenate([w_in[:, o_ff:o_mqk], w_in[:, o_mi:o_mo],
                               jnp.zeros((d_model, LANES - N_GATES), F32)], axis=1).astype(BF16)
    gate_bias = jnp.concatenate([fox_f_bias, mlstm_i_bias, mlstm_f_bias]).reshape(N_GATES, 1)

    proj, vt, gates_raw = _in_proj(x, mixer_norm, w_main, jnp.stack([fox_q_norm, fox_k_norm]), w_gates, conv_w, conv_b,
                                   tm=256, s_len=s_len)
    grow, gcol, ka = _gates(gates_raw, gate_bias, batch=batch, s_len=s_len)
    fo = _fox_attention(proj, vt, ka, grow, fox_out_norm, batch=batch, s_len=s_len)
    mh = _mlstm(proj, gcol, grow, mlstm_out_norm, batch=batch, s_len=s_len)
    x = _proj_residual([fo, mh], w_out.astype(BF16), x, tm=512, tn=512)

    q = _norm_proj(x, xattn_norm, w_xq.astype(BF16), xq_norm, tm=512, n_norm_tiles=XATTN_HEADS)
    kv = _norm_proj(mem, mem_norm, w_xkv.astype(BF16), xk_norm, tm=256, n_norm_tiles=XATTN_HEADS)
    co = _xattn(q, kv, batch=batch, s_len=s_len, n_mem=n_mem, tq=1024)
    x = _proj_residual([co], w_xo.astype(BF16), x, tm=512, tn=512)

    return _mlp(x, mlp_norm, w_up.astype(BF16), w_down.astype(BF16), tm=1024, tf=512)


def kernel(x, mem, mixer_norm, w_in, fox_f_bias, mlstm_i_bias, mlstm_f_bias, conv_w, conv_b, fox_q_norm,
           fox_k_norm, fox_out_norm, mlstm_out_norm, w_out, xattn_norm, mem_norm, w_xq, w_xkv, xq_norm,
           xk_norm, w_xo, mlp_norm, w_up, w_down):
    batch, s_len, d_model = x.shape
    n_mem = mem.shape[1]
    depth = w_in.shape[0]
    h = x.reshape(batch * s_len, d_model)
    mem2 = mem.reshape(batch * n_mem, d_model)
    for l in range(depth):
        h = _layer(h, mem2, mixer_norm[l], w_in[l], fox_f_bias[l], mlstm_i_bias[l], mlstm_f_bias[l],
                   conv_w[l], conv_b[l], fox_q_norm[l], fox_k_norm[l], fox_out_norm[l], mlstm_out_norm[l],
                   w_out[l], xattn_norm[l], mem_norm[l], w_xq[l], w_xkv[l], xq_norm[l], xk_norm[l], w_xo[l],
                   mlp_norm[l], w_up[l], w_down[l], batch=batch, s_len=s_len, n_mem=n_mem)
    return h.reshape(batch, s_len, d_model)
```

```python
import functools

import jax
import jax.numpy as jnp
from jax import lax
from jax.experimental import pallas as pl
from jax.experimental.pallas import tpu as pltpu

F32 = jnp.float32
BF16 = jnp.bfloat16

EPS = 1e-6
FOX_HEADS = 8
FOX_HEAD_DIM = 128
MLSTM_HEADS = 4
MLSTM_QK_DIM = 128
MLSTM_V_DIM = 256
CONV_WIDTH = 4
XATTN_HEADS = 4
N_GATES = 16
LANES = 128
SUBLANES = 8
NEG_BIG = -1e30
LOG2E = 1.4426950408889634
FOX_AUG_DEPTH = 256
FOX_ONES_ROWS = 16

MLSTM_CHUNK = 256
FOX_BLOCK = 512
FOX_KEY_SUB = 256
IN_PROJ_TILE = 512
VMEM_LIMIT = 56 * 1024 * 1024


def _cparams(sem):
    return pltpu.CompilerParams(dimension_semantics=sem, vmem_limit_bytes=VMEM_LIMIT)


def _rms_rows(x, gain):
    ms = jnp.mean(x * x, axis=-1, keepdims=True)
    return x * lax.rsqrt(ms + EPS) * gain


def _sigmoid(x):
    return 1.0 / (1.0 + jnp.exp(-x))


def _split3(x):
    hi = x.astype(BF16).astype(F32)
    r = x - hi
    mid = r.astype(BF16).astype(F32)
    lo = (r - mid).astype(BF16).astype(F32)
    return hi, mid, lo


def _resident(block_shape):
    return pl.BlockSpec(block_shape, lambda i: (0,) * len(block_shape), pipeline_mode=pl.Buffered(1))


def _tile_pipeline(n_tiles, matmul, epilogue):
    pending = matmul(0)
    for j in range(1, n_tiles):
        nxt = matmul(j)
        epilogue(j - 1, pending)
        pending = nxt
    epilogue(n_tiles - 1, pending)


def _in_proj_kernel(x_ref, g_ref, w_ref, qk_gain_ref, wg_ref, out_ref, vt_ref, gates_ref):
    seg_w = FOX_HEADS * FOX_HEAD_DIM
    tn = IN_PROJ_TILE
    xn = _rms_rows(x_ref[...], g_ref[...]).astype(BF16)

    def matmul(j):
        return jnp.dot(xn, w_ref[:, j * tn:(j + 1) * tn], preferred_element_type=F32)

    def head_norm(acc, gain, post_scale, col0):
        for h in range(tn // FOX_HEAD_DIM):
            sl = slice(h * FOX_HEAD_DIM, (h + 1) * FOX_HEAD_DIM)
            y = _rms_rows(acc[:, sl], gain) * post_scale
            out_ref[:, col0 + h * FOX_HEAD_DIM:col0 + (h + 1) * FOX_HEAD_DIM] = y.astype(out_ref.dtype)

    def epilogue(j, acc):
        seg, off = divmod(j * tn, seg_w)
        if seg == 0:
            vt_ref[off:off + tn, :] = acc.T.astype(vt_ref.dtype)
        elif seg == 1:
            head_norm(acc, qk_gain_ref[0], FOX_HEAD_DIM ** -0.5 * LOG2E, off)
        elif seg == 2:
            head_norm(acc, qk_gain_ref[1], 1.0, seg_w + off)
        else:
            out_ref[:, (seg - 1) * seg_w + off:(seg - 1) * seg_w + off + tn] = acc.astype(out_ref.dtype)

    gates_ref[...] = jnp.dot(xn, wg_ref[...], preferred_element_type=F32)
    _tile_pipeline(w_ref.shape[1] // tn, matmul, epilogue)


def _in_proj(x, gain, w, qk_gains, w_gates, *, tm):
    m, k = x.shape
    tn = FOX_HEADS * FOX_HEAD_DIM
    n = w.shape[1]
    return pl.pallas_call(
        _in_proj_kernel,
        grid=(m // tm,),
        in_specs=[
            pl.BlockSpec((tm, k), lambda i: (i, 0)),
            _resident((1, k)),
            _resident((k, n)),
            _resident((2, 1, FOX_HEAD_DIM)),
            _resident((k, LANES)),
        ],
        out_specs=[
            pl.BlockSpec((tm, n - tn), lambda i: (i, 0)),
            pl.BlockSpec((tn, tm), lambda i: (0, i)),
            pl.BlockSpec((tm, LANES), lambda i: (i, 0)),
        ],
        out_shape=[
            jax.ShapeDtypeStruct((m, n - tn), BF16),
            jax.ShapeDtypeStruct((tn, m), BF16),
            jax.ShapeDtypeStruct((m, LANES), F32),
        ],
        compiler_params=_cparams(("parallel",)),
        name="in_proj",
    )(x, gain.reshape(1, k), w, qk_gains.reshape(2, 1, FOX_HEAD_DIM), w_gates)


def _bf16_copy(w_ref, wb_ref):
    @pl.when(pl.program_id(0) == 0)
    def _():
        wb_ref[...] = w_ref[...].astype(BF16)
    return wb_ref


def _norm_proj_kernel(x_ref, g_ref, w_ref, hg_ref, out_ref, *scratch, n_norm_tiles):
    tn = hg_ref.shape[1]
    wb_ref = _bf16_copy(w_ref, scratch[0]) if scratch else None
    xn = _rms_rows(x_ref[...], g_ref[...]).astype(BF16)

    def matmul(j):
        cols = slice(j * tn, (j + 1) * tn)
        w_tile = wb_ref[:, cols] if scratch else w_ref[:, cols].astype(BF16)
        return jnp.dot(xn, w_tile, preferred_element_type=F32)

    def epilogue(j, acc):
        y = _rms_rows(acc, hg_ref[...]) if j < n_norm_tiles else acc
        out_ref[:, j * tn:(j + 1) * tn] = y.astype(out_ref.dtype)

    _tile_pipeline(w_ref.shape[1] // tn, matmul, epilogue)


def _norm_proj(x, gain, w, head_gain, *, tm, n_norm_tiles, cast_once):
    m, k = x.shape
    n = w.shape[1]
    tn = head_gain.shape[0]
    return pl.pallas_call(
        functools.partial(_norm_proj_kernel, n_norm_tiles=n_norm_tiles),
        grid=(m // tm,),
        in_specs=[
            pl.BlockSpec((tm, k), lambda i: (i, 0)),
            _resident((1, k)),
            _resident((k, n)),
            _resident((1, tn)),
        ],
        out_specs=pl.BlockSpec((tm, n), lambda i: (i, 0)),
        out_shape=jax.ShapeDtypeStruct((m, n), BF16),
        scratch_shapes=[pltpu.VMEM((k, n), BF16)] if cast_once else [],
        compiler_params=_cparams(("arbitrary",) if cast_once else ("parallel",)),
        name="norm_proj",
    )(x, gain.reshape(1, k), w, head_gain.reshape(1, tn))


def _proj_residual_kernel(*refs, n_lhs, tn):
    lhs = refs[:n_lhs]
    w_ref, res_ref, out_ref, wb_ref = refs[n_lhs:]
    kp = lhs[0].shape[1]
    _bf16_copy(w_ref, wb_ref)

    def matmul(j):
        cols = slice(j * tn, (j + 1) * tn)
        acc = jnp.dot(lhs[0][...], wb_ref[0:kp, cols], preferred_element_type=F32)
        for p in range(1, n_lhs):
            acc = acc + jnp.dot(lhs[p][...], wb_ref[p * kp:(p + 1) * kp, cols], preferred_element_type=F32)
        return acc

    def epilogue(j, acc):
        cols = slice(j * tn, (j + 1) * tn)
        out_ref[:, cols] = res_ref[:, cols] + acc

    _tile_pipeline(w_ref.shape[1] // tn, matmul, epilogue)


def _proj_residual(lhs_list, w, residual, *, tm, tn):
    m, n = residual.shape
    n_lhs = len(lhs_list)
    kp = lhs_list[0].shape[1]
    in_specs = [pl.BlockSpec((tm, kp), lambda i: (i, 0)) for _ in lhs_list]
    in_specs += [_resident(w.shape), pl.BlockSpec((tm, n), lambda i: (i, 0))]
    return pl.pallas_call(
        functools.partial(_proj_residual_kernel, n_lhs=n_lhs, tn=tn),
        grid=(m // tm,),
        in_specs=in_specs,
        out_specs=pl.BlockSpec((tm, n), lambda i: (i, 0)),
        out_shape=jax.ShapeDtypeStruct((m, n), F32),
        scratch_shapes=[pltpu.VMEM(w.shape, BF16)],
        compiler_params=_cparams(("arbitrary",)),
        name="proj_residual",
    )(*lhs_list, w, residual)


def _gates_kernel(g_ref, bias_ref, row_ref, col_ref, ka_ref):
    s_len = g_ref.shape[0]
    z = g_ref[...].T[0:N_GATES, :] + bias_ref[...]
    row = lax.broadcasted_iota(jnp.int32, z.shape, 0)
    is_forget = (row < FOX_HEADS) | (row >= FOX_HEADS + MLSTM_HEADS)
    logsig = jnp.minimum(z, 0.0) - jnp.log1p(jnp.exp(-jnp.abs(z)))
    c = jnp.where(is_forget, logsig, 0.0)
    lane = lax.broadcasted_iota(jnp.int32, z.shape, 1)
    shift = 1
    while shift < s_len:
        c = c + jnp.where(lane >= shift, pltpu.roll(c, shift, axis=1), 0.0)
        shift *= 2
    vals = jnp.where(is_forget, c, z)
    row_ref[...] = vals
    padded = jnp.concatenate([vals, jnp.zeros((LANES - N_GATES, s_len), F32)], axis=0)
    col_ref[...] = padded.T
    hi, mid, lo = _split3(c[0:FOX_HEADS, :] * LOG2E)
    aug = jnp.concatenate([jnp.ones((FOX_HEADS, s_len), F32), -hi, -mid, -lo,
                           jnp.zeros((LANES - 4 * FOX_HEADS, s_len), F32)], axis=0)
    ka_ref[...] = aug.T.astype(ka_ref.dtype)


def _gates(gates_raw, bias, *, batch, s_len):
    return pl.pallas_call(
        _gates_kernel,
        grid=(batch,),
        in_specs=[pl.BlockSpec((s_len, LANES), lambda b: (b, 0)),
                  pl.BlockSpec((N_GATES, 1), lambda b: (0, 0))],
        out_specs=[pl.BlockSpec((None, N_GATES, s_len), lambda b: (b, 0, 0)),
                   pl.BlockSpec((s_len, LANES), lambda b: (b, 0)),
                   pl.BlockSpec((s_len, LANES), lambda b: (b, 0))],
        out_shape=[jax.ShapeDtypeStruct((batch, N_GATES, s_len), F32),
                   jax.ShapeDtypeStruct((batch * s_len, LANES), F32),
                   jax.ShapeDtypeStruct((batch * s_len, LANES), BF16)],
        compiler_params=_cparams(("parallel",)),
        name="gates",
    )(gates_raw, bias)


def _fox_kernel(qi_ref, ki_ref, q_ref, k_ref, vt_ref, ka_ref, grow_ref, on_ref, o_ref, qt_ref, m_ref, acc_ref):
    pair = pl.program_id(1)
    qi = qi_ref[pair]
    ki = ki_ref[pair]
    tq, tk = q_ref.shape[0], k_ref.shape[0]
    d = FOX_HEAD_DIM

    @pl.when(ki == 0)
    def _():
        m_ref[...] = jnp.full(m_ref.shape, NEG_BIG, F32)
        acc_ref[...] = jnp.zeros(acc_ref.shape, F32)
        sub = lax.broadcasted_iota(jnp.int32, (SUBLANES, tq), 0)
        for h in range(FOX_HEADS):
            hi, mid, lo = _split3(grow_ref[h:h + 1, :] * LOG2E)
            c_rows = jnp.where(sub == 0, hi, jnp.where(sub == 1, mid, jnp.where(sub == 2, lo, 0.0)))
            pick = jnp.where(sub == h, 1.0, 0.0)
            qt = q_ref[:, h * d:(h + 1) * d].astype(F32).T
            pad = jnp.zeros((FOX_AUG_DEPTH - d - 4 * SUBLANES, tq), F32)
            qt_ref[h] = jnp.concatenate([qt, c_rows, pick, pick, pick, pad], axis=0).astype(BF16)

    def step(diagonal):
        sub = FOX_KEY_SUB
        ones = jnp.ones((FOX_ONES_ROWS, sub), BF16)
        if diagonal:
            keep = (lax.broadcasted_iota(jnp.int32, (sub, tq), 0)
                    <= lax.broadcasted_iota(jnp.int32, (sub, tq), 1))

        def scores(h, ks):
            t0 = ks * sub if diagonal else 0
            rows = slice(ks * sub, (ks + 1) * sub)
            k_aug = jnp.concatenate([k_ref[rows, h * d:(h + 1) * d], ka_ref[rows, :]], axis=1)
            st = jnp.dot(k_aug, qt_ref[h, :, t0:], preferred_element_type=F32)
            if diagonal:
                st = jnp.where(keep[:, 0:tq - t0], st, NEG_BIG)
            m_prev = m_ref[h:h + 1, t0:]
            m_new = jnp.maximum(m_prev, jnp.max(st, axis=0, keepdims=True))
            m_ref[h:h + 1, t0:] = m_new
            return t0, st, m_new, jnp.exp2(m_prev - m_new)

        def accumulate(h, ks, t0, st, m_new, alpha):
            p = jnp.exp2(st - m_new)
            vt_aug = jnp.concatenate([vt_ref[h * d:(h + 1) * d, ks * sub:(ks + 1) * sub], ones], axis=0)
            acc_ref[h, :, t0:] = alpha * acc_ref[h, :, t0:] + jnp.dot(vt_aug, p.astype(BF16),
                                                                      preferred_element_type=F32)

        items = [(h, ks) for h in range(FOX_HEADS) for ks in range(tk // sub)]
        lookahead = 2
        pending = [scores(*it) for it in items[:lookahead]]
        for n, it in enumerate(items):
            if n + lookahead < len(items):
                pending.append(scores(*items[n + lookahead]))
            accumulate(*it, *pending.pop(0))

    @pl.when(ki < qi)
    def _():
        step(False)

    @pl.when(ki == qi)
    def _():
        step(True)
        for h in range(FOX_HEADS):
            sl = slice(h * d, (h + 1) * d)
            o = (acc_ref[h, 0:d, :] / acc_ref[h, d:d + 1, :]).T
            o_ref[:, sl] = _rms_rows(o, on_ref[:, sl]).astype(o_ref.dtype)


def _fox_attention(proj, vt, ka, grow, out_norm, *, batch, s_len):
    t = FOX_BLOCK
    nq = s_len // t
    width = FOX_HEADS * FOX_HEAD_DIM
    pairs = [(qi, ki) for qi in range(nq) for ki in range(qi + 1)]
    qi_tab = jnp.asarray([p[0] for p in pairs], jnp.int32)
    ki_tab = jnp.asarray([p[1] for p in pairs], jnp.int32)

    def q_row(b, p, qi_tab, ki_tab):
        return b * nq + qi_tab[p]

    def k_row(b, p, qi_tab, ki_tab):
        return b * nq + ki_tab[p]

    grid_spec = pltpu.PrefetchScalarGridSpec(
        num_scalar_prefetch=2,
        grid=(batch, len(pairs)),
        in_specs=[
            pl.BlockSpec((t, width), lambda *a: (q_row(*a), 0)),
            pl.BlockSpec((t, width), lambda *a: (k_row(*a), 1)),
            pl.BlockSpec((width, t), lambda *a: (0, k_row(*a))),
            pl.BlockSpec((t, LANES), lambda *a: (k_row(*a), 0)),
            pl.BlockSpec((None, N_GATES, t), lambda b, p, qi_tab, ki_tab: (b, 0, qi_tab[p])),
            pl.BlockSpec((1, width), lambda *a: (0, 0)),
        ],
        out_specs=pl.BlockSpec((t, width), lambda *a: (q_row(*a), 0)),
        scratch_shapes=[
            pltpu.VMEM((FOX_HEADS, FOX_AUG_DEPTH, t), BF16),
            pltpu.VMEM((FOX_HEADS, t), F32),
            pltpu.VMEM((FOX_HEADS, FOX_HEAD_DIM + FOX_ONES_ROWS, t), F32),
        ],
    )
    return pl.pallas_call(
        _fox_kernel,
        grid_spec=grid_spec,
        out_shape=jax.ShapeDtypeStruct((batch * s_len, width), BF16),
        compiler_params=_cparams(("parallel", "arbitrary")),
        name="fox_attention",
    )(qi_tab, ki_tab, proj, proj, vt, ka, grow, out_norm.reshape(1, width))


def _mlstm_kernel(mqk_ref, mv_ref, mo_ref, gcol_ref, grow_ref, cw_ref, cb_ref, on_ref, out_ref,
                  ct_ref, mg_ref, ext_ref, qk_ref):
    c = pl.program_id(1)
    length = mqk_ref.shape[0]
    dk, dv = MLSTM_QK_DIM, MLSTM_V_DIM
    n_qk = MLSTM_HEADS * dk
    tail = SUBLANES

    @pl.when(c == 0)
    def _():
        ct_ref[...] = jnp.zeros(ct_ref.shape, F32)
        mg_ref[...] = jnp.zeros(mg_ref.shape, F32)
        ext_ref[0:tail, :] = jnp.zeros((tail, ext_ref.shape[1]), F32)

    u = mqk_ref[...].astype(F32)
    ext_ref[tail:, :] = u
    y = cb_ref[...]
    for j in range(CONV_WIDTH):
        start = tail - (CONV_WIDTH - 1) + j
        y = y + cw_ref[j:j + 1, :] * ext_ref[start:start + length, :]
    ext_ref[0:tail, :] = u[length - tail:, :]
    y = y * _sigmoid(y)
    qk_ref[:, 0:n_qk] = y[:, 0:n_qk].astype(BF16)
    qk_ref[:, n_qk:] = (y[:, n_qk:] * dk ** -0.5).astype(BF16)

    keep = (lax.broadcasted_iota(jnp.int32, (length, length), 0)
            >= lax.broadcasted_iota(jnp.int32, (length, length), 1))
    ones = jnp.ones((length, LANES), BF16)

    def scores(h):
        qb = qk_ref[:, h * dk:(h + 1) * dk]
        kb = qk_ref[:, n_qk + h * dk:n_qk + (h + 1) * dk]
        gi, gf = FOX_HEADS + h, FOX_HEADS + MLSTM_HEADS + h
        u_row = grow_ref[gi:gi + 1, :] - grow_ref[gf:gf + 1, :]
        mg = mg_ref[h:h + 1, 0:1]
        um = jnp.where(keep, u_row, NEG_BIG)
        a = jnp.maximum(jnp.max(um, axis=-1, keepdims=True), mg)
        sqk = lax.dot_general(qb, kb, (((1,), (1,)), ((), ())), preferred_element_type=F32) * jnp.exp(um - a)
        mg_new = jnp.maximum(mg, jnp.max(u_row, axis=-1, keepdims=True))
        return sqk, a, mg, mg_new

    def outputs(h, sqk, a, mg, mg_new):
        qb = qk_ref[:, h * dk:(h + 1) * dk]
        kb = qk_ref[:, n_qk + h * dk:n_qk + (h + 1) * dk]
        sl = slice(h * dv, (h + 1) * dv)
        v_aug = jnp.concatenate([mv_ref[:, sl], ones], axis=1)
        gi, gf = FOX_HEADS + h, FOX_HEADS + MLSTM_HEADS + h
        f_col = gcol_ref[:, gf:gf + 1]
        u_col = gcol_ref[:, gi:gi + 1] - f_col
        ct = ct_ref[h]
        inter = jnp.dot(qb, ct.astype(BF16), preferred_element_type=F32)
        nd = jnp.dot(sqk.astype(BF16), v_aug, preferred_element_type=F32) + jnp.exp(mg - a) * inter
        den = jnp.maximum(jnp.abs(nd[:, dv:dv + 1]), jnp.exp(-(f_col + a)))
        hid = nd[:, :dv] * (1.0 / den)
        gate = _sigmoid(mo_ref[:, sl].astype(F32))
        out_ref[:, sl] = (_rms_rows(hid, on_ref[:, sl]) * gate).astype(out_ref.dtype)

        wav = (jnp.exp(u_col - mg_new) * v_aug.astype(F32)).astype(BF16)
        kt = kb.astype(F32).T.astype(BF16)
        ct_ref[h] = jnp.exp(mg - mg_new) * ct + jnp.dot(kt, wav, preferred_element_type=F32)
        mg_ref[h:h + 1, :] = jnp.broadcast_to(mg_new, (1, LANES))

    pending = scores(0)
    for h in range(1, MLSTM_HEADS):
        nxt = scores(h)
        outputs(h - 1, *pending)
        pending = nxt
    outputs(MLSTM_HEADS - 1, *pending)


def _mlstm(proj, gcol, grow, conv_w, conv_b, out_norm, *, batch, s_len):
    length = MLSTM_CHUNK
    nc = s_len // length
    width = MLSTM_HEADS * MLSTM_V_DIM
    qk_width = 2 * MLSTM_HEADS * MLSTM_QK_DIM
    return pl.pallas_call(
        _mlstm_kernel,
        grid=(batch, nc),
        in_specs=[
            pl.BlockSpec((length, qk_width), lambda b, c: (b * nc + c, 2)),
            pl.BlockSpec((length, width), lambda b, c: (b * nc + c, 3)),
            pl.BlockSpec((length, width), lambda b, c: (b * nc + c, 4)),
            pl.BlockSpec((length, LANES), lambda b, c: (b * nc + c, 0)),
            pl.BlockSpec((None, N_GATES, length), lambda b, c: (b, 0, c)),
            pl.BlockSpec((CONV_WIDTH, qk_width), lambda b, c: (0, 0)),
            pl.BlockSpec((1, qk_width), lambda b, c: (0, 0)),
            pl.BlockSpec((1, width), lambda b, c: (0, 0)),
        ],
        out_specs=pl.BlockSpec((length, width), lambda b, c: (b * nc + c, 0)),
        out_shape=jax.ShapeDtypeStruct((batch * s_len, width), BF16),
        scratch_shapes=[
            pltpu.VMEM((MLSTM_HEADS, MLSTM_QK_DIM, MLSTM_V_DIM + LANES), F32),
            pltpu.VMEM((SUBLANES, LANES), F32),
            pltpu.VMEM((length + SUBLANES, qk_width), F32),
            pltpu.VMEM((length, qk_width), BF16),
        ],
        compiler_params=_cparams(("parallel", "arbitrary")),
        name="mlstm",
    )(proj, proj, proj, gcol, grow, conv_w, conv_b.reshape(1, qk_width), out_norm.reshape(1, width))


def _xattn_kernel(q_ref, k_ref, v_ref, o_ref):
    d = q_ref.shape[1] // XATTN_HEADS
    scale = d ** -0.5
    for h in range(XATTN_HEADS):
        sl = slice(h * d, (h + 1) * d)
        s = lax.dot_general(q_ref[:, sl], k_ref[:, sl], (((1,), (1,)), ((), ())),
                            preferred_element_type=F32) * scale
        p = jnp.exp(s - jnp.max(s, axis=-1, keepdims=True))
        p = p / jnp.sum(p, axis=-1, keepdims=True)
        o_ref[:, sl] = jnp.dot(p.astype(BF16), v_ref[:, sl], preferred_element_type=F32).astype(o_ref.dtype)


def _xattn(q, kv, *, batch, s_len, n_mem, tq):
    d_model = q.shape[1]
    nq = s_len // tq
    return pl.pallas_call(
        _xattn_kernel,
        grid=(batch, nq),
        in_specs=[
            pl.BlockSpec((tq, d_model), lambda b, i: (b * nq + i, 0)),
            pl.BlockSpec((n_mem, d_model), lambda b, i: (b, 0)),
            pl.BlockSpec((n_mem, d_model), lambda b, i: (b, 1)),
        ],
        out_specs=pl.BlockSpec((tq, d_model), lambda b, i: (b * nq + i, 0)),
        out_shape=jax.ShapeDtypeStruct(q.shape, BF16),
        compiler_params=_cparams(("parallel", "parallel")),
        name="xattn",
    )(q, kv, kv)


def _mlp_kernel(x_ref, g_ref, wu_ref, wd_ref, o_ref, xn_ref):
    f = pl.program_id(1)

    @pl.when(f == 0)
    def _():
        x = x_ref[...]
        xn_ref[...] = _rms_rows(x, g_ref[...]).astype(BF16)
        o_ref[...] = x

    hid = jnp.dot(xn_ref[...], wu_ref[...], preferred_element_type=F32)
    hid = jnp.square(jnp.maximum(hid, 0.0)).astype(BF16)
    o_ref[...] += jnp.dot(hid, wd_ref[...], preferred_element_type=F32)


def _mlp(x, gain, w_up, w_down, *, tm, tf):
    m, d = x.shape
    d_ff = w_up.shape[1]
    return pl.pallas_call(
        _mlp_kernel,
        grid=(m // tm, d_ff // tf),
        in_specs=[
            pl.BlockSpec((tm, d), lambda i, f: (i, 0)),
            pl.BlockSpec((1, d), lambda i, f: (0, 0)),
            pl.BlockSpec((d, tf), lambda i, f: (0, f)),
            pl.BlockSpec((tf, d), lambda i, f: (f, 0)),
        ],
        out_specs=pl.BlockSpec((tm, d), lambda i, f: (i, 0)),
        out_shape=jax.ShapeDtypeStruct((m, d), F32),
        scratch_shapes=[pltpu.VMEM((tm, d), BF16)],
        compiler_params=_cparams(("parallel", "arbitrary")),
        name="mlp",
    )(x, gain.reshape(1, d), w_up, w_down)


def _layer(x, mem, mixer_norm, w_in, fox_f_bias, mlstm_i_bias, mlstm_f_bias, conv_w, conv_b, fox_q_norm,
           fox_k_norm, fox_out_norm, mlstm_out_norm, w_out, xattn_norm, mem_norm, w_xq, w_xkv, xq_norm,
           xk_norm, w_xo, mlp_norm, w_up, w_down, *, batch, s_len, n_mem):
    fox_w = FOX_HEADS * FOX_HEAD_DIM
    mqk_w = 2 * MLSTM_HEADS * MLSTM_QK_DIM
    mv_w = MLSTM_HEADS * MLSTM_V_DIM
    d_model = x.shape[1]

    o_ff = 3 * fox_w
    o_mqk = o_ff + FOX_HEADS
    o_mv = o_mqk + mqk_w
    o_mi = o_mv + mv_w
    o_mo = o_mi + 2 * MLSTM_HEADS
    wb = w_in.astype(BF16)
    w_main = jnp.concatenate([wb[:, 2 * fox_w:o_ff], wb[:, :2 * fox_w], wb[:, o_mqk:o_mi], wb[:, o_mo:]],
                             axis=1)
    w_gates = jnp.concatenate([wb[:, o_ff:o_mqk], wb[:, o_mi:o_mo],
                               jnp.zeros((d_model, LANES - N_GATES), BF16)], axis=1)
    gate_bias = jnp.concatenate([fox_f_bias, mlstm_i_bias, mlstm_f_bias]).reshape(N_GATES, 1)

    proj, vt, gates_raw = _in_proj(x, mixer_norm, w_main, jnp.stack([fox_q_norm, fox_k_norm]), w_gates, tm=256)
    grow, gcol, ka = _gates(gates_raw, gate_bias, batch=batch, s_len=s_len)
    fo = _fox_attention(proj, vt, ka, grow, fox_out_norm, batch=batch, s_len=s_len)
    mh = _mlstm(proj, gcol, grow, conv_w, conv_b, mlstm_out_norm, batch=batch, s_len=s_len)
    x = _proj_residual([fo, mh], w_out, x, tm=512, tn=512)

    q = _norm_proj(x, xattn_norm, w_xq, xq_norm, tm=512, n_norm_tiles=XATTN_HEADS, cast_once=True)
    kv = _norm_proj(mem, mem_norm, w_xkv, xk_norm, tm=256, n_norm_tiles=XATTN_HEADS, cast_once=False)
    co = _xattn(q, kv, batch=batch, s_len=s_len, n_mem=n_mem, tq=1024)
    x = _proj_residual([co], w_xo, x, tm=512, tn=512)

    return _mlp(x, mlp_norm, w_up.astype(BF16), w_down.astype(BF16), tm=1024, tf=512)


def kernel(x, mem, mixer_norm, w_in, fox_f_bias, mlstm_i_bias, mlstm_f_bias, conv_w, conv_b, fox_q_norm,
           fox_k_norm, fox_out_norm, mlstm_out_norm, w_out, xattn_norm, mem_norm, w_xq, w_xkv, xq_norm,
           xk_norm, w_xo, mlp_norm, w_up, w_down):
    batch, s_len, d_model = x.shape
    n_mem = mem.shape[1]
    depth = w_in.shape[0]
    h = x.reshape(batch * s_len, d_model)
    mem2 = mem.reshape(batch * n_mem, d_model)
    for l in range(depth):
        h = _layer(h, mem2, mixer_norm[l], w_in[l], fox_f_bias[l], mlstm_i_bias[l], mlstm_f_bias[l],
                   conv_w[l], conv_b[l], fox_q_norm[l], fox_k_norm[l], fox_out_norm[l], mlstm_out_norm[l],
                   w_out[l], xattn_norm[l], mem_norm[l], w_xq[l], w_xkv[l], xq_norm[l], xk_norm[l], w_xo[l],
                   mlp_norm[l], w_up[l], w_down[l], batch=batch, s_len=s_len, n_mem=n_mem)
    return h.reshape(batch, s_len, d_model)
```

```python
import functools

import jax
import jax.numpy as jnp
from jax import lax
from jax.experimental import pallas as pl
from jax.experimental.pallas import tpu as pltpu

F32 = jnp.float32
BF16 = jnp.bfloat16

EPS = 1e-6
FOX_HEADS = 8
FOX_HEAD_DIM = 128
MLSTM_HEADS = 4
MLSTM_QK_DIM = 128
MLSTM_V_DIM = 256
CONV_WIDTH = 4
XATTN_HEADS = 4
N_GATES = 16
LANES = 128
SUBLANES = 8
BF16_SUBLANES = 16
NEG_BIG = -1e30
LOG2E = 1.4426950408889634
FOX_AUG_DEPTH = 256
FOX_ONES_ROWS = 16

MLSTM_CHUNK = 256
FOX_BLOCK = 512
FOX_KEY_SUB = 256
IN_PROJ_TILE = 512
VMEM_LIMIT = 56 * 1024 * 1024


def _cparams(sem):
    return pltpu.CompilerParams(dimension_semantics=sem, vmem_limit_bytes=VMEM_LIMIT)


def _rms_rows(x, gain):
    ms = jnp.mean(x * x, axis=-1, keepdims=True)
    return x * lax.rsqrt(ms + EPS) * gain


def _sigmoid(x):
    return 1.0 / (1.0 + jnp.exp(-x))


def _split3(x):
    hi = x.astype(BF16).astype(F32)
    r = x - hi
    mid = r.astype(BF16).astype(F32)
    lo = (r - mid).astype(BF16).astype(F32)
    return hi, mid, lo


def _resident(block_shape):
    return pl.BlockSpec(block_shape, lambda i: (0,) * len(block_shape), pipeline_mode=pl.Buffered(1))


def _tile_pipeline(n_tiles, matmul, epilogue):
    pending = matmul(0)
    for j in range(1, n_tiles):
        nxt = matmul(j)
        epilogue(j - 1, pending)
        pending = nxt
    epilogue(n_tiles - 1, pending)


def _in_proj_kernel(x_ref, g_ref, wf_ref, wm_ref, wo_ref, qk_gain_ref, wg_ref, out_ref, vt_ref, gates_ref):
    seg_w = FOX_HEADS * FOX_HEAD_DIM
    tn = IN_PROJ_TILE
    xn = _rms_rows(x_ref[...], g_ref[...]).astype(BF16)
    sources = ((wf_ref, 2 * seg_w), (wf_ref, 0), (wf_ref, seg_w), (wm_ref, 0), (wm_ref, seg_w), (wo_ref, 0))

    def matmul(j):
        seg, off = divmod(j * tn, seg_w)
        w_ref, col0 = sources[seg]
        return jnp.dot(xn, w_ref[:, col0 + off:col0 + off + tn], preferred_element_type=F32)

    def head_norm(acc, gain, post_scale, col0):
        for h in range(tn // FOX_HEAD_DIM):
            sl = slice(h * FOX_HEAD_DIM, (h + 1) * FOX_HEAD_DIM)
            y = _rms_rows(acc[:, sl], gain) * post_scale
            out_ref[:, col0 + h * FOX_HEAD_DIM:col0 + (h + 1) * FOX_HEAD_DIM] = y.astype(out_ref.dtype)

    def epilogue(j, acc):
        seg, off = divmod(j * tn, seg_w)
        if seg == 0:
            vt_ref[off:off + tn, :] = acc.T.astype(vt_ref.dtype)
        elif seg == 1:
            head_norm(acc, qk_gain_ref[0], FOX_HEAD_DIM ** -0.5 * LOG2E, off)
        elif seg == 2:
            head_norm(acc, qk_gain_ref[1], 1.0, seg_w + off)
        else:
            out_ref[:, (seg - 1) * seg_w + off:(seg - 1) * seg_w + off + tn] = acc.astype(out_ref.dtype)

    gates_ref[...] = jnp.dot(xn, wg_ref[...], preferred_element_type=F32)
    _tile_pipeline(len(sources) * seg_w // tn, matmul, epilogue)


def _in_proj(x, gain, w_all, w_mlstm, w_ogate, qk_gains, w_gates, *, tm):
    m, k = x.shape
    tn = FOX_HEADS * FOX_HEAD_DIM
    n = 3 * tn + w_mlstm.shape[1] + w_ogate.shape[1]
    return pl.pallas_call(
        _in_proj_kernel,
        grid=(m // tm,),
        in_specs=[
            pl.BlockSpec((tm, k), lambda i: (i, 0)),
            _resident((1, k)),
            _resident((k, 3 * tn)),
            _resident(w_mlstm.shape),
            _resident(w_ogate.shape),
            _resident((2, 1, FOX_HEAD_DIM)),
            _resident((k, LANES)),
        ],
        out_specs=[
            pl.BlockSpec((tm, n - tn), lambda i: (i, 0)),
            pl.BlockSpec((tn, tm), lambda i: (0, i)),
            pl.BlockSpec((tm, LANES), lambda i: (i, 0)),
        ],
        out_shape=[
            jax.ShapeDtypeStruct((m, n - tn), BF16),
            jax.ShapeDtypeStruct((tn, m), BF16),
            jax.ShapeDtypeStruct((m, LANES), F32),
        ],
        compiler_params=_cparams(("parallel",)),
        name="in_proj",
    )(x, gain.reshape(1, k), w_all, w_mlstm, w_ogate, qk_gains.reshape(2, 1, FOX_HEAD_DIM), w_gates)


def _bf16_copy(w_ref, wb_ref):
    @pl.when(pl.program_id(0) == 0)
    def _():
        wb_ref[...] = w_ref[...].astype(BF16)
    return wb_ref


def _norm_proj_kernel(x_ref, g_ref, w_ref, hg_ref, out_ref, *scratch, n_norm_tiles, post_scale):
    tn = hg_ref.shape[1]
    wb_ref = _bf16_copy(w_ref, scratch[0]) if scratch else None
    xn = _rms_rows(x_ref[...], g_ref[...]).astype(BF16)

    def matmul(j):
        cols = slice(j * tn, (j + 1) * tn)
        w_tile = wb_ref[:, cols] if scratch else w_ref[:, cols].astype(BF16)
        return jnp.dot(xn, w_tile, preferred_element_type=F32)

    def epilogue(j, acc):
        y = _rms_rows(acc, hg_ref[...]) * post_scale if j < n_norm_tiles else acc
        out_ref[:, j * tn:(j + 1) * tn] = y.astype(out_ref.dtype)

    _tile_pipeline(w_ref.shape[1] // tn, matmul, epilogue)


def _norm_proj(x, gain, w, head_gain, *, tm, n_norm_tiles, cast_once, post_scale=1.0):
    m, k = x.shape
    n = w.shape[1]
    tn = head_gain.shape[0]
    return pl.pallas_call(
        functools.partial(_norm_proj_kernel, n_norm_tiles=n_norm_tiles, post_scale=post_scale),
        grid=(m // tm,),
        in_specs=[
            pl.BlockSpec((tm, k), lambda i: (i, 0)),
            _resident((1, k)),
            _resident((k, n)),
            _resident((1, tn)),
        ],
        out_specs=pl.BlockSpec((tm, n), lambda i: (i, 0)),
        out_shape=jax.ShapeDtypeStruct((m, n), BF16),
        scratch_shapes=[pltpu.VMEM((k, n), BF16)] if cast_once else [],
        compiler_params=_cparams(("arbitrary",) if cast_once else ("parallel",)),
        name="norm_proj",
    )(x, gain.reshape(1, k), w, head_gain.reshape(1, tn))


def _proj_residual_kernel(*refs, n_lhs, tn):
    lhs = refs[:n_lhs]
    w_ref, res_ref, out_ref, wb_ref = refs[n_lhs:]
    kp = lhs[0].shape[1]
    _bf16_copy(w_ref, wb_ref)

    def matmul(j):
        cols = slice(j * tn, (j + 1) * tn)
        acc = jnp.dot(lhs[0][...], wb_ref[0:kp, cols], preferred_element_type=F32)
        for p in range(1, n_lhs):
            acc = acc + jnp.dot(lhs[p][...], wb_ref[p * kp:(p + 1) * kp, cols], preferred_element_type=F32)
        return acc

    def epilogue(j, acc):
        cols = slice(j * tn, (j + 1) * tn)
        out_ref[:, cols] = res_ref[:, cols] + acc

    _tile_pipeline(w_ref.shape[1] // tn, matmul, epilogue)


def _proj_residual(lhs_list, w, residual, *, tm, tn):
    m, n = residual.shape
    n_lhs = len(lhs_list)
    kp = lhs_list[0].shape[1]
    in_specs = [pl.BlockSpec((tm, kp), lambda i: (i, 0)) for _ in lhs_list]
    in_specs += [_resident(w.shape), pl.BlockSpec((tm, n), lambda i: (i, 0))]
    return pl.pallas_call(
        functools.partial(_proj_residual_kernel, n_lhs=n_lhs, tn=tn),
        grid=(m // tm,),
        in_specs=in_specs,
        out_specs=pl.BlockSpec((tm, n), lambda i: (i, 0)),
        out_shape=jax.ShapeDtypeStruct((m, n), F32),
        scratch_shapes=[pltpu.VMEM(w.shape, BF16)],
        compiler_params=_cparams(("arbitrary",)),
        name="proj_residual",
    )(*lhs_list, w, residual)


def _gates_kernel(g_ref, bias_ref, row_ref, col_ref, ka_ref):
    s_len = g_ref.shape[0]
    z = g_ref[...].T[0:N_GATES, :] + bias_ref[...]
    row = lax.broadcasted_iota(jnp.int32, z.shape, 0)
    is_forget = (row < FOX_HEADS) | (row >= FOX_HEADS + MLSTM_HEADS)
    logsig = jnp.minimum(z, 0.0) - jnp.log1p(jnp.exp(-jnp.abs(z)))
    c = jnp.where(is_forget, logsig, 0.0)
    lane = lax.broadcasted_iota(jnp.int32, z.shape, 1)
    shift = 1
    while shift < s_len:
        c = c + jnp.where(lane >= shift, pltpu.roll(c, shift, axis=1), 0.0)
        shift *= 2
    vals = jnp.where(is_forget, c, z)
    row_ref[...] = vals
    padded = jnp.concatenate([vals, jnp.zeros((LANES - N_GATES, s_len), F32)], axis=0)
    col_ref[...] = padded.T
    hi, mid, lo = _split3(c[0:FOX_HEADS, :] * LOG2E)
    aug = jnp.concatenate([jnp.ones((FOX_HEADS, s_len), F32), -hi, -mid, -lo,
                           jnp.zeros((LANES - 4 * FOX_HEADS, s_len), F32)], axis=0)
    ka_ref[...] = aug.T.astype(ka_ref.dtype)


def _gates(gates_raw, bias, *, batch, s_len):
    return pl.pallas_call(
        _gates_kernel,
        grid=(batch,),
        in_specs=[pl.BlockSpec((s_len, LANES), lambda b: (b, 0)),
                  pl.BlockSpec((N_GATES, 1), lambda b: (0, 0))],
        out_specs=[pl.BlockSpec((None, N_GATES, s_len), lambda b: (b, 0, 0)),
                   pl.BlockSpec((s_len, LANES), lambda b: (b, 0)),
                   pl.BlockSpec((s_len, LANES), lambda b: (b, 0))],
        out_shape=[jax.ShapeDtypeStruct((batch, N_GATES, s_len), F32),
                   jax.ShapeDtypeStruct((batch * s_len, LANES), F32),
                   jax.ShapeDtypeStruct((batch * s_len, LANES), BF16)],
        compiler_params=_cparams(("parallel",)),
        name="gates",
    )(gates_raw, bias)


def _fox_kernel(qi_ref, ki_ref, q_ref, k_ref, vt_ref, ka_ref, grow_ref, on_ref, o_ref, qt_ref, m_ref, acc_ref):
    pair = pl.program_id(1)
    qi = qi_ref[pair]
    ki = ki_ref[pair]
    tq, tk = q_ref.shape[0], k_ref.shape[0]
    d = FOX_HEAD_DIM

    @pl.when(ki == 0)
    def _():
        m_ref[...] = jnp.full(m_ref.shape, NEG_BIG, F32)
        acc_ref[...] = jnp.zeros(acc_ref.shape, F32)
        sub = lax.broadcasted_iota(jnp.int32, (SUBLANES, tq), 0)
        for h in range(FOX_HEADS):
            hi, mid, lo = _split3(grow_ref[h:h + 1, :] * LOG2E)
            c_rows = jnp.where(sub == 0, hi, jnp.where(sub == 1, mid, jnp.where(sub == 2, lo, 0.0)))
            pick = jnp.where(sub == h, 1.0, 0.0)
            qt = q_ref[:, h * d:(h + 1) * d].astype(F32).T
            pad = jnp.zeros((FOX_AUG_DEPTH - d - 4 * SUBLANES, tq), F32)
            qt_ref[h] = jnp.concatenate([qt, c_rows, pick, pick, pick, pad], axis=0).astype(BF16)

    def step(diagonal):
        sub = FOX_KEY_SUB
        ones = jnp.ones((FOX_ONES_ROWS, sub), BF16)
        if diagonal:
            keep = (lax.broadcasted_iota(jnp.int32, (sub, tq), 0)
                    <= lax.broadcasted_iota(jnp.int32, (sub, tq), 1))

        def scores(h, ks):
            t0 = ks * sub if diagonal else 0
            rows = slice(ks * sub, (ks + 1) * sub)
            k_aug = jnp.concatenate([k_ref[rows, h * d:(h + 1) * d], ka_ref[rows, :]], axis=1)
            st = jnp.dot(k_aug, qt_ref[h, :, t0:], preferred_element_type=F32)
            if diagonal:
                st = jnp.where(keep[:, 0:tq - t0], st, NEG_BIG)
            m_prev = m_ref[h:h + 1, t0:]
            m_new = jnp.maximum(m_prev, jnp.max(st, axis=0, keepdims=True))
            m_ref[h:h + 1, t0:] = m_new
            return t0, st, m_new, jnp.exp2(m_prev - m_new)

        def accumulate(h, ks, t0, st, m_new, alpha):
            p = jnp.exp2(st - m_new)
            vt_aug = jnp.concatenate([vt_ref[h * d:(h + 1) * d, ks * sub:(ks + 1) * sub], ones], axis=0)
            acc_ref[h, :, t0:] = alpha * acc_ref[h, :, t0:] + jnp.dot(vt_aug, p.astype(BF16),
                                                                      preferred_element_type=F32)

        items = [(h, ks) for h in range(FOX_HEADS) for ks in range(tk // sub)]
        lookahead = 2
        pending = [scores(*it) for it in items[:lookahead]]
        for n, it in enumerate(items):
            if n + lookahead < len(items):
                pending.append(scores(*items[n + lookahead]))
            accumulate(*it, *pending.pop(0))

    @pl.when(ki < qi)
    def _():
        step(False)

    @pl.when(ki == qi)
    def _():
        step(True)
        for h in range(FOX_HEADS):
            sl = slice(h * d, (h + 1) * d)
            o = (acc_ref[h, 0:d, :] / acc_ref[h, d:d + 1, :]).T
            o_ref[:, sl] = _rms_rows(o, on_ref[:, sl]).astype(o_ref.dtype)


def _fox_attention(proj, vt, ka, grow, out_norm, *, batch, s_len):
    t = FOX_BLOCK
    nq = s_len // t
    width = FOX_HEADS * FOX_HEAD_DIM
    pairs = [(qi, ki) for qi in range(nq) for ki in range(qi + 1)]
    qi_tab = jnp.asarray([p[0] for p in pairs], jnp.int32)
    ki_tab = jnp.asarray([p[1] for p in pairs], jnp.int32)

    def q_row(b, p, qi_tab, ki_tab):
        return b * nq + qi_tab[p]

    def k_row(b, p, qi_tab, ki_tab):
        return b * nq + ki_tab[p]

    grid_spec = pltpu.PrefetchScalarGridSpec(
        num_scalar_prefetch=2,
        grid=(batch, len(pairs)),
        in_specs=[
            pl.BlockSpec((t, width), lambda *a: (q_row(*a), 0)),
            pl.BlockSpec((t, width), lambda *a: (k_row(*a), 1)),
            pl.BlockSpec((width, t), lambda *a: (0, k_row(*a))),
            pl.BlockSpec((t, LANES), lambda *a: (k_row(*a), 0)),
            pl.BlockSpec((None, N_GATES, t), lambda b, p, qi_tab, ki_tab: (b, 0, qi_tab[p])),
            pl.BlockSpec((1, width), lambda *a: (0, 0)),
        ],
        out_specs=pl.BlockSpec((t, width), lambda *a: (q_row(*a), 0)),
        scratch_shapes=[
            pltpu.VMEM((FOX_HEADS, FOX_AUG_DEPTH, t), BF16),
            pltpu.VMEM((FOX_HEADS, t), F32),
            pltpu.VMEM((FOX_HEADS, FOX_HEAD_DIM + FOX_ONES_ROWS, t), F32),
        ],
    )
    return pl.pallas_call(
        _fox_kernel,
        grid_spec=grid_spec,
        out_shape=jax.ShapeDtypeStruct((batch * s_len, width), BF16),
        compiler_params=_cparams(("parallel", "arbitrary")),
        name="fox_attention",
    )(qi_tab, ki_tab, proj, proj, vt, ka, grow, out_norm.reshape(1, width))


def _mlstm_kernel(mqk_ref, mv_ref, mo_ref, gcol_ref, grow_ref, cw_ref, cb_ref, on_ref, out_ref,
                  ct_ref, mg_ref, ext_ref, shift_ref, qk_ref):
    c = pl.program_id(1)
    length = mqk_ref.shape[0]
    dk, dv = MLSTM_QK_DIM, MLSTM_V_DIM
    n_qk = MLSTM_HEADS * dk
    head = ext_ref.shape[0] - length
    tail = BF16_SUBLANES

    @pl.when(c == 0)
    def _():
        ct_ref[...] = jnp.zeros(ct_ref.shape, F32)
        mg_ref[...] = jnp.zeros(mg_ref.shape, F32)
        ext_ref[0:head, :] = jnp.zeros((head, ext_ref.shape[1]), BF16)
        row = lax.broadcasted_iota(jnp.int32, shift_ref.shape[1:], 0)
        col = lax.broadcasted_iota(jnp.int32, shift_ref.shape[1:], 1)
        for s in range(1, CONV_WIDTH):
            shift_ref[s - 1] = jnp.where(col == row + head - s, 1.0, 0.0).astype(BF16)

    u = mqk_ref[...]
    ext_ref[head:, :] = u
    ext = ext_ref[...]
    y = cb_ref[...] + cw_ref[CONV_WIDTH - 1:CONV_WIDTH, :] * u.astype(F32)
    for s in range(1, CONV_WIDTH):
        tap = CONV_WIDTH - 1 - s
        y = y + cw_ref[tap:tap + 1, :] * jnp.dot(shift_ref[s - 1], ext, preferred_element_type=F32)
    ext_ref[head - tail:head, :] = u[length - tail:, :]
    y = y * _sigmoid(y)
    qk_ref[:, 0:n_qk] = y[:, 0:n_qk].astype(BF16)
    qk_ref[:, n_qk:] = (y[:, n_qk:] * dk ** -0.5).astype(BF16)

    keep = (lax.broadcasted_iota(jnp.int32, (length, length), 0)
            >= lax.broadcasted_iota(jnp.int32, (length, length), 1))
    ones = jnp.ones((length, LANES), BF16)

    def scores(h):
        qb = qk_ref[:, h * dk:(h + 1) * dk]
        kb = qk_ref[:, n_qk + h * dk:n_qk + (h + 1) * dk]
        gi, gf = FOX_HEADS + h, FOX_HEADS + MLSTM_HEADS + h
        u_row = grow_ref[gi:gi + 1, :] - grow_ref[gf:gf + 1, :]
        mg = mg_ref[h:h + 1, 0:1]
        um = jnp.where(keep, u_row, NEG_BIG)
        a = jnp.maximum(jnp.max(um, axis=-1, keepdims=True), mg)
        sqk = lax.dot_general(qb, kb, (((1,), (1,)), ((), ())), preferred_element_type=F32) * jnp.exp(um - a)
        mg_new = jnp.maximum(mg, jnp.max(u_row, axis=-1, keepdims=True))
        return sqk, a, mg, mg_new

    def outputs(h, sqk, a, mg, mg_new):
        qb = qk_ref[:, h * dk:(h + 1) * dk]
        kb = qk_ref[:, n_qk + h * dk:n_qk + (h + 1) * dk]
        sl = slice(h * dv, (h + 1) * dv)
        v_aug = jnp.concatenate([mv_ref[:, sl], ones], axis=1)
        gi, gf = FOX_HEADS + h, FOX_HEADS + MLSTM_HEADS + h
        f_col = gcol_ref[:, gf:gf + 1]
        u_col = gcol_ref[:, gi:gi + 1] - f_col
        ct = ct_ref[h]
        inter = jnp.dot(qb, ct.astype(BF16), preferred_element_type=F32)
        nd = jnp.dot(sqk.astype(BF16), v_aug, preferred_element_type=F32) + jnp.exp(mg - a) * inter
        den = jnp.maximum(jnp.abs(nd[:, dv:dv + 1]), jnp.exp(-(f_col + a)))
        hid = nd[:, :dv] * (1.0 / den)
        gate = _sigmoid(mo_ref[:, sl].astype(F32))
        out_ref[:, sl] = (_rms_rows(hid, on_ref[:, sl]) * gate).astype(out_ref.dtype)

        wav = (jnp.exp(u_col - mg_new) * v_aug.astype(F32)).astype(BF16)
        kt = kb.astype(F32).T.astype(BF16)
        ct_ref[h] = jnp.exp(mg - mg_new) * ct + jnp.dot(kt, wav, preferred_element_type=F32)
        mg_ref[h:h + 1, :] = jnp.broadcast_to(mg_new, (1, LANES))

    pending = scores(0)
    for h in range(1, MLSTM_HEADS):
        nxt = scores(h)
        outputs(h - 1, *pending)
        pending = nxt
    outputs(MLSTM_HEADS - 1, *pending)


def _mlstm(proj, gcol, grow, conv_w, conv_b, out_norm, *, batch, s_len):
    length = MLSTM_CHUNK
    nc = s_len // length
    width = MLSTM_HEADS * MLSTM_V_DIM
    qk_width = 2 * MLSTM_HEADS * MLSTM_QK_DIM
    return pl.pallas_call(
        _mlstm_kernel,
        grid=(batch, nc),
        in_specs=[
            pl.BlockSpec((length, qk_width), lambda b, c: (b * nc + c, 2)),
            pl.BlockSpec((length, width), lambda b, c: (b * nc + c, 3)),
            pl.BlockSpec((length, width), lambda b, c: (b * nc + c, 4)),
            pl.BlockSpec((length, LANES), lambda b, c: (b * nc + c, 0)),
            pl.BlockSpec((None, N_GATES, length), lambda b, c: (b, 0, c)),
            pl.BlockSpec((CONV_WIDTH, qk_width), lambda b, c: (0, 0)),
            pl.BlockSpec((1, qk_width), lambda b, c: (0, 0)),
            pl.BlockSpec((1, width), lambda b, c: (0, 0)),
        ],
        out_specs=pl.BlockSpec((length, width), lambda b, c: (b * nc + c, 0)),
        out_shape=jax.ShapeDtypeStruct((batch * s_len, width), BF16),
        scratch_shapes=[
            pltpu.VMEM((MLSTM_HEADS, MLSTM_QK_DIM, MLSTM_V_DIM + LANES), F32),
            pltpu.VMEM((SUBLANES, LANES), F32),
            pltpu.VMEM((LANES + length, qk_width), BF16),
            pltpu.VMEM((CONV_WIDTH - 1, length, LANES + length), BF16),
            pltpu.VMEM((length, qk_width), BF16),
        ],
        compiler_params=_cparams(("parallel", "arbitrary")),
        name="mlstm",
    )(proj, proj, proj, gcol, grow, conv_w, conv_b.reshape(1, qk_width), out_norm.reshape(1, width))


def _xattn_kernel(q_ref, k_ref, v_ref, o_ref):
    d = q_ref.shape[1] // XATTN_HEADS

    def scores(h):
        sl = slice(h * d, (h + 1) * d)
        return lax.dot_general(q_ref[:, sl], k_ref[:, sl], (((1,), (1,)), ((), ())), preferred_element_type=F32)

    def outputs(h, s):
        sl = slice(h * d, (h + 1) * d)
        p = jnp.exp2(s - jnp.max(s, axis=-1, keepdims=True))
        p = p * (1.0 / jnp.sum(p, axis=-1, keepdims=True))
        o_ref[:, sl] = jnp.dot(p.astype(BF16), v_ref[:, sl], preferred_element_type=F32).astype(o_ref.dtype)

    _tile_pipeline(XATTN_HEADS, scores, outputs)


def _xattn(q, kv, *, batch, s_len, n_mem, tq):
    d_model = q.shape[1]
    nq = s_len // tq
    return pl.pallas_call(
        _xattn_kernel,
        grid=(batch, nq),
        in_specs=[
            pl.BlockSpec((tq, d_model), lambda b, i: (b * nq + i, 0)),
            pl.BlockSpec((n_mem, d_model), lambda b, i: (b, 0)),
            pl.BlockSpec((n_mem, d_model), lambda b, i: (b, 1)),
        ],
        out_specs=pl.BlockSpec((tq, d_model), lambda b, i: (b * nq + i, 0)),
        out_shape=jax.ShapeDtypeStruct(q.shape, BF16),
        compiler_params=_cparams(("parallel", "parallel")),
        name="xattn",
    )(q, kv, kv)


def _mlp_kernel(x_ref, g_ref, wu_ref, wd_ref, o_ref, xn_ref):
    f = pl.program_id(1)

    @pl.when(f == 0)
    def _():
        x = x_ref[...]
        xn_ref[...] = _rms_rows(x, g_ref[...]).astype(BF16)
        o_ref[...] = x

    hid = jnp.dot(xn_ref[...], wu_ref[...], preferred_element_type=F32)
    hid = jnp.square(jnp.maximum(hid, 0.0)).astype(BF16)
    o_ref[...] += jnp.dot(hid, wd_ref[...], preferred_element_type=F32)


def _mlp(x, gain, w_up, w_down, *, tm, tf):
    m, d = x.shape
    d_ff = w_up.shape[1]
    return pl.pallas_call(
        _mlp_kernel,
        grid=(m // tm, d_ff // tf),
        in_specs=[
            pl.BlockSpec((tm, d), lambda i, f: (i, 0)),
            pl.BlockSpec((1, d), lambda i, f: (0, 0)),
            pl.BlockSpec((d, tf), lambda i, f: (0, f)),
            pl.BlockSpec((tf, d), lambda i, f: (f, 0)),
        ],
        out_specs=pl.BlockSpec((tm, d), lambda i, f: (i, 0)),
        out_shape=jax.ShapeDtypeStruct((m, d), F32),
        scratch_shapes=[pltpu.VMEM((tm, d), BF16)],
        compiler_params=_cparams(("parallel", "arbitrary")),
        name="mlp",
    )(x, gain.reshape(1, d), w_up, w_down)


def _layer(x, mem, mixer_norm, w_in, fox_f_bias, mlstm_i_bias, mlstm_f_bias, conv_w, conv_b, fox_q_norm,
           fox_k_norm, fox_out_norm, mlstm_out_norm, w_out, xattn_norm, mem_norm, w_xq, w_xkv, xq_norm,
           xk_norm, w_xo, mlp_norm, w_up, w_down, *, batch, s_len, n_mem):
    fox_w = FOX_HEADS * FOX_HEAD_DIM
    mqk_w = 2 * MLSTM_HEADS * MLSTM_QK_DIM
    mv_w = MLSTM_HEADS * MLSTM_V_DIM
    d_model = x.shape[1]

    o_ff = 3 * fox_w
    o_mqk = o_ff + FOX_HEADS
    o_mv = o_mqk + mqk_w
    o_mi = o_mv + mv_w
    o_mo = o_mi + 2 * MLSTM_HEADS
    wb = w_in.astype(BF16)
    w_gates = jnp.concatenate([wb[:, o_ff:o_mqk], wb[:, o_mi:o_mo],
                               jnp.zeros((d_model, LANES - N_GATES), BF16)], axis=1)
    gate_bias = jnp.concatenate([fox_f_bias, mlstm_i_bias, mlstm_f_bias]).reshape(N_GATES, 1)

    proj, vt, gates_raw = _in_proj(x, mixer_norm, wb, wb[:, o_mqk:o_mi], wb[:, o_mo:],
                                   jnp.stack([fox_q_norm, fox_k_norm]), w_gates, tm=256)
    grow, gcol, ka = _gates(gates_raw, gate_bias, batch=batch, s_len=s_len)
    fo = _fox_attention(proj, vt, ka, grow, fox_out_norm, batch=batch, s_len=s_len)
    mh = _mlstm(proj, gcol, grow, conv_w, conv_b, mlstm_out_norm, batch=batch, s_len=s_len)
    x = _proj_residual([fo, mh], w_out, x, tm=512, tn=512)

    xd = d_model // XATTN_HEADS
    q = _norm_proj(x, xattn_norm, w_xq, xq_norm, tm=512, n_norm_tiles=XATTN_HEADS, cast_once=True,
                   post_scale=xd ** -0.5 * LOG2E)
    kv = _norm_proj(mem, mem_norm, w_xkv, xk_norm, tm=256, n_norm_tiles=XATTN_HEADS, cast_once=False)
    co = _xattn(q, kv, batch=batch, s_len=s_len, n_mem=n_mem, tq=1024)
    x = _proj_residual([co], w_xo, x, tm=512, tn=512)

    return _mlp(x, mlp_norm, w_up.astype(BF16), w_down.astype(BF16), tm=1024, tf=512)


def kernel(x, mem, mixer_norm, w_in, fox_f_bias, mlstm_i_bias, mlstm_f_bias, conv_w, conv_b, fox_q_norm,
           fox_k_norm, fox_out_norm, mlstm_out_norm, w_out, xattn_norm, mem_norm, w_xq, w_xkv, xq_norm,
           xk_norm, w_xo, mlp_norm, w_up, w_down):
    batch, s_len, d_model = x.shape
    n_mem = mem.shape[1]
    depth = w_in.shape[0]
    h = x.reshape(batch * s_len, d_model)
    mem2 = mem.reshape(batch * n_mem, d_model)
    for l in range(depth):
        h = _layer(h, mem2, mixer_norm[l], w_in[l], fox_f_bias[l], mlstm_i_bias[l], mlstm_f_bias[l],
                   conv_w[l], conv_b[l], fox_q_norm[l], fox_k_norm[l], fox_out_norm[l], mlstm_out_norm[l],
                   w_out[l], xattn_norm[l], mem_norm[l], w_xq[l], w_xkv[l], xq_norm[l], xk_norm[l], w_xo[l],
                   mlp_norm[l], w_up[l], w_down[l], batch=batch, s_len=s_len, n_mem=n_mem)
    return h.reshape(batch, s_len, d_model)
```

```python
import functools

import jax
import jax.numpy as jnp
from jax import lax
from jax.experimental import pallas as pl
from jax.experimental.pallas import tpu as pltpu

F32 = jnp.float32
BF16 = jnp.bfloat16

EPS = 1e-6
FOX_HEADS = 8
FOX_HEAD_DIM = 128
MLSTM_HEADS = 4
MLSTM_QK_DIM = 128
MLSTM_V_DIM = 256
CONV_WIDTH = 4
XATTN_HEADS = 4
N_GATES = 16
LANES = 128
SUBLANES = 8
BF16_SUBLANES = 16
NEG_BIG = -1e30
LOG2E = 1.4426950408889634
FOX_AUG_DEPTH = 256
FOX_ONES_ROWS = 16

MLSTM_CHUNK = 256
FOX_BLOCK = 1024
FOX_KEY_SUB = 256
IN_PROJ_TILE = 512
VMEM_LIMIT = 56 * 1024 * 1024


def _cparams(sem):
    return pltpu.CompilerParams(dimension_semantics=sem, vmem_limit_bytes=VMEM_LIMIT)


def _rms_rows(x, gain):
    ms = jnp.mean(x * x, axis=-1, keepdims=True)
    return x * lax.rsqrt(ms + EPS) * gain


def _sigmoid(x):
    return 1.0 / (1.0 + jnp.exp(-x))


def _split3(x):
    hi = x.astype(BF16).astype(F32)
    r = x - hi
    mid = r.astype(BF16).astype(F32)
    lo = (r - mid).astype(BF16).astype(F32)
    return hi, mid, lo


def _resident(block_shape):
    return pl.BlockSpec(block_shape, lambda i: (0,) * len(block_shape), pipeline_mode=pl.Buffered(1))


def _tile_pipeline(n_tiles, matmul, epilogue):
    pending = matmul(0)
    for j in range(1, n_tiles):
        nxt = matmul(j)
        epilogue(j - 1, pending)
        pending = nxt
    epilogue(n_tiles - 1, pending)


def _in_proj_kernel(x_ref, g_ref, wf_ref, wm_ref, wo_ref, qk_gain_ref, wg_ref, out_ref, vt_ref, gates_ref):
    seg_w = FOX_HEADS * FOX_HEAD_DIM
    tn = IN_PROJ_TILE
    xn = _rms_rows(x_ref[...], g_ref[...]).astype(BF16)
    sources = ((wf_ref, 2 * seg_w), (wf_ref, 0), (wf_ref, seg_w), (wm_ref, 0), (wm_ref, seg_w), (wo_ref, 0))

    def matmul(j):
        seg, off = divmod(j * tn, seg_w)
        w_ref, col0 = sources[seg]
        return lax.dot_general(xn, w_ref[col0 + off:col0 + off + tn, :], (((1,), (1,)), ((), ())),
                               preferred_element_type=F32)

    def head_norm(acc, gain, post_scale, col0):
        for h in range(tn // FOX_HEAD_DIM):
            sl = slice(h * FOX_HEAD_DIM, (h + 1) * FOX_HEAD_DIM)
            y = _rms_rows(acc[:, sl], gain) * post_scale
            out_ref[:, col0 + h * FOX_HEAD_DIM:col0 + (h + 1) * FOX_HEAD_DIM] = y.astype(out_ref.dtype)

    def epilogue(j, acc):
        seg, off = divmod(j * tn, seg_w)
        if seg == 0:
            vt_ref[off:off + tn, :] = acc.T.astype(vt_ref.dtype)
        elif seg == 1:
            head_norm(acc, qk_gain_ref[0], FOX_HEAD_DIM ** -0.5 * LOG2E, off)
        elif seg == 2:
            head_norm(acc, qk_gain_ref[1], 1.0, seg_w + off)
        else:
            out_ref[:, (seg - 1) * seg_w + off:(seg - 1) * seg_w + off + tn] = acc.astype(out_ref.dtype)

    gates_ref[...] = jnp.dot(xn, wg_ref[...], preferred_element_type=F32)
    _tile_pipeline(len(sources) * seg_w // tn, matmul, epilogue)


def _in_proj(x, gain, w_fox, w_mlstm, w_ogate, qk_gains, w_gates, *, tm):
    m, k = x.shape
    tn = FOX_HEADS * FOX_HEAD_DIM
    n = 3 * tn + w_mlstm.shape[0] + w_ogate.shape[0]
    return pl.pallas_call(
        _in_proj_kernel,
        grid=(m // tm,),
        in_specs=[
            pl.BlockSpec((tm, k), lambda i: (i, 0)),
            _resident((1, k)),
            _resident((3 * tn, k)),
            _resident(w_mlstm.shape),
            _resident(w_ogate.shape),
            _resident((2, 1, FOX_HEAD_DIM)),
            _resident((k, LANES)),
        ],
        out_specs=[
            pl.BlockSpec((tm, n - tn), lambda i: (i, 0)),
            pl.BlockSpec((tn, tm), lambda i: (0, i)),
            pl.BlockSpec((tm, LANES), lambda i: (i, 0)),
        ],
        out_shape=[
            jax.ShapeDtypeStruct((m, n - tn), BF16),
            jax.ShapeDtypeStruct((tn, m), BF16),
            jax.ShapeDtypeStruct((m, LANES), F32),
        ],
        compiler_params=_cparams(("parallel",)),
        name="in_proj",
    )(x, gain.reshape(1, k), w_fox, w_mlstm, w_ogate, qk_gains.reshape(2, 1, FOX_HEAD_DIM), w_gates)


def _bf16_copy(w_ref, wb_ref):
    @pl.when(pl.program_id(0) == 0)
    def _():
        wb_ref[...] = w_ref[...].astype(BF16)
    return wb_ref


def _norm_proj_kernel(x_ref, g_ref, w_ref, hg_ref, out_ref, *scratch, n_norm_tiles, post_scale):
    tn = hg_ref.shape[1]
    wb_ref = _bf16_copy(w_ref, scratch[0]) if scratch else None
    xn = _rms_rows(x_ref[...], g_ref[...]).astype(BF16)

    def matmul(j):
        cols = slice(j * tn, (j + 1) * tn)
        w_tile = wb_ref[:, cols] if scratch else w_ref[:, cols].astype(BF16)
        return jnp.dot(xn, w_tile, preferred_element_type=F32)

    def epilogue(j, acc):
        y = _rms_rows(acc, hg_ref[...]) * post_scale if j < n_norm_tiles else acc
        out_ref[:, j * tn:(j + 1) * tn] = y.astype(out_ref.dtype)

    _tile_pipeline(w_ref.shape[1] // tn, matmul, epilogue)


def _norm_proj(x, gain, w, head_gain, *, tm, n_norm_tiles, cast_once, post_scale=1.0):
    m, k = x.shape
    n = w.shape[1]
    tn = head_gain.shape[0]
    return pl.pallas_call(
        functools.partial(_norm_proj_kernel, n_norm_tiles=n_norm_tiles, post_scale=post_scale),
        grid=(m // tm,),
        in_specs=[
            pl.BlockSpec((tm, k), lambda i: (i, 0)),
            _resident((1, k)),
            _resident((k, n)),
            _resident((1, tn)),
        ],
        out_specs=pl.BlockSpec((tm, n), lambda i: (i, 0)),
        out_shape=jax.ShapeDtypeStruct((m, n), BF16),
        scratch_shapes=[pltpu.VMEM((k, n), BF16)] if cast_once else [],
        compiler_params=_cparams(("arbitrary",) if cast_once else ("parallel",)),
        name="norm_proj",
    )(x, gain.reshape(1, k), w, head_gain.reshape(1, tn))


def _proj_residual_kernel(*refs, n_lhs, tn):
    lhs = refs[:n_lhs]
    w_ref, res_ref, out_ref, wb_ref = refs[n_lhs:]
    kp = lhs[0].shape[1]
    _bf16_copy(w_ref, wb_ref)

    def matmul(j):
        cols = slice(j * tn, (j + 1) * tn)
        acc = jnp.dot(lhs[0][...], wb_ref[0:kp, cols], preferred_element_type=F32)
        for p in range(1, n_lhs):
            acc = acc + jnp.dot(lhs[p][...], wb_ref[p * kp:(p + 1) * kp, cols], preferred_element_type=F32)
        return acc

    def epilogue(j, acc):
        cols = slice(j * tn, (j + 1) * tn)
        out_ref[:, cols] = res_ref[:, cols] + acc

    _tile_pipeline(w_ref.shape[1] // tn, matmul, epilogue)


def _proj_residual(lhs_list, w, residual, *, tm, tn):
    m, n = residual.shape
    n_lhs = len(lhs_list)
    kp = lhs_list[0].shape[1]
    in_specs = [pl.BlockSpec((tm, kp), lambda i: (i, 0)) for _ in lhs_list]
    in_specs += [_resident(w.shape), pl.BlockSpec((tm, n), lambda i: (i, 0))]
    return pl.pallas_call(
        functools.partial(_proj_residual_kernel, n_lhs=n_lhs, tn=tn),
        grid=(m // tm,),
        in_specs=in_specs,
        out_specs=pl.BlockSpec((tm, n), lambda i: (i, 0)),
        out_shape=jax.ShapeDtypeStruct((m, n), F32),
        scratch_shapes=[pltpu.VMEM(w.shape, BF16)],
        compiler_params=_cparams(("arbitrary",)),
        name="proj_residual",
    )(*lhs_list, w, residual)


def _gates_kernel(g_ref, bias_ref, row_ref, col_ref, ka_ref):
    s_len = g_ref.shape[0]
    z = g_ref[...].T[0:N_GATES, :] + bias_ref[...]
    row = lax.broadcasted_iota(jnp.int32, z.shape, 0)
    is_forget = (row < FOX_HEADS) | (row >= FOX_HEADS + MLSTM_HEADS)
    logsig = jnp.minimum(z, 0.0) - jnp.log1p(jnp.exp(-jnp.abs(z)))
    c = jnp.where(is_forget, logsig, 0.0)
    lane = lax.broadcasted_iota(jnp.int32, z.shape, 1)
    shift = 1
    while shift < s_len:
        c = c + jnp.where(lane >= shift, pltpu.roll(c, shift, axis=1), 0.0)
        shift *= 2
    vals = jnp.where(is_forget, c, z)
    row_ref[...] = vals
    padded = jnp.concatenate([vals, jnp.zeros((LANES - N_GATES, s_len), F32)], axis=0)
    col_ref[...] = padded.T
    hi, mid, lo = _split3(c[0:FOX_HEADS, :] * LOG2E)
    aug = jnp.concatenate([jnp.ones((FOX_HEADS, s_len), F32), -hi, -mid, -lo,
                           jnp.zeros((LANES - 4 * FOX_HEADS, s_len), F32)], axis=0)
    ka_ref[...] = aug.T.astype(ka_ref.dtype)


def _gates(gates_raw, bias, *, batch, s_len):
    return pl.pallas_call(
        _gates_kernel,
        grid=(batch,),
        in_specs=[pl.BlockSpec((s_len, LANES), lambda b: (b, 0)),
                  pl.BlockSpec((N_GATES, 1), lambda b: (0, 0))],
        out_specs=[pl.BlockSpec((None, N_GATES, s_len), lambda b: (b, 0, 0)),
                   pl.BlockSpec((s_len, LANES), lambda b: (b, 0)),
                   pl.BlockSpec((s_len, LANES), lambda b: (b, 0))],
        out_shape=[jax.ShapeDtypeStruct((batch, N_GATES, s_len), F32),
                   jax.ShapeDtypeStruct((batch * s_len, LANES), F32),
                   jax.ShapeDtypeStruct((batch * s_len, LANES), BF16)],
        compiler_params=_cparams(("parallel",)),
        name="gates",
    )(gates_raw, bias)


def _fox_kernel(qi_ref, ki_ref, q_ref, k_ref, vt_ref, ka_ref, grow_ref, on_ref, o_ref, qt_ref, m_ref, acc_ref):
    pair = pl.program_id(1)
    qi = qi_ref[pair]
    ki = ki_ref[pair]
    tq, tk = q_ref.shape[0], k_ref.shape[0]
    d = FOX_HEAD_DIM

    @pl.when(ki == 0)
    def _():
        m_ref[...] = jnp.full(m_ref.shape, NEG_BIG, F32)
        acc_ref[...] = jnp.zeros(acc_ref.shape, F32)
        sub = lax.broadcasted_iota(jnp.int32, (SUBLANES, tq), 0)
        for h in range(FOX_HEADS):
            hi, mid, lo = _split3(grow_ref[h:h + 1, :] * LOG2E)
            c_rows = jnp.where(sub == 0, hi, jnp.where(sub == 1, mid, jnp.where(sub == 2, lo, 0.0)))
            pick = jnp.where(sub == h, 1.0, 0.0)
            qt = q_ref[:, h * d:(h + 1) * d].astype(F32).T
            pad = jnp.zeros((FOX_AUG_DEPTH - d - 4 * SUBLANES, tq), F32)
            qt_ref[h] = jnp.concatenate([qt, c_rows, pick, pick, pick, pad], axis=0).astype(BF16)

    def step(diagonal):
        sub = FOX_KEY_SUB
        ones = jnp.ones((FOX_ONES_ROWS, sub), BF16)
        if diagonal:
            keep = (lax.broadcasted_iota(jnp.int32, (sub, tq), 0)
                    <= lax.broadcasted_iota(jnp.int32, (sub, tq), 1))

        def scores(h, ks):
            t0 = ks * sub if diagonal else 0
            rows = slice(ks * sub, (ks + 1) * sub)
            k_aug = jnp.concatenate([k_ref[rows, h * d:(h + 1) * d], ka_ref[rows, :]], axis=1)
            st = jnp.dot(k_aug, qt_ref[h, :, t0:], preferred_element_type=F32)
            if diagonal:
                st = jnp.where(keep[:, 0:tq - t0], st, NEG_BIG)
            m_prev = m_ref[h:h + 1, t0:]
            m_new = jnp.maximum(m_prev, jnp.max(st, axis=0, keepdims=True))
            m_ref[h:h + 1, t0:] = m_new
            return t0, st, m_new, jnp.exp2(m_prev - m_new)

        def accumulate(h, ks, t0, st, m_new, alpha):
            p = jnp.exp2(st - m_new)
            vt_aug = jnp.concatenate([vt_ref[h * d:(h + 1) * d, ks * sub:(ks + 1) * sub], ones], axis=0)
            acc_ref[h, :, t0:] = alpha * acc_ref[h, :, t0:] + jnp.dot(vt_aug, p.astype(BF16),
                                                                      preferred_element_type=F32)

        items = [(h, ks) for h in range(FOX_HEADS) for ks in range(tk // sub)]
        lookahead = 2
        pending = [scores(*it) for it in items[:lookahead]]
        for n, it in enumerate(items):
            if n + lookahead < len(items):
                pending.append(scores(*items[n + lookahead]))
            accumulate(*it, *pending.pop(0))

    @pl.when(ki < qi)
    def _():
        step(False)

    @pl.when(ki == qi)
    def _():
        step(True)
        for h in range(FOX_HEADS):
            sl = slice(h * d, (h + 1) * d)
            o = (acc_ref[h, 0:d, :] / acc_ref[h, d:d + 1, :]).T
            o_ref[:, sl] = _rms_rows(o, on_ref[:, sl]).astype(o_ref.dtype)


def _fox_attention(proj, vt, ka, grow, out_norm, *, batch, s_len):
    t = FOX_BLOCK
    nq = s_len // t
    width = FOX_HEADS * FOX_HEAD_DIM
    pairs = [(qi, ki) for qi in range(nq) for ki in range(qi + 1)]
    qi_tab = jnp.asarray([p[0] for p in pairs], jnp.int32)
    ki_tab = jnp.asarray([p[1] for p in pairs], jnp.int32)

    def q_row(b, p, qi_tab, ki_tab):
        return b * nq + qi_tab[p]

    def k_row(b, p, qi_tab, ki_tab):
        return b * nq + ki_tab[p]

    grid_spec = pltpu.PrefetchScalarGridSpec(
        num_scalar_prefetch=2,
        grid=(batch, len(pairs)),
        in_specs=[
            pl.BlockSpec((t, width), lambda *a: (q_row(*a), 0)),
            pl.BlockSpec((t, width), lambda *a: (k_row(*a), 1)),
            pl.BlockSpec((width, t), lambda *a: (0, k_row(*a))),
            pl.BlockSpec((t, LANES), lambda *a: (k_row(*a), 0)),
            pl.BlockSpec((None, N_GATES, t), lambda b, p, qi_tab, ki_tab: (b, 0, qi_tab[p])),
            pl.BlockSpec((1, width), lambda *a: (0, 0)),
        ],
        out_specs=pl.BlockSpec((t, width), lambda *a: (q_row(*a), 0)),
        scratch_shapes=[
            pltpu.VMEM((FOX_HEADS, FOX_AUG_DEPTH, t), BF16),
            pltpu.VMEM((FOX_HEADS, t), F32),
            pltpu.VMEM((FOX_HEADS, FOX_HEAD_DIM + FOX_ONES_ROWS, t), F32),
        ],
    )
    return pl.pallas_call(
        _fox_kernel,
        grid_spec=grid_spec,
        out_shape=jax.ShapeDtypeStruct((batch * s_len, width), BF16),
        compiler_params=_cparams(("parallel", "arbitrary")),
        name="fox_attention",
    )(qi_tab, ki_tab, proj, proj, vt, ka, grow, out_norm.reshape(1, width))


def _mlstm_kernel(mqk_ref, mv_ref, mo_ref, gcol_ref, grow_ref, cw_ref, cb_ref, on_ref, out_ref,
                  ct_ref, mg_ref, ext_ref, shift_ref, qk_ref):
    c = pl.program_id(1)
    length = mqk_ref.shape[0]
    dk, dv = MLSTM_QK_DIM, MLSTM_V_DIM
    n_qk = MLSTM_HEADS * dk
    head = ext_ref.shape[0] - length
    tail = BF16_SUBLANES

    @pl.when(c == 0)
    def _():
        ct_ref[...] = jnp.zeros(ct_ref.shape, F32)
        mg_ref[...] = jnp.zeros(mg_ref.shape, F32)
        ext_ref[0:head, :] = jnp.zeros((head, ext_ref.shape[1]), BF16)
        row = lax.broadcasted_iota(jnp.int32, shift_ref.shape[1:], 0)
        col = lax.broadcasted_iota(jnp.int32, shift_ref.shape[1:], 1)
        for s in range(1, CONV_WIDTH):
            shift_ref[s - 1] = jnp.where(col == row + head - s, 1.0, 0.0).astype(BF16)

    u = mqk_ref[...]
    ext_ref[head:, :] = u
    ext = ext_ref[...]
    y = cb_ref[...] + cw_ref[CONV_WIDTH - 1:CONV_WIDTH, :] * u.astype(F32)
    for s in range(1, CONV_WIDTH):
        tap = CONV_WIDTH - 1 - s
        y = y + cw_ref[tap:tap + 1, :] * jnp.dot(shift_ref[s - 1], ext, preferred_element_type=F32)
    ext_ref[head - tail:head, :] = u[length - tail:, :]
    y = y * _sigmoid(y)
    qk_ref[:, 0:n_qk] = y[:, 0:n_qk].astype(BF16)
    qk_ref[:, n_qk:] = (y[:, n_qk:] * dk ** -0.5).astype(BF16)

    keep = (lax.broadcasted_iota(jnp.int32, (length, length), 0)
            >= lax.broadcasted_iota(jnp.int32, (length, length), 1))
    ones = jnp.ones((length, LANES), BF16)

    def scores(h):
        qb = qk_ref[:, h * dk:(h + 1) * dk]
        kb = qk_ref[:, n_qk + h * dk:n_qk + (h + 1) * dk]
        gi, gf = FOX_HEADS + h, FOX_HEADS + MLSTM_HEADS + h
        u_row = grow_ref[gi:gi + 1, :] - grow_ref[gf:gf + 1, :]
        mg = mg_ref[h:h + 1, 0:1]
        um = jnp.where(keep, u_row, NEG_BIG)
        a = jnp.maximum(jnp.max(um, axis=-1, keepdims=True), mg)
        sqk = lax.dot_general(qb, kb, (((1,), (1,)), ((), ())), preferred_element_type=F32) * jnp.exp(um - a)
        mg_new = jnp.maximum(mg, jnp.max(u_row, axis=-1, keepdims=True))
        return sqk, a, mg, mg_new

    def outputs(h, sqk, a, mg, mg_new):
        qb = qk_ref[:, h * dk:(h + 1) * dk]
        kb = qk_ref[:, n_qk + h * dk:n_qk + (h + 1) * dk]
        sl = slice(h * dv, (h + 1) * dv)
        v_aug = jnp.concatenate([mv_ref[:, sl], ones], axis=1)
        gi, gf = FOX_HEADS + h, FOX_HEADS + MLSTM_HEADS + h
        f_col = gcol_ref[:, gf:gf + 1]
        u_col = gcol_ref[:, gi:gi + 1] - f_col
        ct = ct_ref[h]
        inter = jnp.dot(qb, ct.astype(BF16), preferred_element_type=F32)
        nd = jnp.dot(sqk.astype(BF16), v_aug, preferred_element_type=F32) + jnp.exp(mg - a) * inter
        den = jnp.maximum(jnp.abs(nd[:, dv:dv + 1]), jnp.exp(-(f_col + a)))
        hid = nd[:, :dv] * (1.0 / den)
        gate = _sigmoid(mo_ref[:, sl].astype(F32))
        out_ref[:, sl] = (_rms_rows(hid, on_ref[:, sl]) * gate).astype(out_ref.dtype)

        wav = (jnp.exp(u_col - mg_new) * v_aug.astype(F32)).astype(BF16)
        kt = kb.astype(F32).T.astype(BF16)
        ct_ref[h] = jnp.exp(mg - mg_new) * ct + jnp.dot(kt, wav, preferred_element_type=F32)
        mg_ref[h:h + 1, :] = jnp.broadcast_to(mg_new, (1, LANES))

    pending = scores(0)
    for h in range(1, MLSTM_HEADS):
        nxt = scores(h)
        outputs(h - 1, *pending)
        pending = nxt
    outputs(MLSTM_HEADS - 1, *pending)


def _mlstm(proj, gcol, grow, conv_w, conv_b, out_norm, *, batch, s_len):
    length = MLSTM_CHUNK
    nc = s_len // length
    width = MLSTM_HEADS * MLSTM_V_DIM
    qk_width = 2 * MLSTM_HEADS * MLSTM_QK_DIM
    return pl.pallas_call(
        _mlstm_kernel,
        grid=(batch, nc),
        in_specs=[
            pl.BlockSpec((length, qk_width), lambda b, c: (b * nc + c, 2)),
            pl.BlockSpec((length, width), lambda b, c: (b * nc + c, 3)),
            pl.BlockSpec((length, width), lambda b, c: (b * nc + c, 4)),
            pl.BlockSpec((length, LANES), lambda b, c: (b * nc + c, 0)),
            pl.BlockSpec((None, N_GATES, length), lambda b, c: (b, 0, c)),
            pl.BlockSpec((CONV_WIDTH, qk_width), lambda b, c: (0, 0)),
            pl.BlockSpec((1, qk_width), lambda b, c: (0, 0)),
            pl.BlockSpec((1, width), lambda b, c: (0, 0)),
        ],
        out_specs=pl.BlockSpec((length, width), lambda b, c: (b * nc + c, 0)),
        out_shape=jax.ShapeDtypeStruct((batch * s_len, width), BF16),
        scratch_shapes=[
            pltpu.VMEM((MLSTM_HEADS, MLSTM_QK_DIM, MLSTM_V_DIM + LANES), F32),
            pltpu.VMEM((SUBLANES, LANES), F32),
            pltpu.VMEM((LANES + length, qk_width), BF16),
            pltpu.VMEM((CONV_WIDTH - 1, length, LANES + length), BF16),
            pltpu.VMEM((length, qk_width), BF16),
        ],
        compiler_params=_cparams(("parallel", "arbitrary")),
        name="mlstm",
    )(proj, proj, proj, gcol, grow, conv_w, conv_b.reshape(1, qk_width), out_norm.reshape(1, width))


def _xattn_kernel(q_ref, k_ref, v_ref, o_ref):
    d = q_ref.shape[1] // XATTN_HEADS

    def scores(h):
        sl = slice(h * d, (h + 1) * d)
        return lax.dot_general(q_ref[:, sl], k_ref[:, sl], (((1,), (1,)), ((), ())), preferred_element_type=F32)

    def outputs(h, s):
        sl = slice(h * d, (h + 1) * d)
        p = jnp.exp2(s - jnp.max(s, axis=-1, keepdims=True))
        p = p * (1.0 / jnp.sum(p, axis=-1, keepdims=True))
        o_ref[:, sl] = jnp.dot(p.astype(BF16), v_ref[:, sl], preferred_element_type=F32).astype(o_ref.dtype)

    _tile_pipeline(XATTN_HEADS, scores, outputs)


def _xattn(q, kv, *, batch, s_len, n_mem, tq):
    d_model = q.shape[1]
    nq = s_len // tq
    return pl.pallas_call(
        _xattn_kernel,
        grid=(batch, nq),
        in_specs=[
            pl.BlockSpec((tq, d_model), lambda b, i: (b * nq + i, 0)),
            pl.BlockSpec((n_mem, d_model), lambda b, i: (b, 0)),
            pl.BlockSpec((n_mem, d_model), lambda b, i: (b, 1)),
        ],
        out_specs=pl.BlockSpec((tq, d_model), lambda b, i: (b * nq + i, 0)),
        out_shape=jax.ShapeDtypeStruct(q.shape, BF16),
        compiler_params=_cparams(("parallel", "parallel")),
        name="xattn",
    )(q, kv, kv)


def _mlp_kernel(x_ref, g_ref, wu_ref, wd_ref, o_ref, xn_ref):
    f = pl.program_id(1)

    @pl.when(f == 0)
    def _():
        x = x_ref[...]
        xn_ref[...] = _rms_rows(x, g_ref[...]).astype(BF16)
        o_ref[...] = x

    hid = jnp.dot(xn_ref[...], wu_ref[...], preferred_element_type=F32)
    hid = jnp.square(jnp.maximum(hid, 0.0)).astype(BF16)
    o_ref[...] += jnp.dot(hid, wd_ref[...], preferred_element_type=F32)


def _mlp(x, gain, w_up, w_down, *, tm, tf):
    m, d = x.shape
    d_ff = w_up.shape[1]
    return pl.pallas_call(
        _mlp_kernel,
        grid=(m // tm, d_ff // tf),
        in_specs=[
            pl.BlockSpec((tm, d), lambda i, f: (i, 0)),
            pl.BlockSpec((1, d), lambda i, f: (0, 0)),
            pl.BlockSpec((d, tf), lambda i, f: (0, f)),
            pl.BlockSpec((tf, d), lambda i, f: (f, 0)),
        ],
        out_specs=pl.BlockSpec((tm, d), lambda i, f: (i, 0)),
        out_shape=jax.ShapeDtypeStruct((m, d), F32),
        scratch_shapes=[pltpu.VMEM((tm, d), BF16)],
        compiler_params=_cparams(("parallel", "arbitrary")),
        name="mlp",
    )(x, gain.reshape(1, d), w_up, w_down)


def _layer(x, mem, mixer_norm, w_in, fox_f_bias, mlstm_i_bias, mlstm_f_bias, conv_w, conv_b, fox_q_norm,
           fox_k_norm, fox_out_norm, mlstm_out_norm, w_out, xattn_norm, mem_norm, w_xq, w_xkv, xq_norm,
           xk_norm, w_xo, mlp_norm, w_up, w_down, *, batch, s_len, n_mem):
    fox_w = FOX_HEADS * FOX_HEAD_DIM
    mqk_w = 2 * MLSTM_HEADS * MLSTM_QK_DIM
    mv_w = MLSTM_HEADS * MLSTM_V_DIM
    d_model = x.shape[1]

    o_ff = 3 * fox_w
    o_mqk = o_ff + FOX_HEADS
    o_mv = o_mqk + mqk_w
    o_mi = o_mv + mv_w
    o_mo = o_mi + 2 * MLSTM_HEADS
    wt = w_in.T
    w_gates = jnp.concatenate([wt[o_ff:o_mqk].T, wt[o_mi:o_mo].T, jnp.zeros((d_model, LANES - N_GATES), F32)],
                              axis=1).astype(BF16)
    gate_bias = jnp.concatenate([fox_f_bias, mlstm_i_bias, mlstm_f_bias]).reshape(N_GATES, 1)

    proj, vt, gates_raw = _in_proj(x, mixer_norm, wt[:o_ff].astype(BF16), wt[o_mqk:o_mi].astype(BF16),
                                   wt[o_mo:].astype(BF16), jnp.stack([fox_q_norm, fox_k_norm]), w_gates, tm=256)
    grow, gcol, ka = _gates(gates_raw, gate_bias, batch=batch, s_len=s_len)
    fo = _fox_attention(proj, vt, ka, grow, fox_out_norm, batch=batch, s_len=s_len)
    mh = _mlstm(proj, gcol, grow, conv_w, conv_b, mlstm_out_norm, batch=batch, s_len=s_len)
    x = _proj_residual([fo, mh], w_out, x, tm=512, tn=512)

    xd = d_model // XATTN_HEADS
    q = _norm_proj(x, xattn_norm, w_xq, xq_norm, tm=512, n_norm_tiles=XATTN_HEADS, cast_once=True,
                   post_scale=xd ** -0.5 * LOG2E)
    kv = _norm_proj(mem, mem_norm, w_xkv, xk_norm, tm=256, n_norm_tiles=XATTN_HEADS, cast_once=False)
    co = _xattn(q, kv, batch=batch, s_len=s_len, n_mem=n_mem, tq=1024)
    x = _proj_residual([co], w_xo, x, tm=512, tn=512)

    return _mlp(x, mlp_norm, w_up.astype(BF16), w_down.astype(BF16), tm=1024, tf=512)


def kernel(x, mem, mixer_norm, w_in, fox_f_bias, mlstm_i_bias, mlstm_f_bias, conv_w, conv_b, fox_q_norm,
           fox_k_norm, fox_out_norm, mlstm_out_norm, w_out, xattn_norm, mem_norm, w_xq, w_xkv, xq_norm,
           xk_norm, w_xo, mlp_norm, w_up, w_down):
    batch, s_len, d_model = x.shape
    n_mem = mem.shape[1]
    depth = w_in.shape[0]
    h = x.reshape(batch * s_len, d_model)
    mem2 = mem.reshape(batch * n_mem, d_model)
    for l in range(depth):
        h = _layer(h, mem2, mixer_norm[l], w_in[l], fox_f_bias[l], mlstm_i_bias[l], mlstm_f_bias[l],
                   conv_w[l], conv_b[l], fox_q_norm[l], fox_k_norm[l], fox_out_norm[l], mlstm_out_norm[l],
                   w_out[l], xattn_norm[l], mem_norm[l], w_xq[l], w_xkv[l], xq_norm[l], xk_norm[l], w_xo[l],
                   mlp_norm[l], w_up[l], w_down[l], batch=batch, s_len=s_len, n_mem=n_mem)
    return h.reshape(batch, s_len, d_model)
```

```python
import functools

import jax
import jax.numpy as jnp
from jax import lax
from jax.experimental import pallas as pl
from jax.experimental.pallas import tpu as pltpu

F32 = jnp.float32
BF16 = jnp.bfloat16

EPS = 1e-6
FOX_HEADS = 8
FOX_HEAD_DIM = 128
MLSTM_HEADS = 4
MLSTM_QK_DIM = 128
MLSTM_V_DIM = 256
CONV_WIDTH = 4
XATTN_HEADS = 4
N_GATES = 16
LANES = 128
SUBLANES = 8
BF16_SUBLANES = 16
NEG_BIG = -1e30
LOG2E = 1.4426950408889634
FOX_AUG_DEPTH = 256
FOX_ONES_ROWS = 16

MLSTM_CHUNK = 256
FOX_BLOCK = 512
FOX_KEY_SUB = 256
IN_PROJ_TILE = 512
MLP_CHUNK = 256
VMEM_LIMIT = 56 * 1024 * 1024


def _cparams(sem):
    return pltpu.CompilerParams(dimension_semantics=sem, vmem_limit_bytes=VMEM_LIMIT)


def _rms_rows(x, gain):
    ms = jnp.mean(x * x, axis=-1, keepdims=True)
    return x * lax.rsqrt(ms + EPS) * gain


def _sigmoid(x):
    return 1.0 / (1.0 + jnp.exp(-x))


def _split3(x):
    hi = x.astype(BF16).astype(F32)
    r = x - hi
    mid = r.astype(BF16).astype(F32)
    lo = (r - mid).astype(BF16).astype(F32)
    return hi, mid, lo


def _resident(block_shape):
    return pl.BlockSpec(block_shape, lambda i: (0,) * len(block_shape), pipeline_mode=pl.Buffered(1))


def _tile_pipeline(n_tiles, matmul, epilogue):
    pending = matmul(0)
    for j in range(1, n_tiles):
        nxt = matmul(j)
        epilogue(j - 1, pending)
        pending = nxt
    epilogue(n_tiles - 1, pending)


def _in_proj_kernel(x_ref, g_ref, wf_ref, wm_ref, wo_ref, qk_gain_ref, wg_ref, out_ref, vt_ref, gates_ref):
    seg_w = FOX_HEADS * FOX_HEAD_DIM
    tn = IN_PROJ_TILE
    xn = _rms_rows(x_ref[...], g_ref[...]).astype(BF16)
    sources = ((wf_ref, 2 * seg_w), (wf_ref, 0), (wf_ref, seg_w), (wm_ref, 0), (wm_ref, seg_w), (wo_ref, 0))

    def matmul(j):
        seg, off = divmod(j * tn, seg_w)
        w_ref, col0 = sources[seg]
        return jnp.dot(xn, w_ref[:, col0 + off:col0 + off + tn], preferred_element_type=F32)

    def head_norm(acc, gain, post_scale, col0):
        for h in range(tn // FOX_HEAD_DIM):
            sl = slice(h * FOX_HEAD_DIM, (h + 1) * FOX_HEAD_DIM)
            y = _rms_rows(acc[:, sl], gain) * post_scale
            out_ref[:, col0 + h * FOX_HEAD_DIM:col0 + (h + 1) * FOX_HEAD_DIM] = y.astype(out_ref.dtype)

    def epilogue(j, acc):
        seg, off = divmod(j * tn, seg_w)
        if seg == 0:
            vt_ref[off:off + tn, :] = acc.T.astype(vt_ref.dtype)
        elif seg == 1:
            head_norm(acc, qk_gain_ref[0], FOX_HEAD_DIM ** -0.5 * LOG2E, off)
        elif seg == 2:
            head_norm(acc, qk_gain_ref[1], 1.0, seg_w + off)
        else:
            out_ref[:, (seg - 1) * seg_w + off:(seg - 1) * seg_w + off + tn] = acc.astype(out_ref.dtype)

    gates_ref[...] = jnp.dot(xn, wg_ref[...], preferred_element_type=F32)
    _tile_pipeline(len(sources) * seg_w // tn, matmul, epilogue)


def _in_proj(x, gain, w_all, w_mlstm, w_ogate, qk_gains, w_gates, *, tm):
    m, k = x.shape
    tn = FOX_HEADS * FOX_HEAD_DIM
    n = 3 * tn + w_mlstm.shape[1] + w_ogate.shape[1]
    return pl.pallas_call(
        _in_proj_kernel,
        grid=(m // tm,),
        in_specs=[
            pl.BlockSpec((tm, k), lambda i: (i, 0)),
            _resident((1, k)),
            _resident((k, 3 * tn)),
            _resident(w_mlstm.shape),
            _resident(w_ogate.shape),
            _resident((2, 1, FOX_HEAD_DIM)),
            _resident((k, LANES)),
        ],
        out_specs=[
            pl.BlockSpec((tm, n - tn), lambda i: (i, 0)),
            pl.BlockSpec((tn, tm), lambda i: (0, i)),
            pl.BlockSpec((tm, LANES), lambda i: (i, 0)),
        ],
        out_shape=[
            jax.ShapeDtypeStruct((m, n - tn), BF16),
            jax.ShapeDtypeStruct((tn, m), BF16),
            jax.ShapeDtypeStruct((m, LANES), F32),
        ],
        compiler_params=_cparams(("parallel",)),
        name="in_proj",
    )(x, gain.reshape(1, k), w_all, w_mlstm, w_ogate, qk_gains.reshape(2, 1, FOX_HEAD_DIM), w_gates)


def _bf16_copy(w_ref, wb_ref):
    @pl.when(pl.program_id(0) == 0)
    def _():
        wb_ref[...] = w_ref[...].astype(BF16)
    return wb_ref


def _norm_proj_kernel(x_ref, g_ref, w_ref, hg_ref, out_ref, wb_ref, *, post_scale):
    tn = hg_ref.shape[1]
    _bf16_copy(w_ref, wb_ref)
    xn = _rms_rows(x_ref[...], g_ref[...]).astype(BF16)

    def matmul(j):
        return jnp.dot(xn, wb_ref[:, j * tn:(j + 1) * tn], preferred_element_type=F32)

    def epilogue(j, acc):
        out_ref[:, j * tn:(j + 1) * tn] = (_rms_rows(acc, hg_ref[...]) * post_scale).astype(out_ref.dtype)

    _tile_pipeline(w_ref.shape[1] // tn, matmul, epilogue)


def _norm_proj(x, gain, w, head_gain, *, tm, post_scale):
    m, k = x.shape
    n = w.shape[1]
    tn = head_gain.shape[0]
    return pl.pallas_call(
        functools.partial(_norm_proj_kernel, post_scale=post_scale),
        grid=(m // tm,),
        in_specs=[
            pl.BlockSpec((tm, k), lambda i: (i, 0)),
            _resident((1, k)),
            _resident((k, n)),
            _resident((1, tn)),
        ],
        out_specs=pl.BlockSpec((tm, n), lambda i: (i, 0)),
        out_shape=jax.ShapeDtypeStruct((m, n), BF16),
        scratch_shapes=[pltpu.VMEM((k, n), BF16)],
        compiler_params=_cparams(("arbitrary",)),
        name="norm_proj",
    )(x, gain.reshape(1, k), w, head_gain.reshape(1, tn))


def _kv_proj_kernel(x_ref, g_ref, w_ref, hg_ref, out_ref, xn_ref, *, n_norm_tiles):
    j = pl.program_id(0)

    @pl.when(j == 0)
    def _():
        xn_ref[...] = _rms_rows(x_ref[...], g_ref[...]).astype(BF16)

    acc = jnp.dot(xn_ref[...], w_ref[...].astype(BF16), preferred_element_type=F32)

    @pl.when(j < n_norm_tiles)
    def _():
        out_ref[...] = _rms_rows(acc, hg_ref[...]).astype(out_ref.dtype)

    @pl.when(j >= n_norm_tiles)
    def _():
        out_ref[...] = acc.astype(out_ref.dtype)


def _kv_proj(x, gain, w, head_gain, *, n_norm_tiles):
    m, k = x.shape
    n = w.shape[1]
    tn = head_gain.shape[0]
    return pl.pallas_call(
        functools.partial(_kv_proj_kernel, n_norm_tiles=n_norm_tiles),
        grid=(n // tn,),
        in_specs=[
            _resident((m, k)),
            _resident((1, k)),
            pl.BlockSpec((k, tn), lambda j: (0, j)),
            _resident((1, tn)),
        ],
        out_specs=pl.BlockSpec((m, tn), lambda j: (0, j)),
        out_shape=jax.ShapeDtypeStruct((m, n), BF16),
        scratch_shapes=[pltpu.VMEM((m, k), BF16)],
        compiler_params=_cparams(("arbitrary",)),
        name="kv_proj",
    )(x, gain.reshape(1, k), w, head_gain.reshape(1, tn))


def _proj_residual_kernel(*refs, n_lhs, tn):
    lhs = refs[:n_lhs]
    w_ref, res_ref, out_ref, wb_ref = refs[n_lhs:]
    kp = lhs[0].shape[1]
    _bf16_copy(w_ref, wb_ref)

    def matmul(j):
        cols = slice(j * tn, (j + 1) * tn)
        acc = jnp.dot(lhs[0][...], wb_ref[0:kp, cols], preferred_element_type=F32)
        for p in range(1, n_lhs):
            acc = acc + jnp.dot(lhs[p][...], wb_ref[p * kp:(p + 1) * kp, cols], preferred_element_type=F32)
        return acc

    def epilogue(j, acc):
        cols = slice(j * tn, (j + 1) * tn)
        out_ref[:, cols] = res_ref[:, cols] + acc

    _tile_pipeline(w_ref.shape[1] // tn, matmul, epilogue)


def _proj_residual(lhs_list, w, residual, *, tm, tn):
    m, n = residual.shape
    n_lhs = len(lhs_list)
    kp = lhs_list[0].shape[1]
    in_specs = [pl.BlockSpec((tm, kp), lambda i: (i, 0)) for _ in lhs_list]
    in_specs += [_resident(w.shape), pl.BlockSpec((tm, n), lambda i: (i, 0))]
    return pl.pallas_call(
        functools.partial(_proj_residual_kernel, n_lhs=n_lhs, tn=tn),
        grid=(m // tm,),
        in_specs=in_specs,
        out_specs=pl.BlockSpec((tm, n), lambda i: (i, 0)),
        out_shape=jax.ShapeDtypeStruct((m, n), F32),
        scratch_shapes=[pltpu.VMEM(w.shape, BF16)],
        compiler_params=_cparams(("arbitrary",)),
        name="proj_residual",
    )(*lhs_list, w, residual)


def _gates_kernel(g_ref, bias_ref, row_ref, col_ref, ka_ref):
    s_len = g_ref.shape[0]
    z = g_ref[...].T[0:N_GATES, :] + bias_ref[...]
    row = lax.broadcasted_iota(jnp.int32, z.shape, 0)
    is_forget = (row < FOX_HEADS) | (row >= FOX_HEADS + MLSTM_HEADS)
    logsig = jnp.minimum(z, 0.0) - jnp.log1p(jnp.exp(-jnp.abs(z)))
    c = jnp.where(is_forget, logsig, 0.0)
    lane = lax.broadcasted_iota(jnp.int32, z.shape, 1)
    shift = 1
    while shift < s_len:
        c = c + jnp.where(lane >= shift, pltpu.roll(c, shift, axis=1), 0.0)
        shift *= 2
    vals = jnp.where(is_forget, c, z)
    row_ref[...] = vals
    padded = jnp.concatenate([vals, jnp.zeros((LANES - N_GATES, s_len), F32)], axis=0)
    col_ref[...] = padded.T
    hi, mid, lo = _split3(c[0:FOX_HEADS, :] * LOG2E)
    aug = jnp.concatenate([jnp.ones((FOX_HEADS, s_len), F32), -hi, -mid, -lo,
                           jnp.zeros((LANES - 4 * FOX_HEADS, s_len), F32)], axis=0)
    ka_ref[...] = aug.T.astype(ka_ref.dtype)


def _gates(gates_raw, bias, *, batch, s_len):
    return pl.pallas_call(
        _gates_kernel,
        grid=(batch,),
        in_specs=[pl.BlockSpec((s_len, LANES), lambda b: (b, 0)),
                  pl.BlockSpec((N_GATES, 1), lambda b: (0, 0))],
        out_specs=[pl.BlockSpec((None, N_GATES, s_len), lambda b: (b, 0, 0)),
                   pl.BlockSpec((s_len, LANES), lambda b: (b, 0)),
                   pl.BlockSpec((s_len, LANES), lambda b: (b, 0))],
        out_shape=[jax.ShapeDtypeStruct((batch, N_GATES, s_len), F32),
                   jax.ShapeDtypeStruct((batch * s_len, LANES), F32),
                   jax.ShapeDtypeStruct((batch * s_len, LANES), BF16)],
        compiler_params=_cparams(("parallel",)),
        name="gates",
    )(gates_raw, bias)


def _fox_kernel(qi_ref, ki_ref, q_ref, k_ref, vt_ref, ka_ref, grow_ref, on_ref, o_ref, qt_ref, m_ref, acc_ref):
    pair = pl.program_id(1)
    qi = qi_ref[pair]
    ki = ki_ref[pair]
    tq, tk = q_ref.shape[0], k_ref.shape[0]
    d = FOX_HEAD_DIM

    @pl.when(ki == 0)
    def _():
        m_ref[...] = jnp.full(m_ref.shape, NEG_BIG, F32)
        acc_ref[...] = jnp.zeros(acc_ref.shape, F32)
        sub = lax.broadcasted_iota(jnp.int32, (SUBLANES, tq), 0)
        for h in range(FOX_HEADS):
            hi, mid, lo = _split3(grow_ref[h:h + 1, :] * LOG2E)
            c_rows = jnp.where(sub == 0, hi, jnp.where(sub == 1, mid, jnp.where(sub == 2, lo, 0.0)))
            pick = jnp.where(sub == h, 1.0, 0.0)
            qt = q_ref[:, h * d:(h + 1) * d].astype(F32).T
            pad = jnp.zeros((FOX_AUG_DEPTH - d - 4 * SUBLANES, tq), F32)
            qt_ref[h] = jnp.concatenate([qt, c_rows, pick, pick, pick, pad], axis=0).astype(BF16)

    def step(diagonal):
        sub = FOX_KEY_SUB
        ones = jnp.ones((FOX_ONES_ROWS, sub), BF16)
        if diagonal:
            keep = (lax.broadcasted_iota(jnp.int32, (sub, tq), 0)
                    <= lax.broadcasted_iota(jnp.int32, (sub, tq), 1))

        def scores(h, ks):
            t0 = ks * sub if diagonal else 0
            rows = slice(ks * sub, (ks + 1) * sub)
            k_aug = jnp.concatenate([k_ref[rows, h * d:(h + 1) * d], ka_ref[rows, :]], axis=1)
            st = jnp.dot(k_aug, qt_ref[h, :, t0:], preferred_element_type=F32)
            if diagonal:
                st = jnp.where(keep[:, 0:tq - t0], st, NEG_BIG)
            m_prev = m_ref[h:h + 1, t0:]
            m_new = jnp.maximum(m_prev, jnp.max(st, axis=0, keepdims=True))
            m_ref[h:h + 1, t0:] = m_new
            return t0, st, m_new, jnp.exp2(m_prev - m_new)

        def accumulate(h, ks, t0, st, m_new, alpha):
            p = jnp.exp2(st - m_new)
            vt_aug = jnp.concatenate([vt_ref[h * d:(h + 1) * d, ks * sub:(ks + 1) * sub], ones], axis=0)
            acc_ref[h, :, t0:] = alpha * acc_ref[h, :, t0:] + jnp.dot(vt_aug, p.astype(BF16),
                                                                      preferred_element_type=F32)

        items = [(h, ks) for h in range(FOX_HEADS) for ks in range(tk // sub)]
        lookahead = 2
        pending = [scores(*it) for it in items[:lookahead]]
        for n, it in enumerate(items):
            if n + lookahead < len(items):
                pending.append(scores(*items[n + lookahead]))
            accumulate(*it, *pending.pop(0))

    @pl.when(ki < qi)
    def _():
        step(False)

    @pl.when(ki == qi)
    def _():
        step(True)
        for h in range(FOX_HEADS):
            sl = slice(h * d, (h + 1) * d)
            ot = acc_ref[h, 0:d, :] * (1.0 / acc_ref[h, d:d + 1, :])
            ot = ot * lax.rsqrt(jnp.mean(ot * ot, axis=0, keepdims=True) + EPS)
            ot = ot * jnp.tile(on_ref[sl, :], (1, tq // LANES))
            o_ref[:, sl] = ot.T.astype(o_ref.dtype)


def _fox_attention(proj, vt, ka, grow, out_norm, *, batch, s_len):
    t = FOX_BLOCK
    nq = s_len // t
    width = FOX_HEADS * FOX_HEAD_DIM
    pairs = [(qi, ki) for qi in range(nq) for ki in range(qi + 1)]
    qi_tab = jnp.asarray([p[0] for p in pairs], jnp.int32)
    ki_tab = jnp.asarray([p[1] for p in pairs], jnp.int32)

    def q_row(b, p, qi_tab, ki_tab):
        return b * nq + qi_tab[p]

    def k_row(b, p, qi_tab, ki_tab):
        return b * nq + ki_tab[p]

    grid_spec = pltpu.PrefetchScalarGridSpec(
        num_scalar_prefetch=2,
        grid=(batch, len(pairs)),
        in_specs=[
            pl.BlockSpec((t, width), lambda *a: (q_row(*a), 0)),
            pl.BlockSpec((t, width), lambda *a: (k_row(*a), 1)),
            pl.BlockSpec((width, t), lambda *a: (0, k_row(*a))),
            pl.BlockSpec((t, LANES), lambda *a: (k_row(*a), 0)),
            pl.BlockSpec((None, N_GATES, t), lambda b, p, qi_tab, ki_tab: (b, 0, qi_tab[p])),
            pl.BlockSpec((width, LANES), lambda *a: (0, 0)),
        ],
        out_specs=pl.BlockSpec((t, width), lambda *a: (q_row(*a), 0)),
        scratch_shapes=[
            pltpu.VMEM((FOX_HEADS, FOX_AUG_DEPTH, t), BF16),
            pltpu.VMEM((FOX_HEADS, t), F32),
            pltpu.VMEM((FOX_HEADS, FOX_HEAD_DIM + FOX_ONES_ROWS, t), F32),
        ],
    )
    return pl.pallas_call(
        _fox_kernel,
        grid_spec=grid_spec,
        out_shape=jax.ShapeDtypeStruct((batch * s_len, width), BF16),
        compiler_params=_cparams(("parallel", "arbitrary")),
        name="fox_attention",
    )(qi_tab, ki_tab, proj, proj, vt, ka, grow, jnp.broadcast_to(out_norm.reshape(width, 1), (width, LANES)))


def _mlstm_kernel(mqk_ref, mv_ref, mo_ref, gcol_ref, grow_ref, cw_ref, cb_ref, on_ref, out_ref,
                  ct_ref, mg_ref, ext_ref, shift_ref, qk_ref):
    c = pl.program_id(1)
    length = mqk_ref.shape[0]
    dk, dv = MLSTM_QK_DIM, MLSTM_V_DIM
    n_qk = MLSTM_HEADS * dk
    head = ext_ref.shape[0] - length
    tail = BF16_SUBLANES

    @pl.when(c == 0)
    def _():
        ct_ref[...] = jnp.zeros(ct_ref.shape, F32)
        mg_ref[...] = jnp.zeros(mg_ref.shape, F32)
        ext_ref[0:head, :] = jnp.zeros((head, ext_ref.shape[1]), BF16)
        row = lax.broadcasted_iota(jnp.int32, shift_ref.shape[1:], 0)
        col = lax.broadcasted_iota(jnp.int32, shift_ref.shape[1:], 1)
        for s in range(1, CONV_WIDTH):
            shift_ref[s - 1] = jnp.where(col == row + head - s, 1.0, 0.0).astype(BF16)

    u = mqk_ref[...]
    ext_ref[head:, :] = u
    ext = ext_ref[...]
    y = cb_ref[...] + cw_ref[CONV_WIDTH - 1:CONV_WIDTH, :] * u.astype(F32)
    for s in range(1, CONV_WIDTH):
        tap = CONV_WIDTH - 1 - s
        y = y + cw_ref[tap:tap + 1, :] * jnp.dot(shift_ref[s - 1], ext, preferred_element_type=F32)
    ext_ref[head - tail:head, :] = u[length - tail:, :]
    y = y * _sigmoid(y)
    qk_ref[:, 0:n_qk] = y[:, 0:n_qk].astype(BF16)
    qk_ref[:, n_qk:] = (y[:, n_qk:] * dk ** -0.5).astype(BF16)

    keep = (lax.broadcasted_iota(jnp.int32, (length, length), 0)
            >= lax.broadcasted_iota(jnp.int32, (length, length), 1))
    ones = jnp.ones((length, LANES), BF16)

    def scores(h):
        qb = qk_ref[:, h * dk:(h + 1) * dk]
        kb = qk_ref[:, n_qk + h * dk:n_qk + (h + 1) * dk]
        gi, gf = FOX_HEADS + h, FOX_HEADS + MLSTM_HEADS + h
        u_row = grow_ref[gi:gi + 1, :] - grow_ref[gf:gf + 1, :]
        mg = mg_ref[h:h + 1, 0:1]
        um = jnp.where(keep, u_row, NEG_BIG)
        a = jnp.maximum(jnp.max(um, axis=-1, keepdims=True), mg)
        sqk = lax.dot_general(qb, kb, (((1,), (1,)), ((), ())), preferred_element_type=F32) * jnp.exp(um - a)
        mg_new = jnp.maximum(mg, jnp.max(u_row, axis=-1, keepdims=True))
        return sqk, a, mg, mg_new

    def outputs(h, sqk, a, mg, mg_new):
        qb = qk_ref[:, h * dk:(h + 1) * dk]
        kb = qk_ref[:, n_qk + h * dk:n_qk + (h + 1) * dk]
        sl = slice(h * dv, (h + 1) * dv)
        v_aug = jnp.concatenate([mv_ref[:, sl], ones], axis=1)
        gi, gf = FOX_HEADS + h, FOX_HEADS + MLSTM_HEADS + h
        f_col = gcol_ref[:, gf:gf + 1]
        u_col = gcol_ref[:, gi:gi + 1] - f_col
        ct = ct_ref[h]
        inter = jnp.dot(qb, ct.astype(BF16), preferred_element_type=F32)
        nd = jnp.dot(sqk.astype(BF16), v_aug, preferred_element_type=F32) + jnp.exp(mg - a) * inter
        den = jnp.maximum(jnp.abs(nd[:, dv:dv + 1]), jnp.exp(-(f_col + a)))
        hid = nd[:, :dv] * (1.0 / den)
        gate = _sigmoid(mo_ref[:, sl].astype(F32))
        out_ref[:, sl] = (_rms_rows(hid, on_ref[:, sl]) * gate).astype(out_ref.dtype)

        wav = (jnp.exp(u_col - mg_new) * v_aug.astype(F32)).astype(BF16)
        kt = kb.astype(F32).T.astype(BF16)
        ct_ref[h] = jnp.exp(mg - mg_new) * ct + jnp.dot(kt, wav, preferred_element_type=F32)
        mg_ref[h:h + 1, :] = jnp.broadcast_to(mg_new, (1, LANES))

    pending = scores(0)
    for h in range(1, MLSTM_HEADS):
        nxt = scores(h)
        outputs(h - 1, *pending)
        pending = nxt
    outputs(MLSTM_HEADS - 1, *pending)


def _mlstm(proj, gcol, grow, conv_w, conv_b, out_norm, *, batch, s_len):
    length = MLSTM_CHUNK
    nc = s_len // length
    width = MLSTM_HEADS * MLSTM_V_DIM
    qk_width = 2 * MLSTM_HEADS * MLSTM_QK_DIM
    return pl.pallas_call(
        _mlstm_kernel,
        grid=(batch, nc),
        in_specs=[
            pl.BlockSpec((length, qk_width), lambda b, c: (b * nc + c, 2)),
            pl.BlockSpec((length, width), lambda b, c: (b * nc + c, 3)),
            pl.BlockSpec((length, width), lambda b, c: (b * nc + c, 4)),
            pl.BlockSpec((length, LANES), lambda b, c: (b * nc + c, 0)),
            pl.BlockSpec((None, N_GATES, length), lambda b, c: (b, 0, c)),
            pl.BlockSpec((CONV_WIDTH, qk_width), lambda b, c: (0, 0)),
            pl.BlockSpec((1, qk_width), lambda b, c: (0, 0)),
            pl.BlockSpec((1, width), lambda b, c: (0, 0)),
        ],
        out_specs=pl.BlockSpec((length, width), lambda b, c: (b * nc + c, 0)),
        out_shape=jax.ShapeDtypeStruct((batch * s_len, width), BF16),
        scratch_shapes=[
            pltpu.VMEM((MLSTM_HEADS, MLSTM_QK_DIM, MLSTM_V_DIM + LANES), F32),
            pltpu.VMEM((SUBLANES, LANES), F32),
            pltpu.VMEM((LANES + length, qk_width), BF16),
            pltpu.VMEM((CONV_WIDTH - 1, length, LANES + length), BF16),
            pltpu.VMEM((length, qk_width), BF16),
        ],
        compiler_params=_cparams(("parallel", "arbitrary")),
        name="mlstm",
    )(proj, proj, proj, gcol, grow, conv_w, conv_b.reshape(1, qk_width), out_norm.reshape(1, width))


def _xattn_kernel(q_ref, k_ref, v_ref, o_ref):
    d = q_ref.shape[1] // XATTN_HEADS

    def scores(h):
        sl = slice(h * d, (h + 1) * d)
        return lax.dot_general(q_ref[:, sl], k_ref[:, sl], (((1,), (1,)), ((), ())), preferred_element_type=F32)

    def outputs(h, s):
        sl = slice(h * d, (h + 1) * d)
        p = jnp.exp2(s - jnp.max(s, axis=-1, keepdims=True))
        p = p * (1.0 / jnp.sum(p, axis=-1, keepdims=True))
        o_ref[:, sl] = jnp.dot(p.astype(BF16), v_ref[:, sl], preferred_element_type=F32).astype(o_ref.dtype)

    _tile_pipeline(XATTN_HEADS, scores, outputs)


def _xattn(q, kv, *, batch, s_len, n_mem, tq):
    d_model = q.shape[1]
    nq = s_len // tq
    return pl.pallas_call(
        _xattn_kernel,
        grid=(batch, nq),
        in_specs=[
            pl.BlockSpec((tq, d_model), lambda b, i: (b * nq + i, 0)),
            pl.BlockSpec((n_mem, d_model), lambda b, i: (b, 0)),
            pl.BlockSpec((n_mem, d_model), lambda b, i: (b, 1)),
        ],
        out_specs=pl.BlockSpec((tq, d_model), lambda b, i: (b * nq + i, 0)),
        out_shape=jax.ShapeDtypeStruct(q.shape, BF16),
        compiler_params=_cparams(("parallel", "parallel")),
        name="xattn",
    )(q, kv, kv)


def _mlp_kernel(x_ref, g_ref, wu_ref, wd_ref, o_ref, xn_ref):
    f = pl.program_id(1)

    @pl.when(f == 0)
    def _():
        x = x_ref[...]
        xn_ref[...] = _rms_rows(x, g_ref[...]).astype(BF16)
        o_ref[...] = x

    tc = MLP_CHUNK

    def up(j):
        return jnp.dot(xn_ref[...], wu_ref[:, j * tc:(j + 1) * tc], preferred_element_type=F32)

    def down(j, hid):
        hid = jnp.square(jnp.maximum(hid, 0.0)).astype(BF16)
        o_ref[...] += jnp.dot(hid, wd_ref[j * tc:(j + 1) * tc, :], preferred_element_type=F32)

    _tile_pipeline(wu_ref.shape[1] // tc, up, down)


def _mlp(x, gain, w_up, w_down, *, tm, tf):
    m, d = x.shape
    d_ff = w_up.shape[1]
    return pl.pallas_call(
        _mlp_kernel,
        grid=(m // tm, d_ff // tf),
        in_specs=[
            pl.BlockSpec((tm, d), lambda i, f: (i, 0)),
            pl.BlockSpec((1, d), lambda i, f: (0, 0)),
            pl.BlockSpec((d, tf), lambda i, f: (0, f)),
            pl.BlockSpec((tf, d), lambda i, f: (f, 0)),
        ],
        out_specs=pl.BlockSpec((tm, d), lambda i, f: (i, 0)),
        out_shape=jax.ShapeDtypeStruct((m, d), F32),
        scratch_shapes=[pltpu.VMEM((tm, d), BF16)],
        compiler_params=_cparams(("parallel", "arbitrary")),
        name="mlp",
    )(x, gain.reshape(1, d), w_up, w_down)


def _layer(x, mem, mixer_norm, w_in, fox_f_bias, mlstm_i_bias, mlstm_f_bias, conv_w, conv_b, fox_q_norm,
           fox_k_norm, fox_out_norm, mlstm_out_norm, w_out, xattn_norm, mem_norm, w_xq, w_xkv, xq_norm,
           xk_norm, w_xo, mlp_norm, w_up, w_down, *, batch, s_len, n_mem):
    fox_w = FOX_HEADS * FOX_HEAD_DIM
    mqk_w = 2 * MLSTM_HEADS * MLSTM_QK_DIM
    mv_w = MLSTM_HEADS * MLSTM_V_DIM
    d_model = x.shape[1]

    o_ff = 3 * fox_w
    o_mqk = o_ff + FOX_HEADS
    o_mv = o_mqk + mqk_w
    o_mi = o_mv + mv_w
    o_mo = o_mi + 2 * MLSTM_HEADS
    wb = w_in.astype(BF16)
    w_gates = jnp.concatenate([wb[:, o_ff:o_mqk], wb[:, o_mi:o_mo],
                               jnp.zeros((d_model, LANES - N_GATES), BF16)], axis=1)
    gate_bias = jnp.concatenate([fox_f_bias, mlstm_i_bias, mlstm_f_bias]).reshape(N_GATES, 1)

    proj, vt, gates_raw = _in_proj(x, mixer_norm, wb, wb[:, o_mqk:o_mi], wb[:, o_mo:],
                                   jnp.stack([fox_q_norm, fox_k_norm]), w_gates, tm=256)
    grow, gcol, ka = _gates(gates_raw, gate_bias, batch=batch, s_len=s_len)
    fo = _fox_attention(proj, vt, ka, grow, fox_out_norm, batch=batch, s_len=s_len)
    mh = _mlstm(proj, gcol, grow, conv_w, conv_b, mlstm_out_norm, batch=batch, s_len=s_len)
    x = _proj_residual([fo, mh], w_out, x, tm=512, tn=512)

    xd = d_model // XATTN_HEADS
    q = _norm_proj(x, xattn_norm, w_xq, xq_norm, tm=512,
                   post_scale=xd ** -0.5 * LOG2E)
    kv = _kv_proj(mem, mem_norm, w_xkv, xk_norm, n_norm_tiles=XATTN_HEADS)
    co = _xattn(q, kv, batch=batch, s_len=s_len, n_mem=n_mem, tq=1024)
    x = _proj_residual([co], w_xo, x, tm=512, tn=512)

    return _mlp(x, mlp_norm, w_up.astype(BF16), w_down.astype(BF16), tm=1024, tf=1024)


def kernel(x, mem, mixer_norm, w_in, fox_f_bias, mlstm_i_bias, mlstm_f_bias, conv_w, conv_b, fox_q_norm,
           fox_k_norm, fox_out_norm, mlstm_out_norm, w_out, xattn_norm, mem_norm, w_xq, w_xkv, xq_norm,
           xk_norm, w_xo, mlp_norm, w_up, w_down):
    batch, s_len, d_model = x.shape
    n_mem = mem.shape[1]
    depth = w_in.shape[0]
    h = x.reshape(batch * s_len, d_model)
    mem2 = mem.reshape(batch * n_mem, d_model)
    for l in range(depth):
        h = _layer(h, mem2, mixer_norm[l], w_in[l], fox_f_bias[l], mlstm_i_bias[l], mlstm_f_bias[l],
                   conv_w[l], conv_b[l], fox_q_norm[l], fox_k_norm[l], fox_out_norm[l], mlstm_out_norm[l],
                   w_out[l], xattn_norm[l], mem_norm[l], w_xq[l], w_xkv[l], xq_norm[l], xk_norm[l], w_xo[l],
                   mlp_norm[l], w_up[l], w_down[l], batch=batch, s_len=s_len, n_mem=n_mem)
    return h.reshape(batch, s_len, d_model)
```

```python
import functools

import jax
import jax.numpy as jnp
from jax import lax
from jax.experimental import pallas as pl
from jax.experimental.pallas import tpu as pltpu

F32 = jnp.float32
BF16 = jnp.bfloat16

EPS = 1e-6
FOX_HEADS = 8
FOX_HEAD_DIM = 128
MLSTM_HEADS = 4
MLSTM_QK_DIM = 128
MLSTM_V_DIM = 256
CONV_WIDTH = 4
XATTN_HEADS = 4
N_GATES = 16
LANES = 128
SUBLANES = 8
BF16_SUBLANES = 16
NEG_BIG = -1e30
LOG2E = 1.4426950408889634
FOX_AUG_DEPTH = 256
FOX_ONES_ROWS = 16

MLSTM_CHUNK = 256
FOX_BLOCK = 512
FOX_KEY_SUB = 256
IN_PROJ_TILE = 512
MLP_CHUNK = 256
VMEM_LIMIT = 56 * 1024 * 1024


def _cparams(sem):
    return pltpu.CompilerParams(dimension_semantics=sem, vmem_limit_bytes=VMEM_LIMIT)


def _rms_rows(x, gain):
    ms = jnp.mean(x * x, axis=-1, keepdims=True)
    return x * lax.rsqrt(ms + EPS) * gain


def _sigmoid(x):
    return 1.0 / (1.0 + jnp.exp(-x))


def _split3(x):
    hi = x.astype(BF16).astype(F32)
    r = x - hi
    mid = r.astype(BF16).astype(F32)
    lo = (r - mid).astype(BF16).astype(F32)
    return hi, mid, lo


def _resident(block_shape):
    return pl.BlockSpec(block_shape, lambda i: (0,) * len(block_shape), pipeline_mode=pl.Buffered(1))


def _tile_pipeline(n_tiles, matmul, epilogue):
    pending = matmul(0)
    for j in range(1, n_tiles):
        nxt = matmul(j)
        epilogue(j - 1, pending)
        pending = nxt
    epilogue(n_tiles - 1, pending)


def _w_in_prep_kernel(w_ref, main_ref, gates_ref, *, o_mqk, o_mi, o_mo):
    o_ff = 3 * FOX_HEADS * FOX_HEAD_DIM
    n_mlstm = o_mi - o_mqk
    main_ref[:, 0:o_ff] = w_ref[:, 0:o_ff].astype(BF16)
    main_ref[:, o_ff:o_ff + n_mlstm] = w_ref[:, o_mqk:o_mi].astype(BF16)
    main_ref[:, o_ff + n_mlstm:] = w_ref[:, o_mo:].astype(BF16)
    lane = lax.broadcasted_iota(jnp.int32, gates_ref.shape, 1)
    fox_tile = w_ref[:, o_ff:o_ff + LANES]
    mlstm_tile = w_ref[:, o_mi - FOX_HEADS:o_mi - FOX_HEADS + LANES]
    gates_ref[...] = jnp.where(lane < FOX_HEADS, fox_tile,
                               jnp.where(lane < N_GATES, mlstm_tile, 0.0)).astype(BF16)


def _w_in_prep(w_in, *, o_mqk, o_mi, o_mo, tk):
    k, n_all = w_in.shape
    n_main = n_all - N_GATES
    assert (o_mi - FOX_HEADS) % LANES == 0 and (3 * FOX_HEADS * FOX_HEAD_DIM) % LANES == 0
    return pl.pallas_call(
        functools.partial(_w_in_prep_kernel, o_mqk=o_mqk, o_mi=o_mi, o_mo=o_mo),
        grid=(k // tk,),
        in_specs=[pl.BlockSpec((tk, n_all), lambda i: (i, 0))],
        out_specs=[pl.BlockSpec((tk, n_main), lambda i: (i, 0)), pl.BlockSpec((tk, LANES), lambda i: (i, 0))],
        out_shape=[jax.ShapeDtypeStruct((k, n_main), BF16), jax.ShapeDtypeStruct((k, LANES), BF16)],
        compiler_params=_cparams(("parallel",)),
        name="w_in_prep",
    )(w_in)


def _in_proj_kernel(x_ref, g_ref, w_ref, qk_gain_ref, wg_ref, out_ref, vt_ref, gates_ref):
    seg_w = FOX_HEADS * FOX_HEAD_DIM
    tn = IN_PROJ_TILE
    xn = _rms_rows(x_ref[...], g_ref[...]).astype(BF16)
    sources = (2 * seg_w, 0, seg_w, 3 * seg_w, 4 * seg_w, 5 * seg_w)

    def matmul(j):
        seg, off = divmod(j * tn, seg_w)
        col0 = sources[seg] + off
        return jnp.dot(xn, w_ref[:, col0:col0 + tn], preferred_element_type=F32)

    def head_norm(acc, gain, post_scale, col0):
        for h in range(tn // FOX_HEAD_DIM):
            sl = slice(h * FOX_HEAD_DIM, (h + 1) * FOX_HEAD_DIM)
            y = _rms_rows(acc[:, sl], gain) * post_scale
            out_ref[:, col0 + h * FOX_HEAD_DIM:col0 + (h + 1) * FOX_HEAD_DIM] = y.astype(out_ref.dtype)

    def epilogue(j, acc):
        seg, off = divmod(j * tn, seg_w)
        if seg == 0:
            vt_ref[off:off + tn, :] = acc.T.astype(vt_ref.dtype)
        elif seg == 1:
            head_norm(acc, qk_gain_ref[0], FOX_HEAD_DIM ** -0.5 * LOG2E, off)
        elif seg == 2:
            head_norm(acc, qk_gain_ref[1], 1.0, seg_w + off)
        else:
            out_ref[:, (seg - 1) * seg_w + off:(seg - 1) * seg_w + off + tn] = acc.astype(out_ref.dtype)

    gates_ref[...] = jnp.dot(xn, wg_ref[...], preferred_element_type=F32)
    _tile_pipeline(len(sources) * seg_w // tn, matmul, epilogue)


def _in_proj(x, gain, w, qk_gains, w_gates, *, tm):
    m, k = x.shape
    tn = FOX_HEADS * FOX_HEAD_DIM
    n = w.shape[1]
    return pl.pallas_call(
        _in_proj_kernel,
        grid=(m // tm,),
        in_specs=[
            pl.BlockSpec((tm, k), lambda i: (i, 0)),
            _resident((1, k)),
            _resident((k, n)),
            _resident((2, 1, FOX_HEAD_DIM)),
            _resident((k, LANES)),
        ],
        out_specs=[
            pl.BlockSpec((tm, n - tn), lambda i: (i, 0)),
            pl.BlockSpec((tn, tm), lambda i: (0, i)),
            pl.BlockSpec((tm, LANES), lambda i: (i, 0)),
        ],
        out_shape=[
            jax.ShapeDtypeStruct((m, n - tn), BF16),
            jax.ShapeDtypeStruct((tn, m), BF16),
            jax.ShapeDtypeStruct((m, LANES), F32),
        ],
        compiler_params=_cparams(("parallel",)),
        name="in_proj",
    )(x, gain.reshape(1, k), w, qk_gains.reshape(2, 1, FOX_HEAD_DIM), w_gates)


def _bf16_copy(w_ref, wb_ref):
    @pl.when(pl.program_id(0) == 0)
    def _():
        wb_ref[...] = w_ref[...].astype(BF16)
    return wb_ref


def _norm_proj_kernel(x_ref, g_ref, w_ref, hg_ref, out_ref, wb_ref, *, post_scale):
    tn = hg_ref.shape[1]
    _bf16_copy(w_ref, wb_ref)
    xn = _rms_rows(x_ref[...], g_ref[...]).astype(BF16)

    def matmul(j):
        return jnp.dot(xn, wb_ref[:, j * tn:(j + 1) * tn], preferred_element_type=F32)

    def epilogue(j, acc):
        out_ref[:, j * tn:(j + 1) * tn] = (_rms_rows(acc, hg_ref[...]) * post_scale).astype(out_ref.dtype)

    _tile_pipeline(w_ref.shape[1] // tn, matmul, epilogue)


def _norm_proj(x, gain, w, head_gain, *, tm, post_scale):
    m, k = x.shape
    n = w.shape[1]
    tn = head_gain.shape[0]
    return pl.pallas_call(
        functools.partial(_norm_proj_kernel, post_scale=post_scale),
        grid=(m // tm,),
        in_specs=[
            pl.BlockSpec((tm, k), lambda i: (i, 0)),
            _resident((1, k)),
            _resident((k, n)),
            _resident((1, tn)),
        ],
        out_specs=pl.BlockSpec((tm, n), lambda i: (i, 0)),
        out_shape=jax.ShapeDtypeStruct((m, n), BF16),
        scratch_shapes=[pltpu.VMEM((k, n), BF16)],
        compiler_params=_cparams(("arbitrary",)),
        name="norm_proj",
    )(x, gain.reshape(1, k), w, head_gain.reshape(1, tn))


def _kv_proj_kernel(x_ref, g_ref, w_ref, hg_ref, out_ref, xn_ref, *, n_norm_tiles):
    j = pl.program_id(0)

    @pl.when(j == 0)
    def _():
        xn_ref[...] = _rms_rows(x_ref[...], g_ref[...]).astype(BF16)

    acc = jnp.dot(xn_ref[...], w_ref[...].astype(BF16), preferred_element_type=F32)

    @pl.when(j < n_norm_tiles)
    def _():
        out_ref[...] = _rms_rows(acc, hg_ref[...]).astype(out_ref.dtype)

    @pl.when(j >= n_norm_tiles)
    def _():
        out_ref[...] = acc.astype(out_ref.dtype)


def _kv_proj(x, gain, w, head_gain, *, n_norm_tiles):
    m, k = x.shape
    n = w.shape[1]
    tn = head_gain.shape[0]
    return pl.pallas_call(
        functools.partial(_kv_proj_kernel, n_norm_tiles=n_norm_tiles),
        grid=(n // tn,),
        in_specs=[
            _resident((m, k)),
            _resident((1, k)),
            pl.BlockSpec((k, tn), lambda j: (0, j)),
            _resident((1, tn)),
        ],
        out_specs=pl.BlockSpec((m, tn), lambda j: (0, j)),
        out_shape=jax.ShapeDtypeStruct((m, n), BF16),
        scratch_shapes=[pltpu.VMEM((m, k), BF16)],
        compiler_params=_cparams(("arbitrary",)),
        name="kv_proj",
    )(x, gain.reshape(1, k), w, head_gain.reshape(1, tn))


def _proj_residual_kernel(*refs, n_lhs, tn):
    lhs = refs[:n_lhs]
    w_ref, res_ref, out_ref, wb_ref = refs[n_lhs:]
    kp = lhs[0].shape[1]
    _bf16_copy(w_ref, wb_ref)

    def matmul(j):
        cols = slice(j * tn, (j + 1) * tn)
        acc = jnp.dot(lhs[0][...], wb_ref[0:kp, cols], preferred_element_type=F32)
        for p in range(1, n_lhs):
            acc = acc + jnp.dot(lhs[p][...], wb_ref[p * kp:(p + 1) * kp, cols], preferred_element_type=F32)
        return acc

    def epilogue(j, acc):
        cols = slice(j * tn, (j + 1) * tn)
        out_ref[:, cols] = res_ref[:, cols] + acc

    _tile_pipeline(w_ref.shape[1] // tn, matmul, epilogue)


def _proj_residual(lhs_list, w, residual, *, tm, tn):
    m, n = residual.shape
    n_lhs = len(lhs_list)
    kp = lhs_list[0].shape[1]
    in_specs = [pl.BlockSpec((tm, kp), lambda i: (i, 0)) for _ in lhs_list]
    in_specs += [_resident(w.shape), pl.BlockSpec((tm, n), lambda i: (i, 0))]
    return pl.pallas_call(
        functools.partial(_proj_residual_kernel, n_lhs=n_lhs, tn=tn),
        grid=(m // tm,),
        in_specs=in_specs,
        out_specs=pl.BlockSpec((tm, n), lambda i: (i, 0)),
        out_shape=jax.ShapeDtypeStruct((m, n), F32),
        scratch_shapes=[pltpu.VMEM(w.shape, BF16)],
        compiler_params=_cparams(("arbitrary",)),
        name="proj_residual",
    )(*lhs_list, w, residual)


def _gates_kernel(g_ref, bias_ref, row_ref, col_ref, ka_ref):
    s_len = g_ref.shape[0]
    z = g_ref[...].T[0:N_GATES, :] + bias_ref[...]
    row = lax.broadcasted_iota(jnp.int32, z.shape, 0)
    is_forget = (row < FOX_HEADS) | (row >= FOX_HEADS + MLSTM_HEADS)
    logsig = jnp.minimum(z, 0.0) - jnp.log1p(jnp.exp(-jnp.abs(z)))
    c = jnp.where(is_forget, logsig, 0.0)
    lane = lax.broadcasted_iota(jnp.int32, z.shape, 1)
    shift = 1
    while shift < s_len:
        c = c + jnp.where(lane >= shift, pltpu.roll(c, shift, axis=1), 0.0)
        shift *= 2
    vals = jnp.where(is_forget, c, z)
    row_ref[...] = vals
    padded = jnp.concatenate([vals, jnp.zeros((LANES - N_GATES, s_len), F32)], axis=0)
    col_ref[...] = padded.T
    hi, mid, lo = _split3(c[0:FOX_HEADS, :] * LOG2E)
    aug = jnp.concatenate([jnp.ones((FOX_HEADS, s_len), F32), -hi, -mid, -lo,
                           jnp.zeros((LANES - 4 * FOX_HEADS, s_len), F32)], axis=0)
    ka_ref[...] = aug.T.astype(ka_ref.dtype)


def _gates(gates_raw, bias, *, batch, s_len):
    return pl.pallas_call(
        _gates_kernel,
        grid=(batch,),
        in_specs=[pl.BlockSpec((s_len, LANES), lambda b: (b, 0)),
                  pl.BlockSpec((N_GATES, 1), lambda b: (0, 0))],
        out_specs=[pl.BlockSpec((None, N_GATES, s_len), lambda b: (b, 0, 0)),
                   pl.BlockSpec((s_len, LANES), lambda b: (b, 0)),
                   pl.BlockSpec((s_len, LANES), lambda b: (b, 0))],
        out_shape=[jax.ShapeDtypeStruct((batch, N_GATES, s_len), F32),
                   jax.ShapeDtypeStruct((batch * s_len, LANES), F32),
                   jax.ShapeDtypeStruct((batch * s_len, LANES), BF16)],
        compiler_params=_cparams(("parallel",)),
        name="gates",
    )(gates_raw, bias)


def _fox_kernel(qi_ref, ki_ref, q_ref, k_ref, vt_ref, ka_ref, grow_ref, on_ref, o_ref, qt_ref, m_ref, acc_ref):
    pair = pl.program_id(1)
    qi = qi_ref[pair]
    ki = ki_ref[pair]
    tq, tk = q_ref.shape[0], k_ref.shape[0]
    d = FOX_HEAD_DIM

    @pl.when(ki == 0)
    def _():
        m_ref[...] = jnp.full(m_ref.shape, NEG_BIG, F32)
        acc_ref[...] = jnp.zeros(acc_ref.shape, F32)
        sub = lax.broadcasted_iota(jnp.int32, (SUBLANES, tq), 0)
        for h in range(FOX_HEADS):
            hi, mid, lo = _split3(grow_ref[h:h + 1, :] * LOG2E)
            c_rows = jnp.where(sub == 0, hi, jnp.where(sub == 1, mid, jnp.where(sub == 2, lo, 0.0)))
            pick = jnp.where(sub == h, 1.0, 0.0)
            qt = q_ref[:, h * d:(h + 1) * d].astype(F32).T
            pad = jnp.zeros((FOX_AUG_DEPTH - d - 4 * SUBLANES, tq), F32)
            qt_ref[h] = jnp.concatenate([qt, c_rows, pick, pick, pick, pad], axis=0).astype(BF16)

    def step(diagonal):
        sub = FOX_KEY_SUB
        ones = jnp.ones((FOX_ONES_ROWS, sub), BF16)
        if diagonal:
            keep = (lax.broadcasted_iota(jnp.int32, (sub, tq), 0)
                    <= lax.broadcasted_iota(jnp.int32, (sub, tq), 1))

        def scores(h, ks):
            t0 = ks * sub if diagonal else 0
            rows = slice(ks * sub, (ks + 1) * sub)
            k_aug = jnp.concatenate([k_ref[rows, h * d:(h + 1) * d], ka_ref[rows, :]], axis=1)
            st = jnp.dot(k_aug, qt_ref[h, :, t0:], preferred_element_type=F32)
            if diagonal:
                st = jnp.where(keep[:, 0:tq - t0], st, NEG_BIG)
            m_prev = m_ref[h:h + 1, t0:]
            m_new = jnp.maximum(m_prev, jnp.max(st, axis=0, keepdims=True))
            m_ref[h:h + 1, t0:] = m_new
            return t0, st, m_new, jnp.exp2(m_prev - m_new)

        def accumulate(h, ks, t0, st, m_new, alpha):
            p = jnp.exp2(st - m_new)
            vt_aug = jnp.concatenate([vt_ref[h * d:(h + 1) * d, ks * sub:(ks + 1) * sub], ones], axis=0)
            acc_ref[h, :, t0:] = alpha * acc_ref[h, :, t0:] + jnp.dot(vt_aug, p.astype(BF16),
                                                                      preferred_element_type=F32)

        items = [(h, ks) for h in range(FOX_HEADS) for ks in range(tk // sub)]
        lookahead = 2
        pending = [scores(*it) for it in items[:lookahead]]
        for n, it in enumerate(items):
            if n + lookahead < len(items):
                pending.append(scores(*items[n + lookahead]))
            accumulate(*it, *pending.pop(0))

    @pl.when(ki < qi)
    def _():
        step(False)

    @pl.when(ki == qi)
    def _():
        step(True)
        for h in range(FOX_HEADS):
            sl = slice(h * d, (h + 1) * d)
            o = (acc_ref[h, 0:d, :] / acc_ref[h, d:d + 1, :]).T
            o_ref[:, sl] = _rms_rows(o, on_ref[:, sl]).astype(o_ref.dtype)


def _fox_attention(proj, vt, ka, grow, out_norm, *, batch, s_len):
    t = FOX_BLOCK
    nq = s_len // t
    width = FOX_HEADS * FOX_HEAD_DIM
    pairs = [(qi, ki) for qi in range(nq) for ki in range(qi + 1)]
    qi_tab = jnp.asarray([p[0] for p in pairs], jnp.int32)
    ki_tab = jnp.asarray([p[1] for p in pairs], jnp.int32)

    def q_row(b, p, qi_tab, ki_tab):
        return b * nq + qi_tab[p]

    def k_row(b, p, qi_tab, ki_tab):
        return b * nq + ki_tab[p]

    grid_spec = pltpu.PrefetchScalarGridSpec(
        num_scalar_prefetch=2,
        grid=(batch, len(pairs)),
        in_specs=[
            pl.BlockSpec((t, width), lambda *a: (q_row(*a), 0)),
            pl.BlockSpec((t, width), lambda *a: (k_row(*a), 1)),
            pl.BlockSpec((width, t), lambda *a: (0, k_row(*a))),
            pl.BlockSpec((t, LANES), lambda *a: (k_row(*a), 0)),
            pl.BlockSpec((None, N_GATES, t), lambda b, p, qi_tab, ki_tab: (b, 0, qi_tab[p])),
            pl.BlockSpec((1, width), lambda *a: (0, 0)),
        ],
        out_specs=pl.BlockSpec((t, width), lambda *a: (q_row(*a), 0)),
        scratch_shapes=[
            pltpu.VMEM((FOX_HEADS, FOX_AUG_DEPTH, t), BF16),
            pltpu.VMEM((FOX_HEADS, t), F32),
            pltpu.VMEM((FOX_HEADS, FOX_HEAD_DIM + FOX_ONES_ROWS, t), F32),
        ],
    )
    return pl.pallas_call(
        _fox_kernel,
        grid_spec=grid_spec,
        out_shape=jax.ShapeDtypeStruct((batch * s_len, width), BF16),
        compiler_params=_cparams(("parallel", "arbitrary")),
        name="fox_attention",
    )(qi_tab, ki_tab, proj, proj, vt, ka, grow, out_norm.reshape(1, width))


def _mlstm_kernel(mqk_ref, mv_ref, mo_ref, gcol_ref, grow_ref, cw_ref, cb_ref, on_ref, out_ref,
                  ct_ref, mg_ref, ext_ref, shift_ref, qk_ref):
    c = pl.program_id(1)
    length = mqk_ref.shape[0]
    dk, dv = MLSTM_QK_DIM, MLSTM_V_DIM
    n_qk = MLSTM_HEADS * dk
    head = ext_ref.shape[0] - length
    tail = BF16_SUBLANES

    @pl.when(c == 0)
    def _():
        ct_ref[...] = jnp.zeros(ct_ref.shape, F32)
        mg_ref[...] = jnp.zeros(mg_ref.shape, F32)
        ext_ref[0:head, :] = jnp.zeros((head, ext_ref.shape[1]), BF16)
        row = lax.broadcasted_iota(jnp.int32, shift_ref.shape[1:], 0)
        col = lax.broadcasted_iota(jnp.int32, shift_ref.shape[1:], 1)
        for s in range(1, CONV_WIDTH):
            shift_ref[s - 1] = jnp.where(col == row + head - s, 1.0, 0.0).astype(BF16)

    u = mqk_ref[...]
    ext_ref[head:, :] = u
    ext = ext_ref[...]
    y = cb_ref[...] + cw_ref[CONV_WIDTH - 1:CONV_WIDTH, :] * u.astype(F32)
    for s in range(1, CONV_WIDTH):
        tap = CONV_WIDTH - 1 - s
        y = y + cw_ref[tap:tap + 1, :] * jnp.dot(shift_ref[s - 1], ext, preferred_element_type=F32)
    ext_ref[head - tail:head, :] = u[length - tail:, :]
    y = y * _sigmoid(y)
    qk_ref[:, 0:n_qk] = y[:, 0:n_qk].astype(BF16)
    qk_ref[:, n_qk:] = (y[:, n_qk:] * dk ** -0.5).astype(BF16)

    keep = (lax.broadcasted_iota(jnp.int32, (length, length), 0)
            >= lax.broadcasted_iota(jnp.int32, (length, length), 1))
    ones = jnp.ones((length, LANES), BF16)

    def scores(h):
        qb = qk_ref[:, h * dk:(h + 1) * dk]
        kb = qk_ref[:, n_qk + h * dk:n_qk + (h + 1) * dk]
        gi, gf = FOX_HEADS + h, FOX_HEADS + MLSTM_HEADS + h
        u_row = grow_ref[gi:gi + 1, :] - grow_ref[gf:gf + 1, :]
        mg = mg_ref[h:h + 1, 0:1]
        um = jnp.where(keep, u_row, NEG_BIG)
        a = jnp.maximum(jnp.max(um, axis=-1, keepdims=True), mg)
        sqk = lax.dot_general(qb, kb, (((1,), (1,)), ((), ())), preferred_element_type=F32) * jnp.exp(um - a)
        mg_new = jnp.maximum(mg, jnp.max(u_row, axis=-1, keepdims=True))
        return sqk, a, mg, mg_new

    def outputs(h, sqk, a, mg, mg_new):
        qb = qk_ref[:, h * dk:(h + 1) * dk]
        kb = qk_ref[:, n_qk + h * dk:n_qk + (h + 1) * dk]
        sl = slice(h * dv, (h + 1) * dv)
        v_aug = jnp.concatenate([mv_ref[:, sl], ones], axis=1)
        gi, gf = FOX_HEADS + h, FOX_HEADS + MLSTM_HEADS + h
        f_col = gcol_ref[:, gf:gf + 1]
        u_col = gcol_ref[:, gi:gi + 1] - f_col
        ct = ct_ref[h]
        inter = jnp.dot(qb, ct.astype(BF16), preferred_element_type=F32)
        nd = jnp.dot(sqk.astype(BF16), v_aug, preferred_element_type=F32) + jnp.exp(mg - a) * inter
        den = jnp.maximum(jnp.abs(nd[:, dv:dv + 1]), jnp.exp(-(f_col + a)))
        hid = nd[:, :dv] * (1.0 / den)
        gate = _sigmoid(mo_ref[:, sl].astype(F32))
        out_ref[:, sl] = (_rms_rows(hid, on_ref[:, sl]) * gate).astype(out_ref.dtype)

        wav = (jnp.exp(u_col - mg_new) * v_aug.astype(F32)).astype(BF16)
        kt = kb.astype(F32).T.astype(BF16)
        ct_ref[h] = jnp.exp(mg - mg_new) * ct + jnp.dot(kt, wav, preferred_element_type=F32)
        mg_ref[h:h + 1, :] = jnp.broadcast_to(mg_new, (1, LANES))

    pending = scores(0)
    for h in range(1, MLSTM_HEADS):
        nxt = scores(h)
        outputs(h - 1, *pending)
        pending = nxt
    outputs(MLSTM_HEADS - 1, *pending)


def _mlstm(proj, gcol, grow, conv_w, conv_b, out_norm, *, batch, s_len):
    length = MLSTM_CHUNK
    nc = s_len // length
    width = MLSTM_HEADS * MLSTM_V_DIM
    qk_width = 2 * MLSTM_HEADS * MLSTM_QK_DIM
    return pl.pallas_call(
        _mlstm_kernel,
        grid=(batch, nc),
        in_specs=[
            pl.BlockSpec((length, qk_width), lambda b, c: (b * nc + c, 2)),
            pl.BlockSpec((length, width), lambda b, c: (b * nc + c, 3)),
            pl.BlockSpec((length, width), lambda b, c: (b * nc + c, 4)),
            pl.BlockSpec((length, LANES), lambda b, c: (b * nc + c, 0)),
            pl.BlockSpec((None, N_GATES, length), lambda b, c: (b, 0, c)),
            pl.BlockSpec((CONV_WIDTH, qk_width), lambda b, c: (0, 0)),
            pl.BlockSpec((1, qk_width), lambda b, c: (0, 0)),
            pl.BlockSpec((1, width), lambda b, c: (0, 0)),
        ],
        out_specs=pl.BlockSpec((length, width), lambda b, c: (b * nc + c, 0)),
        out_shape=jax.ShapeDtypeStruct((batch * s_len, width), BF16),
        scratch_shapes=[
            pltpu.VMEM((MLSTM_HEADS, MLSTM_QK_DIM, MLSTM_V_DIM + LANES), F32),
            pltpu.VMEM((SUBLANES, LANES), F32),
            pltpu.VMEM((LANES + length, qk_width), BF16),
            pltpu.VMEM((CONV_WIDTH - 1, length, LANES + length), BF16),
            pltpu.VMEM((length, qk_width), BF16),
        ],
        compiler_params=_cparams(("parallel", "arbitrary")),
        name="mlstm",
    )(proj, proj, proj, gcol, grow, conv_w, conv_b.reshape(1, qk_width), out_norm.reshape(1, width))


def _xattn_kernel(q_ref, k_ref, v_ref, o_ref):
    d = q_ref.shape[1] // XATTN_HEADS

    def scores(h):
        sl = slice(h * d, (h + 1) * d)
        return lax.dot_general(q_ref[:, sl], k_ref[:, sl], (((1,), (1,)), ((), ())), preferred_element_type=F32)

    def outputs(h, s):
        sl = slice(h * d, (h + 1) * d)
        p = jnp.exp2(s - jnp.max(s, axis=-1, keepdims=True))
        p = p * (1.0 / jnp.sum(p, axis=-1, keepdims=True))
        o_ref[:, sl] = jnp.dot(p.astype(BF16), v_ref[:, sl], preferred_element_type=F32).astype(o_ref.dtype)

    _tile_pipeline(XATTN_HEADS, scores, outputs)


def _xattn(q, kv, *, batch, s_len, n_mem, tq):
    d_model = q.shape[1]
    nq = s_len // tq
    return pl.pallas_call(
        _xattn_kernel,
        grid=(batch, nq),
        in_specs=[
            pl.BlockSpec((tq, d_model), lambda b, i: (b * nq + i, 0)),
            pl.BlockSpec((n_mem, d_model), lambda b, i: (b, 0)),
            pl.BlockSpec((n_mem, d_model), lambda b, i: (b, 1)),
        ],
        out_specs=pl.BlockSpec((tq, d_model), lambda b, i: (b * nq + i, 0)),
        out_shape=jax.ShapeDtypeStruct(q.shape, BF16),
        compiler_params=_cparams(("parallel", "parallel")),
        name="xattn",
    )(q, kv, kv)


def _mlp_kernel(x_ref, g_ref, wu_ref, wd_ref, o_ref, xn_ref):
    f = pl.program_id(1)

    @pl.when(f == 0)
    def _():
        x = x_ref[...]
        xn_ref[...] = _rms_rows(x, g_ref[...]).astype(BF16)
        o_ref[...] = x

    tc = MLP_CHUNK

    def up(j):
        return jnp.dot(xn_ref[...], wu_ref[:, j * tc:(j + 1) * tc], preferred_element_type=F32)

    def down(j, hid):
        hid = jnp.square(jnp.maximum(hid, 0.0)).astype(BF16)
        o_ref[...] += jnp.dot(hid, wd_ref[j * tc:(j + 1) * tc, :], preferred_element_type=F32)

    _tile_pipeline(wu_ref.shape[1] // tc, up, down)


def _mlp(x, gain, w_up, w_down, *, tm, tf):
    m, d = x.shape
    d_ff = w_up.shape[1]
    return pl.pallas_call(
        _mlp_kernel,
        grid=(m // tm, d_ff // tf),
        in_specs=[
            pl.BlockSpec((tm, d), lambda i, f: (i, 0)),
            pl.BlockSpec((1, d), lambda i, f: (0, 0)),
            pl.BlockSpec((d, tf), lambda i, f: (0, f)),
            pl.BlockSpec((tf, d), lambda i, f: (f, 0)),
        ],
        out_specs=pl.BlockSpec((tm, d), lambda i, f: (i, 0)),
        out_shape=jax.ShapeDtypeStruct((m, d), F32),
        scratch_shapes=[pltpu.VMEM((tm, d), BF16)],
        compiler_params=_cparams(("parallel", "arbitrary")),
        name="mlp",
    )(x, gain.reshape(1, d), w_up, w_down)


def _layer(x, mem, mixer_norm, w_in, fox_f_bias, mlstm_i_bias, mlstm_f_bias, conv_w, conv_b, fox_q_norm,
           fox_k_norm, fox_out_norm, mlstm_out_norm, w_out, xattn_norm, mem_norm, w_xq, w_xkv, xq_norm,
           xk_norm, w_xo, mlp_norm, w_up, w_down, *, batch, s_len, n_mem):
    fox_w = FOX_HEADS * FOX_HEAD_DIM
    mqk_w = 2 * MLSTM_HEADS * MLSTM_QK_DIM
    mv_w = MLSTM_HEADS * MLSTM_V_DIM
    d_model = x.shape[1]

    o_ff = 3 * fox_w
    o_mqk = o_ff + FOX_HEADS
    o_mv = o_mqk + mqk_w
    o_mi = o_mv + mv_w
    o_mo = o_mi + 2 * MLSTM_HEADS
    w_main, w_gates = _w_in_prep(w_in, o_mqk=o_mqk, o_mi=o_mi, o_mo=o_mo, tk=256)
    gate_bias = jnp.concatenate([fox_f_bias, mlstm_i_bias, mlstm_f_bias]).reshape(N_GATES, 1)

    proj, vt, gates_raw = _in_proj(x, mixer_norm, w_main, jnp.stack([fox_q_norm, fox_k_norm]), w_gates, tm=256)
    grow, gcol, ka = _gates(gates_raw, gate_bias, batch=batch, s_len=s_len)
    fo = _fox_attention(proj, vt, ka, grow, fox_out_norm, batch=batch, s_len=s_len)
    mh = _mlstm(proj, gcol, grow, conv_w, conv_b, mlstm_out_norm, batch=batch, s_len=s_len)
    x = _proj_residual([fo, mh], w_out, x, tm=512, tn=512)

    xd = d_model // XATTN_HEADS
    q = _norm_proj(x, xattn_norm, w_xq, xq_norm, tm=512,
                   post_scale=xd ** -0.5 * LOG2E)
    kv = _kv_proj(mem, mem_norm, w_xkv, xk_norm, n_norm_tiles=XATTN_HEADS)
    co = _xattn(q, kv, batch=batch, s_len=s_len, n_mem=n_mem, tq=1024)
    x = _proj_residual([co], w_xo, x, tm=512, tn=512)

    return _mlp(x, mlp_norm, w_up.astype(BF16), w_down.astype(BF16), tm=1024, tf=1024)


def kernel(x, mem, mixer_norm, w_in, fox_f_bias, mlstm_i_bias, mlstm_f_bias, conv_w, conv_b, fox_q_norm,
           fox_k_norm, fox_out_norm, mlstm_out_norm, w_out, xattn_norm, mem_norm, w_xq, w_xkv, xq_norm,
           xk_norm, w_xo, mlp_norm, w_up, w_down):
    batch, s_len, d_model = x.shape
    n_mem = mem.shape[1]
    depth = w_in.shape[0]
    h = x.reshape(batch * s_len, d_model)
    mem2 = mem.reshape(batch * n_mem, d_model)
    for l in range(depth):
        h = _layer(h, mem2, mixer_norm[l], w_in[l], fox_f_bias[l], mlstm_i_bias[l], mlstm_f_bias[l],
                   conv_w[l], conv_b[l], fox_q_norm[l], fox_k_norm[l], fox_out_norm[l], mlstm_out_norm[l],
                   w_out[l], xattn_norm[l], mem_norm[l], w_xq[l], w_xkv[l], xq_norm[l], xk_norm[l], w_xo[l],
                   mlp_norm[l], w_up[l], w_down[l], batch=batch, s_len=s_len, n_mem=n_mem)
    return h.reshape(batch, s_len, d_model)
```

```python
import functools

import jax
import jax.numpy as jnp
from jax import lax
from jax.experimental import pallas as pl
from jax.experimental.pallas import tpu as pltpu

F32 = jnp.float32
BF16 = jnp.bfloat16

EPS = 1e-6
FOX_HEADS = 8
FOX_HEAD_DIM = 128
MLSTM_HEADS = 4
MLSTM_QK_DIM = 128
MLSTM_V_DIM = 256
CONV_WIDTH = 4
XATTN_HEADS = 4
N_GATES = 16
LANES = 128
SUBLANES = 8
BF16_SUBLANES = 16
NEG_BIG = -1e30
LOG2E = 1.4426950408889634
FOX_AUG_DEPTH = 256
FOX_ONES_ROWS = 16

MLSTM_CHUNK = 256
FOX_BLOCK = 512
FOX_KEY_SUB = 256
IN_PROJ_TILE = 512
MLP_CHUNK = 256
VMEM_LIMIT = 56 * 1024 * 1024


def _cparams(sem):
    return pltpu.CompilerParams(dimension_semantics=sem, vmem_limit_bytes=VMEM_LIMIT)


def _rms_rows(x, gain):
    ms = jnp.mean(x * x, axis=-1, keepdims=True)
    return x * lax.rsqrt(ms + EPS) * gain


def _sigmoid(x):
    return 1.0 / (1.0 + jnp.exp(-x))


def _split3(x):
    hi = x.astype(BF16).astype(F32)
    r = x - hi
    mid = r.astype(BF16).astype(F32)
    lo = (r - mid).astype(BF16).astype(F32)
    return hi, mid, lo


def _resident(block_shape):
    return pl.BlockSpec(block_shape, lambda i: (0,) * len(block_shape), pipeline_mode=pl.Buffered(1))


def _tile_pipeline(n_tiles, matmul, epilogue):
    pending = matmul(0)
    for j in range(1, n_tiles):
        nxt = matmul(j)
        epilogue(j - 1, pending)
        pending = nxt
    epilogue(n_tiles - 1, pending)


def _w_in_prep_kernel(wt_ref, gf_ref, gm_ref, main_ref, gates_ref):
    main_ref[...] = wt_ref[...].T.astype(BF16)

    @pl.when(pl.program_id(0) == 0)
    def _():
        pad = jnp.zeros((LANES - N_GATES, gf_ref.shape[1]), F32)
        gates_ref[...] = jnp.concatenate([gf_ref[...], gm_ref[...], pad], axis=0).T.astype(BF16)


def _w_in_prep(wt, *, o_ff, o_mi, tb):
    n_all, k = wt.shape
    n_main = n_all - N_GATES
    n_fox, n_before_mi = o_ff // tb, (o_mi - FOX_HEADS) // tb
    assert o_ff % tb == 0 and (o_mi - FOX_HEADS) % tb == 0 and n_main % tb == 0

    def src_row(b):
        return b * tb + jnp.where(b >= n_fox, FOX_HEADS, 0) + jnp.where(b >= n_before_mi, 2 * MLSTM_HEADS, 0)

    return pl.pallas_call(
        _w_in_prep_kernel,
        grid=(n_main // tb,),
        in_specs=[
            pl.BlockSpec((pl.Element(tb), pl.Element(k)), lambda b: (pl.multiple_of(src_row(b), SUBLANES), 0)),
            pl.BlockSpec((pl.Element(FOX_HEADS), pl.Element(k)), lambda b: (o_ff, 0)),
            pl.BlockSpec((pl.Element(2 * MLSTM_HEADS), pl.Element(k)), lambda b: (o_mi, 0)),
        ],
        out_specs=[pl.BlockSpec((k, tb), lambda b: (0, b)), pl.BlockSpec((k, LANES), lambda b: (0, 0))],
        out_shape=[jax.ShapeDtypeStruct((k, n_main), BF16), jax.ShapeDtypeStruct((k, LANES), BF16)],
        compiler_params=_cparams(("arbitrary",)),
        name="w_in_prep",
    )(wt, wt, wt)


def _in_proj_kernel(x_ref, g_ref, w_ref, qk_gain_ref, wg_ref, out_ref, vt_ref, gates_ref):
    seg_w = FOX_HEADS * FOX_HEAD_DIM
    tn = IN_PROJ_TILE
    xn = _rms_rows(x_ref[...], g_ref[...]).astype(BF16)
    sources = (2 * seg_w, 0, seg_w, 3 * seg_w, 4 * seg_w, 5 * seg_w)

    def matmul(j):
        seg, off = divmod(j * tn, seg_w)
        col0 = sources[seg] + off
        return jnp.dot(xn, w_ref[:, col0:col0 + tn], preferred_element_type=F32)

    def head_norm(acc, gain, post_scale, col0):
        for h in range(tn // FOX_HEAD_DIM):
            sl = slice(h * FOX_HEAD_DIM, (h + 1) * FOX_HEAD_DIM)
            y = _rms_rows(acc[:, sl], gain) * post_scale
            out_ref[:, col0 + h * FOX_HEAD_DIM:col0 + (h + 1) * FOX_HEAD_DIM] = y.astype(out_ref.dtype)

    def epilogue(j, acc):
        seg, off = divmod(j * tn, seg_w)
        if seg == 0:
            vt_ref[off:off + tn, :] = acc.T.astype(vt_ref.dtype)
        elif seg == 1:
            head_norm(acc, qk_gain_ref[0], FOX_HEAD_DIM ** -0.5 * LOG2E, off)
        elif seg == 2:
            head_norm(acc, qk_gain_ref[1], 1.0, seg_w + off)
        else:
            out_ref[:, (seg - 1) * seg_w + off:(seg - 1) * seg_w + off + tn] = acc.astype(out_ref.dtype)

    gates_ref[...] = jnp.dot(xn, wg_ref[...], preferred_element_type=F32)
    _tile_pipeline(len(sources) * seg_w // tn, matmul, epilogue)


def _in_proj(x, gain, w, qk_gains, w_gates, *, tm):
    m, k = x.shape
    tn = FOX_HEADS * FOX_HEAD_DIM
    n = w.shape[1]
    return pl.pallas_call(
        _in_proj_kernel,
        grid=(m // tm,),
        in_specs=[
            pl.BlockSpec((tm, k), lambda i: (i, 0)),
            _resident((1, k)),
            _resident((k, n)),
            _resident((2, 1, FOX_HEAD_DIM)),
            _resident((k, LANES)),
        ],
        out_specs=[
            pl.BlockSpec((tm, n - tn), lambda i: (i, 0)),
            pl.BlockSpec((tn, tm), lambda i: (0, i)),
            pl.BlockSpec((tm, LANES), lambda i: (i, 0)),
        ],
        out_shape=[
            jax.ShapeDtypeStruct((m, n - tn), BF16),
            jax.ShapeDtypeStruct((tn, m), BF16),
            jax.ShapeDtypeStruct((m, LANES), F32),
        ],
        compiler_params=_cparams(("parallel",)),
        name="in_proj",
    )(x, gain.reshape(1, k), w, qk_gains.reshape(2, 1, FOX_HEAD_DIM), w_gates)


def _bf16_copy(w_ref, wb_ref):
    @pl.when(pl.program_id(0) == 0)
    def _():
        wb_ref[...] = w_ref[...].astype(BF16)
    return wb_ref


def _norm_proj_kernel(x_ref, g_ref, w_ref, hg_ref, out_ref, wb_ref, *, post_scale):
    tn = hg_ref.shape[1]
    _bf16_copy(w_ref, wb_ref)
    xn = _rms_rows(x_ref[...], g_ref[...]).astype(BF16)

    def matmul(j):
        return jnp.dot(xn, wb_ref[:, j * tn:(j + 1) * tn], preferred_element_type=F32)

    def epilogue(j, acc):
        out_ref[:, j * tn:(j + 1) * tn] = (_rms_rows(acc, hg_ref[...]) * post_scale).astype(out_ref.dtype)

    _tile_pipeline(w_ref.shape[1] // tn, matmul, epilogue)


def _norm_proj(x, gain, w, head_gain, *, tm, post_scale):
    m, k = x.shape
    n = w.shape[1]
    tn = head_gain.shape[0]
    return pl.pallas_call(
        functools.partial(_norm_proj_kernel, post_scale=post_scale),
        grid=(m // tm,),
        in_specs=[
            pl.BlockSpec((tm, k), lambda i: (i, 0)),
            _resident((1, k)),
            _resident((k, n)),
            _resident((1, tn)),
        ],
        out_specs=pl.BlockSpec((tm, n), lambda i: (i, 0)),
        out_shape=jax.ShapeDtypeStruct((m, n), BF16),
        scratch_shapes=[pltpu.VMEM((k, n), BF16)],
        compiler_params=_cparams(("arbitrary",)),
        name="norm_proj",
    )(x, gain.reshape(1, k), w, head_gain.reshape(1, tn))


def _kv_proj_kernel(x_ref, g_ref, w_ref, hg_ref, out_ref, xn_ref, *, n_norm_tiles):
    j = pl.program_id(0)

    @pl.when(j == 0)
    def _():
        xn_ref[...] = _rms_rows(x_ref[...], g_ref[...]).astype(BF16)

    acc = jnp.dot(xn_ref[...], w_ref[...].astype(BF16), preferred_element_type=F32)

    @pl.when(j < n_norm_tiles)
    def _():
        out_ref[...] = _rms_rows(acc, hg_ref[...]).astype(out_ref.dtype)

    @pl.when(j >= n_norm_tiles)
    def _():
        out_ref[...] = acc.astype(out_ref.dtype)


def _kv_proj(x, gain, w, head_gain, *, n_norm_tiles):
    m, k = x.shape
    n = w.shape[1]
    tn = head_gain.shape[0]
    return pl.pallas_call(
        functools.partial(_kv_proj_kernel, n_norm_tiles=n_norm_tiles),
        grid=(n // tn,),
        in_specs=[
            _resident((m, k)),
            _resident((1, k)),
            pl.BlockSpec((k, tn), lambda j: (0, j)),
            _resident((1, tn)),
        ],
        out_specs=pl.BlockSpec((m, tn), lambda j: (0, j)),
        out_shape=jax.ShapeDtypeStruct((m, n), BF16),
        scratch_shapes=[pltpu.VMEM((m, k), BF16)],
        compiler_params=_cparams(("arbitrary",)),
        name="kv_proj",
    )(x, gain.reshape(1, k), w, head_gain.reshape(1, tn))


def _proj_residual_kernel(*refs, n_lhs, tn):
    lhs = refs[:n_lhs]
    w_ref, res_ref, out_ref, wb_ref = refs[n_lhs:]
    kp = lhs[0].shape[1]
    _bf16_copy(w_ref, wb_ref)

    def matmul(j):
        cols = slice(j * tn, (j + 1) * tn)
        acc = jnp.dot(lhs[0][...], wb_ref[0:kp, cols], preferred_element_type=F32)
        for p in range(1, n_lhs):
            acc = acc + jnp.dot(lhs[p][...], wb_ref[p * kp:(p + 1) * kp, cols], preferred_element_type=F32)
        return acc

    def epilogue(j, acc):
        cols = slice(j * tn, (j + 1) * tn)
        out_ref[:, cols] = res_ref[:, cols] + acc

    _tile_pipeline(w_ref.shape[1] // tn, matmul, epilogue)


def _proj_residual(lhs_list, w, residual, *, tm, tn):
    m, n = residual.shape
    n_lhs = len(lhs_list)
    kp = lhs_list[0].shape[1]
    in_specs = [pl.BlockSpec((tm, kp), lambda i: (i, 0)) for _ in lhs_list]
    in_specs += [_resident(w.shape), pl.BlockSpec((tm, n), lambda i: (i, 0))]
    return pl.pallas_call(
        functools.partial(_proj_residual_kernel, n_lhs=n_lhs, tn=tn),
        grid=(m // tm,),
        in_specs=in_specs,
        out_specs=pl.BlockSpec((tm, n), lambda i: (i, 0)),
        out_shape=jax.ShapeDtypeStruct((m, n), F32),
        scratch_shapes=[pltpu.VMEM(w.shape, BF16)],
        compiler_params=_cparams(("arbitrary",)),
        name="proj_residual",
    )(*lhs_list, w, residual)


def _gates_kernel(g_ref, bias_ref, row_ref, col_ref, ka_ref):
    s_len = g_ref.shape[0]
    z = g_ref[...].T[0:N_GATES, :] + bias_ref[...]
    row = lax.broadcasted_iota(jnp.int32, z.shape, 0)
    is_forget = (row < FOX_HEADS) | (row >= FOX_HEADS + MLSTM_HEADS)
    logsig = jnp.minimum(z, 0.0) - jnp.log1p(jnp.exp(-jnp.abs(z)))
    c = jnp.where(is_forget, logsig, 0.0)
    lane = lax.broadcasted_iota(jnp.int32, z.shape, 1)
    shift = 1
    while shift < s_len:
        c = c + jnp.where(lane >= shift, pltpu.roll(c, shift, axis=1), 0.0)
        shift *= 2
    vals = jnp.where(is_forget, c, z)
    row_ref[...] = vals
    padded = jnp.concatenate([vals, jnp.zeros((LANES - N_GATES, s_len), F32)], axis=0)
    col_ref[...] = padded.T
    hi, mid, lo = _split3(c[0:FOX_HEADS, :] * LOG2E)
    aug = jnp.concatenate([jnp.ones((FOX_HEADS, s_len), F32), -hi, -mid, -lo,
                           jnp.zeros((LANES - 4 * FOX_HEADS, s_len), F32)], axis=0)
    ka_ref[...] = aug.T.astype(ka_ref.dtype)


def _gates(gates_raw, bias, *, batch, s_len):
    return pl.pallas_call(
        _gates_kernel,
        grid=(batch,),
        in_specs=[pl.BlockSpec((s_len, LANES), lambda b: (b, 0)),
                  pl.BlockSpec((N_GATES, 1), lambda b: (0, 0))],
        out_specs=[pl.BlockSpec((None, N_GATES, s_len), lambda b: (b, 0, 0)),
                   pl.BlockSpec((s_len, LANES), lambda b: (b, 0)),
                   pl.BlockSpec((s_len, LANES), lambda b: (b, 0))],
        out_shape=[jax.ShapeDtypeStruct((batch, N_GATES, s_len), F32),
                   jax.ShapeDtypeStruct((batch * s_len, LANES), F32),
                   jax.ShapeDtypeStruct((batch * s_len, LANES), BF16)],
        compiler_params=_cparams(("parallel",)),
        name="gates",
    )(gates_raw, bias)


def _fox_kernel(qi_ref, ki_ref, q_ref, k_ref, vt_ref, ka_ref, grow_ref, on_ref, o_ref, qt_ref, m_ref, acc_ref):
    pair = pl.program_id(1)
    qi = qi_ref[pair]
    ki = ki_ref[pair]
    tq, tk = q_ref.shape[0], k_ref.shape[0]
    d = FOX_HEAD_DIM

    @pl.when(ki == 0)
    def _():
        m_ref[...] = jnp.full(m_ref.shape, NEG_BIG, F32)
        acc_ref[...] = jnp.zeros(acc_ref.shape, F32)
        sub = lax.broadcasted_iota(jnp.int32, (SUBLANES, tq), 0)
        for h in range(FOX_HEADS):
            hi, mid, lo = _split3(grow_ref[h:h + 1, :] * LOG2E)
            c_rows = jnp.where(sub == 0, hi, jnp.where(sub == 1, mid, jnp.where(sub == 2, lo, 0.0)))
            pick = jnp.where(sub == h, 1.0, 0.0)
            qt = q_ref[:, h * d:(h + 1) * d].astype(F32).T
            pad = jnp.zeros((FOX_AUG_DEPTH - d - 4 * SUBLANES, tq), F32)
            qt_ref[h] = jnp.concatenate([qt, c_rows, pick, pick, pick, pad], axis=0).astype(BF16)

    def step(diagonal):
        sub = FOX_KEY_SUB
        ones = jnp.ones((FOX_ONES_ROWS, sub), BF16)
        if diagonal:
            keep = (lax.broadcasted_iota(jnp.int32, (sub, tq), 0)
                    <= lax.broadcasted_iota(jnp.int32, (sub, tq), 1))

        def scores(h, ks):
            t0 = ks * sub if diagonal else 0
            rows = slice(ks * sub, (ks + 1) * sub)
            k_aug = jnp.concatenate([k_ref[rows, h * d:(h + 1) * d], ka_ref[rows, :]], axis=1)
            st = jnp.dot(k_aug, qt_ref[h, :, t0:], preferred_element_type=F32)
            if diagonal:
                st = jnp.where(keep[:, 0:tq - t0], st, NEG_BIG)
            m_prev = m_ref[h:h + 1, t0:]
            m_new = jnp.maximum(m_prev, jnp.max(st, axis=0, keepdims=True))
            m_ref[h:h + 1, t0:] = m_new
            return t0, st, m_new, jnp.exp2(m_prev - m_new)

        def accumulate(h, ks, t0, st, m_new, alpha):
            p = jnp.exp2(st - m_new)
            vt_aug = jnp.concatenate([vt_ref[h * d:(h + 1) * d, ks * sub:(ks + 1) * sub], ones], axis=0)
            acc_ref[h, :, t0:] = alpha * acc_ref[h, :, t0:] + jnp.dot(vt_aug, p.astype(BF16),
                                                                      preferred_element_type=F32)

        items = [(h, ks) for h in range(FOX_HEADS) for ks in range(tk // sub)]
        lookahead = 2
        pending = [scores(*it) for it in items[:lookahead]]
        for n, it in enumerate(items):
            if n + lookahead < len(items):
                pending.append(scores(*items[n + lookahead]))
            accumulate(*it, *pending.pop(0))

    @pl.when(ki < qi)
    def _():
        step(False)

    @pl.when(ki == qi)
    def _():
        step(True)
        for h in range(FOX_HEADS):
            sl = slice(h * d, (h + 1) * d)
            o = (acc_ref[h, 0:d, :] / acc_ref[h, d:d + 1, :]).T
            o_ref[:, sl] = _rms_rows(o, on_ref[:, sl]).astype(o_ref.dtype)


def _fox_attention(proj, vt, ka, grow, out_norm, *, batch, s_len):
    t = FOX_BLOCK
    nq = s_len // t
    width = FOX_HEADS * FOX_HEAD_DIM
    pairs = [(qi, ki) for qi in range(nq) for ki in range(qi + 1)]
    qi_tab = jnp.asarray([p[0] for p in pairs], jnp.int32)
    ki_tab = jnp.asarray([p[1] for p in pairs], jnp.int32)

    def q_row(b, p, qi_tab, ki_tab):
        return b * nq + qi_tab[p]

    def k_row(b, p, qi_tab, ki_tab):
        return b * nq + ki_tab[p]

    grid_spec = pltpu.PrefetchScalarGridSpec(
        num_scalar_prefetch=2,
        grid=(batch, len(pairs)),
        in_specs=[
            pl.BlockSpec((t, width), lambda *a: (q_row(*a), 0)),
            pl.BlockSpec((t, width), lambda *a: (k_row(*a), 1)),
            pl.BlockSpec((width, t), lambda *a: (0, k_row(*a))),
            pl.BlockSpec((t, LANES), lambda *a: (k_row(*a), 0)),
            pl.BlockSpec((None, N_GATES, t), lambda b, p, qi_tab, ki_tab: (b, 0, qi_tab[p])),
            pl.BlockSpec((1, width), lambda *a: (0, 0)),
        ],
        out_specs=pl.BlockSpec((t, width), lambda *a: (q_row(*a), 0)),
        scratch_shapes=[
            pltpu.VMEM((FOX_HEADS, FOX_AUG_DEPTH, t), BF16),
            pltpu.VMEM((FOX_HEADS, t), F32),
            pltpu.VMEM((FOX_HEADS, FOX_HEAD_DIM + FOX_ONES_ROWS, t), F32),
        ],
    )
    return pl.pallas_call(
        _fox_kernel,
        grid_spec=grid_spec,
        out_shape=jax.ShapeDtypeStruct((batch * s_len, width), BF16),
        compiler_params=_cparams(("parallel", "arbitrary")),
        name="fox_attention",
    )(qi_tab, ki_tab, proj, proj, vt, ka, grow, out_norm.reshape(1, width))


def _mlstm_kernel(mqk_ref, mv_ref, mo_ref, gcol_ref, grow_ref, cw_ref, cb_ref, on_ref, out_ref,
                  ct_ref, mg_ref, ext_ref, shift_ref, qk_ref):
    c = pl.program_id(1)
    length = mqk_ref.shape[0]
    dk, dv = MLSTM_QK_DIM, MLSTM_V_DIM
    n_qk = MLSTM_HEADS * dk
    head = ext_ref.shape[0] - length
    tail = BF16_SUBLANES

    @pl.when(c == 0)
    def _():
        ct_ref[...] = jnp.zeros(ct_ref.shape, F32)
        mg_ref[...] = jnp.zeros(mg_ref.shape, F32)
        ext_ref[0:head, :] = jnp.zeros((head, ext_ref.shape[1]), BF16)
        row = lax.broadcasted_iota(jnp.int32, shift_ref.shape[1:], 0)
        col = lax.broadcasted_iota(jnp.int32, shift_ref.shape[1:], 1)
        for s in range(1, CONV_WIDTH):
            shift_ref[s - 1] = jnp.where(col == row + head - s, 1.0, 0.0).astype(BF16)

    u = mqk_ref[...]
    ext_ref[head:, :] = u
    ext = ext_ref[...]
    y = cb_ref[...] + cw_ref[CONV_WIDTH - 1:CONV_WIDTH, :] * u.astype(F32)
    for s in range(1, CONV_WIDTH):
        tap = CONV_WIDTH - 1 - s
        y = y + cw_ref[tap:tap + 1, :] * jnp.dot(shift_ref[s - 1], ext, preferred_element_type=F32)
    ext_ref[head - tail:head, :] = u[length - tail:, :]
    y = y * _sigmoid(y)
    qk_ref[:, 0:n_qk] = y[:, 0:n_qk].astype(BF16)
    qk_ref[:, n_qk:] = (y[:, n_qk:] * dk ** -0.5).astype(BF16)

    keep = (lax.broadcasted_iota(jnp.int32, (length, length), 0)
            >= lax.broadcasted_iota(jnp.int32, (length, length), 1))
    ones = jnp.ones((length, LANES), BF16)

    def scores(h):
        qb = qk_ref[:, h * dk:(h + 1) * dk]
        kb = qk_ref[:, n_qk + h * dk:n_qk + (h + 1) * dk]
        gi, gf = FOX_HEADS + h, FOX_HEADS + MLSTM_HEADS + h
        u_row = grow_ref[gi:gi + 1, :] - grow_ref[gf:gf + 1, :]
        mg = mg_ref[h:h + 1, 0:1]
        um = jnp.where(keep, u_row, NEG_BIG)
        a = jnp.maximum(jnp.max(um, axis=-1, keepdims=True), mg)
        sqk = lax.dot_general(qb, kb, (((1,), (1,)), ((), ())), preferred_element_type=F32) * jnp.exp(um - a)
        mg_new = jnp.maximum(mg, jnp.max(u_row, axis=-1, keepdims=True))
        return sqk, a, mg, mg_new

    def outputs(h, sqk, a, mg, mg_new):
        qb = qk_ref[:, h * dk:(h + 1) * dk]
        kb = qk_ref[:, n_qk + h * dk:n_qk + (h + 1) * dk]
        sl = slice(h * dv, (h + 1) * dv)
        v_aug = jnp.concatenate([mv_ref[:, sl], ones], axis=1)
        gi, gf = FOX_HEADS + h, FOX_HEADS + MLSTM_HEADS + h
        f_col = gcol_ref[:, gf:gf + 1]
        u_col = gcol_ref[:, gi:gi + 1] - f_col
        ct = ct_ref[h]
        inter = jnp.dot(qb, ct.astype(BF16), preferred_element_type=F32)
        nd = jnp.dot(sqk.astype(BF16), v_aug, preferred_element_type=F32) + jnp.exp(mg - a) * inter
        den = jnp.maximum(jnp.abs(nd[:, dv:dv + 1]), jnp.exp(-(f_col + a)))
        hid = nd[:, :dv] * (1.0 / den)
        gate = _sigmoid(mo_ref[:, sl].astype(F32))
        out_ref[:, sl] = (_rms_rows(hid, on_ref[:, sl]) * gate).astype(out_ref.dtype)

        wav = (jnp.exp(u_col - mg_new) * v_aug.astype(F32)).astype(BF16)
        kt = kb.astype(F32).T.astype(BF16)
        ct_ref[h] = jnp.exp(mg - mg_new) * ct + jnp.dot(kt, wav, preferred_element_type=F32)
        mg_ref[h:h + 1, :] = jnp.broadcast_to(mg_new, (1, LANES))

    pending = scores(0)
    for h in range(1, MLSTM_HEADS):
        nxt = scores(h)
        outputs(h - 1, *pending)
        pending = nxt
    outputs(MLSTM_HEADS - 1, *pending)


def _mlstm(proj, gcol, grow, conv_w, conv_b, out_norm, *, batch, s_len):
    length = MLSTM_CHUNK
    nc = s_len // length
    width = MLSTM_HEADS * MLSTM_V_DIM
    qk_width = 2 * MLSTM_HEADS * MLSTM_QK_DIM
    return pl.pallas_call(
        _mlstm_kernel,
        grid=(batch, nc),
        in_specs=[
            pl.BlockSpec((length, qk_width), lambda b, c: (b * nc + c, 2)),
            pl.BlockSpec((length, width), lambda b, c: (b * nc + c, 3)),
            pl.BlockSpec((length, width), lambda b, c: (b * nc + c, 4)),
            pl.BlockSpec((length, LANES), lambda b, c: (b * nc + c, 0)),
            pl.BlockSpec((None, N_GATES, length), lambda b, c: (b, 0, c)),
            pl.BlockSpec((CONV_WIDTH, qk_width), lambda b, c: (0, 0)),
            pl.BlockSpec((1, qk_width), lambda b, c: (0, 0)),
            pl.BlockSpec((1, width), lambda b, c: (0, 0)),
        ],
        out_specs=pl.BlockSpec((length, width), lambda b, c: (b * nc + c, 0)),
        out_shape=jax.ShapeDtypeStruct((batch * s_len, width), BF16),
        scratch_shapes=[
            pltpu.VMEM((MLSTM_HEADS, MLSTM_QK_DIM, MLSTM_V_DIM + LANES), F32),
            pltpu.VMEM((SUBLANES, LANES), F32),
            pltpu.VMEM((LANES + length, qk_width), BF16),
            pltpu.VMEM((CONV_WIDTH - 1, length, LANES + length), BF16),
            pltpu.VMEM((length, qk_width), BF16),
        ],
        compiler_params=_cparams(("parallel", "arbitrary")),
        name="mlstm",
    )(proj, proj, proj, gcol, grow, conv_w, conv_b.reshape(1, qk_width), out_norm.reshape(1, width))


def _xattn_kernel(q_ref, k_ref, v_ref, o_ref):
    d = q_ref.shape[1] // XATTN_HEADS

    def scores(h):
        sl = slice(h * d, (h + 1) * d)
        return lax.dot_general(q_ref[:, sl], k_ref[:, sl], (((1,), (1,)), ((), ())), preferred_element_type=F32)

    def outputs(h, s):
        sl = slice(h * d, (h + 1) * d)
        p = jnp.exp2(s - jnp.max(s, axis=-1, keepdims=True))
        p = p * (1.0 / jnp.sum(p, axis=-1, keepdims=True))
        o_ref[:, sl] = jnp.dot(p.astype(BF16), v_ref[:, sl], preferred_element_type=F32).astype(o_ref.dtype)

    _tile_pipeline(XATTN_HEADS, scores, outputs)


def _xattn(q, kv, *, batch, s_len, n_mem, tq):
    d_model = q.shape[1]
    nq = s_len // tq
    return pl.pallas_call(
        _xattn_kernel,
        grid=(batch, nq),
        in_specs=[
            pl.BlockSpec((tq, d_model), lambda b, i: (b * nq + i, 0)),
            pl.BlockSpec((n_mem, d_model), lambda b, i: (b, 0)),
            pl.BlockSpec((n_mem, d_model), lambda b, i: (b, 1)),
        ],
        out_specs=pl.BlockSpec((tq, d_model), lambda b, i: (b * nq + i, 0)),
        out_shape=jax.ShapeDtypeStruct(q.shape, BF16),
        compiler_params=_cparams(("parallel", "parallel")),
        name="xattn",
    )(q, kv, kv)


def _mlp_kernel(x_ref, g_ref, wu_ref, wd_ref, o_ref, xn_ref):
    f = pl.program_id(1)

    @pl.when(f == 0)
    def _():
        x = x_ref[...]
        xn_ref[...] = _rms_rows(x, g_ref[...]).astype(BF16)
        o_ref[...] = x

    tc = MLP_CHUNK

    def up(j):
        return jnp.dot(xn_ref[...], wu_ref[:, j * tc:(j + 1) * tc], preferred_element_type=F32)

    def down(j, hid):
        hid = jnp.square(jnp.maximum(hid, 0.0)).astype(BF16)
        o_ref[...] += jnp.dot(hid, wd_ref[j * tc:(j + 1) * tc, :], preferred_element_type=F32)

    _tile_pipeline(wu_ref.shape[1] // tc, up, down)


def _mlp(x, gain, w_up, w_down, *, tm, tf):
    m, d = x.shape
    d_ff = w_up.shape[1]
    return pl.pallas_call(
        _mlp_kernel,
        grid=(m // tm, d_ff // tf),
        in_specs=[
            pl.BlockSpec((tm, d), lambda i, f: (i, 0)),
            pl.BlockSpec((1, d), lambda i, f: (0, 0)),
            pl.BlockSpec((d, tf), lambda i, f: (0, f)),
            pl.BlockSpec((tf, d), lambda i, f: (f, 0)),
        ],
        out_specs=pl.BlockSpec((tm, d), lambda i, f: (i, 0)),
        out_shape=jax.ShapeDtypeStruct((m, d), F32),
        scratch_shapes=[pltpu.VMEM((tm, d), BF16)],
        compiler_params=_cparams(("parallel", "arbitrary")),
        name="mlp",
    )(x, gain.reshape(1, d), w_up, w_down)


def _layer(x, mem, mixer_norm, w_in, fox_f_bias, mlstm_i_bias, mlstm_f_bias, conv_w, conv_b, fox_q_norm,
           fox_k_norm, fox_out_norm, mlstm_out_norm, w_out, xattn_norm, mem_norm, w_xq, w_xkv, xq_norm,
           xk_norm, w_xo, mlp_norm, w_up, w_down, *, batch, s_len, n_mem):
    fox_w = FOX_HEADS * FOX_HEAD_DIM
    mqk_w = 2 * MLSTM_HEADS * MLSTM_QK_DIM
    mv_w = MLSTM_HEADS * MLSTM_V_DIM
    d_model = x.shape[1]

    o_ff = 3 * fox_w
    o_mqk = o_ff + FOX_HEADS
    o_mv = o_mqk + mqk_w
    o_mi = o_mv + mv_w
    o_mo = o_mi + 2 * MLSTM_HEADS
    w_main, w_gates = _w_in_prep(w_in.T, o_ff=o_ff, o_mi=o_mi, tb=512)
    gate_bias = jnp.concatenate([fox_f_bias, mlstm_i_bias, mlstm_f_bias]).reshape(N_GATES, 1)

    proj, vt, gates_raw = _in_proj(x, mixer_norm, w_main, jnp.stack([fox_q_norm, fox_k_norm]), w_gates, tm=256)
    grow, gcol, ka = _gates(gates_raw, gate_bias, batch=batch, s_len=s_len)
    fo = _fox_attention(proj, vt, ka, grow, fox_out_norm, batch=batch, s_len=s_len)
    mh = _mlstm(proj, gcol, grow, conv_w, conv_b, mlstm_out_norm, batch=batch, s_len=s_len)
    x = _proj_residual([fo, mh], w_out, x, tm=512, tn=512)

    xd = d_model // XATTN_HEADS
    q = _norm_proj(x, xattn_norm, w_xq, xq_norm, tm=512,
                   post_scale=xd ** -0.5 * LOG2E)
    kv = _kv_proj(mem, mem_norm, w_xkv, xk_norm, n_norm_tiles=XATTN_HEADS)
    co = _xattn(q, kv, batch=batch, s_len=s_len, n_mem=n_mem, tq=1024)
    x = _proj_residual([co], w_xo, x, tm=512, tn=512)

    return _mlp(x, mlp_norm, w_up.astype(BF16), w_down.astype(BF16), tm=1024, tf=1024)


def kernel(x, mem, mixer_norm, w_in, fox_f_bias, mlstm_i_bias, mlstm_f_bias, conv_w, conv_b, fox_q_norm,
           fox_k_norm, fox_out_norm, mlstm_out_norm, w_out, xattn_norm, mem_norm, w_xq, w_xkv, xq_norm,
           xk_norm, w_xo, mlp_norm, w_up, w_down):
    batch, s_len, d_model = x.shape
    n_mem = mem.shape[1]
    depth = w_in.shape[0]
    h = x.reshape(batch * s_len, d_model)
    mem2 = mem.reshape(batch * n_mem, d_model)
    for l in range(depth):
        h = _layer(h, mem2, mixer_norm[l], w_in[l], fox_f_bias[l], mlstm_i_bias[l], mlstm_f_bias[l],
                   conv_w[l], conv_b[l], fox_q_norm[l], fox_k_norm[l], fox_out_norm[l], mlstm_out_norm[l],
                   w_out[l], xattn_norm[l], mem_norm[l], w_xq[l], w_xkv[l], xq_norm[l], xk_norm[l], w_xo[l],
                   mlp_norm[l], w_up[l], w_down[l], batch=batch, s_len=s_len, n_mem=n_mem)
    return h.reshape(batch, s_len, d_model)
```

```python
import functools

import jax
import jax.numpy as jnp
from jax import lax
from jax.experimental import pallas as pl
from jax.experimental.pallas import tpu as pltpu

F32 = jnp.float32
BF16 = jnp.bfloat16

EPS = 1e-6
FOX_HEADS = 8
FOX_HEAD_DIM = 128
MLSTM_HEADS = 4
MLSTM_QK_DIM = 128
MLSTM_V_DIM = 256
CONV_WIDTH = 4
XATTN_HEADS = 4
N_GATES = 16
LANES = 128
SUBLANES = 8
BF16_SUBLANES = 16
NEG_BIG = -1e30
LOG2E = 1.4426950408889634
FOX_AUG_DEPTH = 256
FOX_ONES_ROWS = 16

MLSTM_CHUNK = 256
FOX_BLOCK = 512
FOX_KEY_SUB = 256
IN_PROJ_TILE = 512
MLP_CHUNK = 256
VMEM_LIMIT = 56 * 1024 * 1024


def _cparams(sem):
    return pltpu.CompilerParams(dimension_semantics=sem, vmem_limit_bytes=VMEM_LIMIT)


def _rms_rows(x, gain):
    ms = jnp.mean(x * x, axis=-1, keepdims=True)
    return x * lax.rsqrt(ms + EPS) * gain


def _sigmoid(x):
    return 1.0 / (1.0 + jnp.exp(-x))


def _split3(x):
    hi = x.astype(BF16).astype(F32)
    r = x - hi
    mid = r.astype(BF16).astype(F32)
    lo = (r - mid).astype(BF16).astype(F32)
    return hi, mid, lo


def _resident(block_shape):
    return pl.BlockSpec(block_shape, lambda i: (0,) * len(block_shape), pipeline_mode=pl.Buffered(1))


def _tile_pipeline(n_tiles, matmul, epilogue):
    pending = matmul(0)
    for j in range(1, n_tiles):
        nxt = matmul(j)
        epilogue(j - 1, pending)
        pending = nxt
    epilogue(n_tiles - 1, pending)


def _w_in_prep_kernel(wt_ref, gf_ref, gm_ref, main_ref, gates_ref):
    main_ref[...] = wt_ref[...].T.astype(BF16)

    @pl.when(pl.program_id(0) == 0)
    def _():
        pad = jnp.zeros((LANES - N_GATES, gf_ref.shape[1]), F32)
        gates_ref[...] = jnp.concatenate([gf_ref[...], gm_ref[...], pad], axis=0).T.astype(BF16)


def _w_in_prep(wt, *, o_ff, o_mi, tb):
    n_all, k = wt.shape
    n_main = n_all - N_GATES
    n_fox, n_before_mi = o_ff // tb, (o_mi - FOX_HEADS) // tb
    assert o_ff % tb == 0 and (o_mi - FOX_HEADS) % tb == 0 and n_main % tb == 0

    def src_row(b):
        return b * tb + jnp.where(b >= n_fox, FOX_HEADS, 0) + jnp.where(b >= n_before_mi, 2 * MLSTM_HEADS, 0)

    return pl.pallas_call(
        _w_in_prep_kernel,
        grid=(n_main // tb,),
        in_specs=[
            pl.BlockSpec((pl.Element(tb), pl.Element(k)), lambda b: (pl.multiple_of(src_row(b), SUBLANES), 0)),
            pl.BlockSpec((pl.Element(FOX_HEADS), pl.Element(k)), lambda b: (o_ff, 0)),
            pl.BlockSpec((pl.Element(2 * MLSTM_HEADS), pl.Element(k)), lambda b: (o_mi, 0)),
        ],
        out_specs=[pl.BlockSpec((k, tb), lambda b: (0, b)), pl.BlockSpec((k, LANES), lambda b: (0, 0))],
        out_shape=[jax.ShapeDtypeStruct((k, n_main), BF16), jax.ShapeDtypeStruct((k, LANES), BF16)],
        compiler_params=_cparams(("arbitrary",)),
        name="w_in_prep",
    )(wt, wt, wt)


def _in_proj_kernel(x_ref, g_ref, w_ref, qk_gain_ref, wg_ref, out_ref, vt_ref, gates_ref):
    seg_w = FOX_HEADS * FOX_HEAD_DIM
    tn = IN_PROJ_TILE
    xn = _rms_rows(x_ref[...], g_ref[...]).astype(BF16)
    sources = (2 * seg_w, 0, seg_w, 3 * seg_w, 4 * seg_w, 5 * seg_w)

    def matmul(j):
        seg, off = divmod(j * tn, seg_w)
        col0 = sources[seg] + off
        return jnp.dot(xn, w_ref[:, col0:col0 + tn], preferred_element_type=F32)

    def head_norm(acc, gain, post_scale, col0):
        for h in range(tn // FOX_HEAD_DIM):
            sl = slice(h * FOX_HEAD_DIM, (h + 1) * FOX_HEAD_DIM)
            y = _rms_rows(acc[:, sl], gain) * post_scale
            out_ref[:, col0 + h * FOX_HEAD_DIM:col0 + (h + 1) * FOX_HEAD_DIM] = y.astype(out_ref.dtype)

    def epilogue(j, acc):
        seg, off = divmod(j * tn, seg_w)
        if seg == 0:
            vt_ref[off:off + tn, :] = acc.T.astype(vt_ref.dtype)
        elif seg == 1:
            head_norm(acc, qk_gain_ref[0], FOX_HEAD_DIM ** -0.5 * LOG2E, off)
        elif seg == 2:
            head_norm(acc, qk_gain_ref[1], 1.0, seg_w + off)
        else:
            out_ref[:, (seg - 1) * seg_w + off:(seg - 1) * seg_w + off + tn] = acc.astype(out_ref.dtype)

    gates_ref[...] = jnp.dot(xn, wg_ref[...], preferred_element_type=F32)
    _tile_pipeline(len(sources) * seg_w // tn, matmul, epilogue)


def _in_proj(x, gain, w, qk_gains, w_gates, *, tm):
    m, k = x.shape
    tn = FOX_HEADS * FOX_HEAD_DIM
    n = w.shape[1]
    return pl.pallas_call(
        _in_proj_kernel,
        grid=(m // tm,),
        in_specs=[
            pl.BlockSpec((tm, k), lambda i: (i, 0)),
            _resident((1, k)),
            _resident((k, n)),
            _resident((2, 1, FOX_HEAD_DIM)),
            _resident((k, LANES)),
        ],
        out_specs=[
            pl.BlockSpec((tm, n - tn), lambda i: (i, 0)),
            pl.BlockSpec((tn, tm), lambda i: (0, i)),
            pl.BlockSpec((tm, LANES), lambda i: (i, 0)),
        ],
        out_shape=[
            jax.ShapeDtypeStruct((m, n - tn), BF16),
            jax.ShapeDtypeStruct((tn, m), BF16),
            jax.ShapeDtypeStruct((m, LANES), F32),
        ],
        compiler_params=_cparams(("parallel",)),
        name="in_proj",
    )(x, gain.reshape(1, k), w, qk_gains.reshape(2, 1, FOX_HEAD_DIM), w_gates)


def _bf16_copy(w_ref, wb_ref):
    @pl.when(pl.program_id(0) == 0)
    def _():
        wb_ref[...] = w_ref[...].astype(BF16)
    return wb_ref


def _norm_proj_kernel(x_ref, g_ref, w_ref, hg_ref, out_ref, wb_ref, *, post_scale):
    tn = hg_ref.shape[1]
    _bf16_copy(w_ref, wb_ref)
    xn = _rms_rows(x_ref[...], g_ref[...]).astype(BF16)

    def matmul(j):
        return jnp.dot(xn, wb_ref[:, j * tn:(j + 1) * tn], preferred_element_type=F32)

    def epilogue(j, acc):
        out_ref[:, j * tn:(j + 1) * tn] = (_rms_rows(acc, hg_ref[...]) * post_scale).astype(out_ref.dtype)

    _tile_pipeline(w_ref.shape[1] // tn, matmul, epilogue)


def _norm_proj(x, gain, w, head_gain, *, tm, post_scale):
    m, k = x.shape
    n = w.shape[1]
    tn = head_gain.shape[0]
    return pl.pallas_call(
        functools.partial(_norm_proj_kernel, post_scale=post_scale),
        grid=(m // tm,),
        in_specs=[
            pl.BlockSpec((tm, k), lambda i: (i, 0)),
            _resident((1, k)),
            _resident((k, n)),
            _resident((1, tn)),
        ],
        out_specs=pl.BlockSpec((tm, n), lambda i: (i, 0)),
        out_shape=jax.ShapeDtypeStruct((m, n), BF16),
        scratch_shapes=[pltpu.VMEM((k, n), BF16)],
        compiler_params=_cparams(("arbitrary",)),
        name="norm_proj",
    )(x, gain.reshape(1, k), w, head_gain.reshape(1, tn))


def _kv_proj_kernel(x_ref, g_ref, w_ref, hg_ref, out_ref, xn_ref, *, n_norm_tiles):
    j = pl.program_id(0)

    @pl.when(j == 0)
    def _():
        xn_ref[...] = _rms_rows(x_ref[...], g_ref[...]).astype(BF16)

    acc = jnp.dot(xn_ref[...], w_ref[...].astype(BF16), preferred_element_type=F32)

    @pl.when(j < n_norm_tiles)
    def _():
        out_ref[...] = _rms_rows(acc, hg_ref[...]).astype(out_ref.dtype)

    @pl.when(j >= n_norm_tiles)
    def _():
        out_ref[...] = acc.astype(out_ref.dtype)


def _kv_proj(x, gain, w, head_gain, *, n_norm_tiles):
    m, k = x.shape
    n = w.shape[1]
    tn = head_gain.shape[0]
    return pl.pallas_call(
        functools.partial(_kv_proj_kernel, n_norm_tiles=n_norm_tiles),
        grid=(n // tn,),
        in_specs=[
            _resident((m, k)),
            _resident((1, k)),
            pl.BlockSpec((k, tn), lambda j: (0, j)),
            _resident((1, tn)),
        ],
        out_specs=pl.BlockSpec((m, tn), lambda j: (0, j)),
        out_shape=jax.ShapeDtypeStruct((m, n), BF16),
        scratch_shapes=[pltpu.VMEM((m, k), BF16)],
        compiler_params=_cparams(("arbitrary",)),
        name="kv_proj",
    )(x, gain.reshape(1, k), w, head_gain.reshape(1, tn))


def _proj_residual_kernel(*refs, n_lhs, tn):
    lhs = refs[:n_lhs]
    w_ref, res_ref, out_ref, wb_ref = refs[n_lhs:]
    kp = lhs[0].shape[1]
    _bf16_copy(w_ref, wb_ref)

    def matmul(j):
        cols = slice(j * tn, (j + 1) * tn)
        acc = jnp.dot(lhs[0][...], wb_ref[0:kp, cols], preferred_element_type=F32)
        for p in range(1, n_lhs):
            acc = acc + jnp.dot(lhs[p][...], wb_ref[p * kp:(p + 1) * kp, cols], preferred_element_type=F32)
        return acc

    def epilogue(j, acc):
        cols = slice(j * tn, (j + 1) * tn)
        out_ref[:, cols] = res_ref[:, cols] + acc

    _tile_pipeline(w_ref.shape[1] // tn, matmul, epilogue)


def _proj_residual(lhs_list, w, residual, *, tm, tn):
    m, n = residual.shape
    n_lhs = len(lhs_list)
    kp = lhs_list[0].shape[1]
    in_specs = [pl.BlockSpec((tm, kp), lambda i: (i, 0)) for _ in lhs_list]
    in_specs += [_resident(w.shape), pl.BlockSpec((tm, n), lambda i: (i, 0))]
    return pl.pallas_call(
        functools.partial(_proj_residual_kernel, n_lhs=n_lhs, tn=tn),
        grid=(m // tm,),
        in_specs=in_specs,
        out_specs=pl.BlockSpec((tm, n), lambda i: (i, 0)),
        out_shape=jax.ShapeDtypeStruct((m, n), F32),
        scratch_shapes=[pltpu.VMEM(w.shape, BF16)],
        compiler_params=_cparams(("arbitrary",)),
        name="proj_residual",
    )(*lhs_list, w, residual)


def _gates_kernel(g_ref, bias_ref, row_ref, col_ref, ka_ref):
    s_len = g_ref.shape[0]
    z = g_ref[...].T[0:N_GATES, :] + bias_ref[...]
    row = lax.broadcasted_iota(jnp.int32, z.shape, 0)
    is_forget = (row < FOX_HEADS) | (row >= FOX_HEADS + MLSTM_HEADS)
    logsig = jnp.minimum(z, 0.0) - jnp.log1p(jnp.exp(-jnp.abs(z)))
    c = jnp.where(is_forget, logsig, 0.0)
    lane = lax.broadcasted_iota(jnp.int32, z.shape, 1)
    shift = 1
    while shift < s_len:
        c = c + jnp.where(lane >= shift, pltpu.roll(c, shift, axis=1), 0.0)
        shift *= 2
    vals = jnp.where(is_forget, c, z)
    row_ref[...] = vals
    padded = jnp.concatenate([vals, jnp.zeros((LANES - N_GATES, s_len), F32)], axis=0)
    col_ref[...] = padded.T
    hi, mid, lo = _split3(c[0:FOX_HEADS, :] * LOG2E)
    aug = jnp.concatenate([jnp.ones((FOX_HEADS, s_len), F32), -hi, -mid, -lo,
                           jnp.zeros((LANES - 4 * FOX_HEADS, s_len), F32)], axis=0)
    ka_ref[...] = aug.T.astype(ka_ref.dtype)


def _gates(gates_raw, bias, *, batch, s_len):
    return pl.pallas_call(
        _gates_kernel,
        grid=(batch,),
        in_specs=[pl.BlockSpec((s_len, LANES), lambda b: (b, 0)),
                  pl.BlockSpec((N_GATES, 1), lambda b: (0, 0))],
        out_specs=[pl.BlockSpec((None, N_GATES, s_len), lambda b: (b, 0, 0)),
                   pl.BlockSpec((s_len, LANES), lambda b: (b, 0)),
                   pl.BlockSpec((s_len, LANES), lambda b: (b, 0))],
        out_shape=[jax.ShapeDtypeStruct((batch, N_GATES, s_len), F32),
                   jax.ShapeDtypeStruct((batch * s_len, LANES), F32),
                   jax.ShapeDtypeStruct((batch * s_len, LANES), BF16)],
        compiler_params=_cparams(("parallel",)),
        name="gates",
    )(gates_raw, bias)


def _fox_kernel(qi_ref, ki_ref, q_ref, k_ref, vt_ref, ka_ref, grow_ref, on_ref, o_ref, qt_ref, m_ref, acc_ref):
    pair = pl.program_id(1)
    qi = qi_ref[pair]
    ki = ki_ref[pair]
    tq, tk = q_ref.shape[0], k_ref.shape[0]
    d = FOX_HEAD_DIM

    @pl.when(ki == 0)
    def _():
        m_ref[...] = jnp.full(m_ref.shape, NEG_BIG, F32)
        acc_ref[...] = jnp.zeros(acc_ref.shape, F32)
        sub = lax.broadcasted_iota(jnp.int32, (SUBLANES, tq), 0)
        for h in range(FOX_HEADS):
            hi, mid, lo = _split3(grow_ref[h:h + 1, :] * LOG2E)
            c_rows = jnp.where(sub == 0, hi, jnp.where(sub == 1, mid, jnp.where(sub == 2, lo, 0.0)))
            pick = jnp.where(sub == h, 1.0, 0.0)
            qt = q_ref[:, h * d:(h + 1) * d].astype(F32).T
            pad = jnp.zeros((FOX_AUG_DEPTH - d - 4 * SUBLANES, tq), F32)
            qt_ref[h] = jnp.concatenate([qt, c_rows, pick, pick, pick, pad], axis=0).astype(BF16)

    def step(diagonal):
        sub = FOX_KEY_SUB
        ones = jnp.ones((FOX_ONES_ROWS, sub), BF16)
        if diagonal:
            keep = (lax.broadcasted_iota(jnp.int32, (sub, tq), 0)
                    <= lax.broadcasted_iota(jnp.int32, (sub, tq), 1))

        def scores(h, ks):
            t0 = ks * sub if diagonal else 0
            rows = slice(ks * sub, (ks + 1) * sub)
            k_aug = jnp.concatenate([k_ref[rows, h * d:(h + 1) * d], ka_ref[rows, :]], axis=1)
            st = jnp.dot(k_aug, qt_ref[h, :, t0:], preferred_element_type=F32)
            if diagonal:
                st = jnp.where(keep[:, 0:tq - t0], st, NEG_BIG)
            m_prev = m_ref[h:h + 1, t0:]
            m_new = jnp.maximum(m_prev, jnp.max(st, axis=0, keepdims=True))
            m_ref[h:h + 1, t0:] = m_new
            return t0, st, m_new, jnp.exp2(m_prev - m_new)

        def accumulate(h, ks, t0, st, m_new, alpha):
            p = jnp.exp2(st - m_new)
            vt_aug = jnp.concatenate([vt_ref[h * d:(h + 1) * d, ks * sub:(ks + 1) * sub], ones], axis=0)
            acc_ref[h, :, t0:] = alpha * acc_ref[h, :, t0:] + jnp.dot(vt_aug, p.astype(BF16),
                                                                      preferred_element_type=F32)

        items = [(h, ks) for h in range(FOX_HEADS) for ks in range(tk // sub)]
        lookahead = 2
        pending = [scores(*it) for it in items[:lookahead]]
        for n, it in enumerate(items):
            if n + lookahead < len(items):
                pending.append(scores(*items[n + lookahead]))
            accumulate(*it, *pending.pop(0))

    @pl.when(ki < qi)
    def _():
        step(False)

    @pl.when(ki == qi)
    def _():
        step(True)
        for h in range(FOX_HEADS):
            sl = slice(h * d, (h + 1) * d)
            o = (acc_ref[h, 0:d, :] / acc_ref[h, d:d + 1, :]).T
            o_ref[:, sl] = _rms_rows(o, on_ref[:, sl]).astype(o_ref.dtype)


def _fox_attention(proj, vt, ka, grow, out_norm, *, batch, s_len):
    t = FOX_BLOCK
    nq = s_len // t
    width = FOX_HEADS * FOX_HEAD_DIM
    pairs = [(qi, ki) for qi in range(nq) for ki in range(qi + 1)]
    qi_tab = jnp.asarray([p[0] for p in pairs], jnp.int32)
    ki_tab = jnp.asarray([p[1] for p in pairs], jnp.int32)

    def q_row(b, p, qi_tab, ki_tab):
        return b * nq + qi_tab[p]

    def k_row(b, p, qi_tab, ki_tab):
        return b * nq + ki_tab[p]

    grid_spec = pltpu.PrefetchScalarGridSpec(
        num_scalar_prefetch=2,
        grid=(batch, len(pairs)),
        in_specs=[
            pl.BlockSpec((t, width), lambda *a: (q_row(*a), 0)),
            pl.BlockSpec((t, width), lambda *a: (k_row(*a), 1)),
            pl.BlockSpec((width, t), lambda *a: (0, k_row(*a))),
            pl.BlockSpec((t, LANES), lambda *a: (k_row(*a), 0)),
            pl.BlockSpec((None, N_GATES, t), lambda b, p, qi_tab, ki_tab: (b, 0, qi_tab[p])),
            pl.BlockSpec((1, width), lambda *a: (0, 0)),
        ],
        out_specs=pl.BlockSpec((t, width), lambda *a: (q_row(*a), 0)),
        scratch_shapes=[
            pltpu.VMEM((FOX_HEADS, FOX_AUG_DEPTH, t), BF16),
            pltpu.VMEM((FOX_HEADS, t), F32),
            pltpu.VMEM((FOX_HEADS, FOX_HEAD_DIM + FOX_ONES_ROWS, t), F32),
        ],
    )
    return pl.pallas_call(
        _fox_kernel,
        grid_spec=grid_spec,
        out_shape=jax.ShapeDtypeStruct((batch * s_len, width), BF16),
        compiler_params=_cparams(("parallel", "arbitrary")),
        name="fox_attention",
    )(qi_tab, ki_tab, proj, proj, vt, ka, grow, out_norm.reshape(1, width))


def _mlstm_kernel(mqk_ref, mv_ref, mo_ref, gcol_ref, grow_ref, cw_ref, cb_ref, on_ref, out_ref,
                  ct_ref, mg_ref, ext_ref, shift_ref, qk_ref):
    c = pl.program_id(1)
    length = mqk_ref.shape[0]
    dk, dv = MLSTM_QK_DIM, MLSTM_V_DIM
    n_qk = MLSTM_HEADS * dk
    head = ext_ref.shape[0] - length
    tail = BF16_SUBLANES

    @pl.when(c == 0)
    def _():
        ct_ref[...] = jnp.zeros(ct_ref.shape, F32)
        mg_ref[...] = jnp.zeros(mg_ref.shape, F32)
        ext_ref[0:head, :] = jnp.zeros((head, ext_ref.shape[1]), BF16)
        row = lax.broadcasted_iota(jnp.int32, shift_ref.shape[1:], 0)
        col = lax.broadcasted_iota(jnp.int32, shift_ref.shape[1:], 1)
        for s in range(1, CONV_WIDTH):
            shift_ref[s - 1] = jnp.where(col == row + head - s, 1.0, 0.0).astype(BF16)

    u = mqk_ref[...]
    ext_ref[head:, :] = u
    ext = ext_ref[...]
    y = cb_ref[...] + cw_ref[CONV_WIDTH - 1:CONV_WIDTH, :] * u.astype(F32)
    for s in range(1, CONV_WIDTH):
        tap = CONV_WIDTH - 1 - s
        y = y + cw_ref[tap:tap + 1, :] * jnp.dot(shift_ref[s - 1], ext, preferred_element_type=F32)
    ext_ref[head - tail:head, :] = u[length - tail:, :]
    y = y * _sigmoid(y)
    qk_ref[:, 0:n_qk] = y[:, 0:n_qk].astype(BF16)
    qk_ref[:, n_qk:] = (y[:, n_qk:] * dk ** -0.5).astype(BF16)

    keep = (lax.broadcasted_iota(jnp.int32, (length, length), 0)
            >= lax.broadcasted_iota(jnp.int32, (length, length), 1))
    ones = jnp.ones((length, LANES), BF16)

    def scores(h):
        qb = qk_ref[:, h * dk:(h + 1) * dk]
        kb = qk_ref[:, n_qk + h * dk:n_qk + (h + 1) * dk]
        gi, gf = FOX_HEADS + h, FOX_HEADS + MLSTM_HEADS + h
        u_row = grow_ref[gi:gi + 1, :] - grow_ref[gf:gf + 1, :]
        mg = mg_ref[h:h + 1, 0:1]
        um = jnp.where(keep, u_row, NEG_BIG)
        a = jnp.maximum(jnp.max(um, axis=-1, keepdims=True), mg)
        sqk = lax.dot_general(qb, kb, (((1,), (1,)), ((), ())), preferred_element_type=F32) * jnp.exp(um - a)
        mg_new = jnp.maximum(mg, jnp.max(u_row, axis=-1, keepdims=True))
        return sqk, a, mg, mg_new

    def outputs(h, sqk, a, mg, mg_new):
        qb = qk_ref[:, h * dk:(h + 1) * dk]
        kb = qk_ref[:, n_qk + h * dk:n_qk + (h + 1) * dk]
        sl = slice(h * dv, (h + 1) * dv)
        v_aug = jnp.concatenate([mv_ref[:, sl], ones], axis=1)
        gi, gf = FOX_HEADS + h, FOX_HEADS + MLSTM_HEADS + h
        f_col = gcol_ref[:, gf:gf + 1]
        u_col = gcol_ref[:, gi:gi + 1] - f_col
        ct = ct_ref[h]
        inter = jnp.dot(qb, ct.astype(BF16), preferred_element_type=F32)
        nd = jnp.dot(sqk.astype(BF16), v_aug, preferred_element_type=F32) + jnp.exp(mg - a) * inter
        den = jnp.maximum(jnp.abs(nd[:, dv:dv + 1]), jnp.exp(-(f_col + a)))
        hid = nd[:, :dv] * (1.0 / den)
        gate = _sigmoid(mo_ref[:, sl].astype(F32))
        out_ref[:, sl] = (_rms_rows(hid, on_ref[:, sl]) * gate).astype(out_ref.dtype)

        wav = (jnp.exp(u_col - mg_new) * v_aug.astype(F32)).astype(BF16)
        kt = kb.astype(F32).T.astype(BF16)
        ct_ref[h] = jnp.exp(mg - mg_new) * ct + jnp.dot(kt, wav, preferred_element_type=F32)
        mg_ref[h:h + 1, :] = jnp.broadcast_to(mg_new, (1, LANES))

    pending = scores(0)
    for h in range(1, MLSTM_HEADS):
        nxt = scores(h)
        outputs(h - 1, *pending)
        pending = nxt
    outputs(MLSTM_HEADS - 1, *pending)


def _mlstm(proj, gcol, grow, conv_w, conv_b, out_norm, *, batch, s_len):
    length = MLSTM_CHUNK
    nc = s_len // length
    width = MLSTM_HEADS * MLSTM_V_DIM
    qk_width = 2 * MLSTM_HEADS * MLSTM_QK_DIM
    return pl.pallas_call(
        _mlstm_kernel,
        grid=(batch, nc),
        in_specs=[
            pl.BlockSpec((length, qk_width), lambda b, c: (b * nc + c, 2)),
            pl.BlockSpec((length, width), lambda b, c: (b * nc + c, 3)),
            pl.BlockSpec((length, width), lambda b, c: (b * nc + c, 4)),
            pl.BlockSpec((length, LANES), lambda b, c: (b * nc + c, 0)),
            pl.BlockSpec((None, N_GATES, length), lambda b, c: (b, 0, c)),
            pl.BlockSpec((CONV_WIDTH, qk_width), lambda b, c: (0, 0)),
            pl.BlockSpec((1, qk_width), lambda b, c: (0, 0)),
            pl.BlockSpec((1, width), lambda b, c: (0, 0)),
        ],
        out_specs=pl.BlockSpec((length, width), lambda b, c: (b * nc + c, 0)),
        out_shape=jax.ShapeDtypeStruct((batch * s_len, width), BF16),
        scratch_shapes=[
            pltpu.VMEM((MLSTM_HEADS, MLSTM_QK_DIM, MLSTM_V_DIM + LANES), F32),
            pltpu.VMEM((SUBLANES, LANES), F32),
            pltpu.VMEM((LANES + length, qk_width), BF16),
            pltpu.VMEM((CONV_WIDTH - 1, length, LANES + length), BF16),
            pltpu.VMEM((length, qk_width), BF16),
        ],
        compiler_params=_cparams(("parallel", "arbitrary")),
        name="mlstm",
    )(proj, proj, proj, gcol, grow, conv_w, conv_b.reshape(1, qk_width), out_norm.reshape(1, width))


def _xattn_kernel(q_ref, k_ref, v_ref, o_ref):
    d = q_ref.shape[1] // XATTN_HEADS

    def scores(h):
        sl = slice(h * d, (h + 1) * d)
        return lax.dot_general(q_ref[:, sl], k_ref[:, sl], (((1,), (1,)), ((), ())), preferred_element_type=F32)

    def outputs(h, s):
        sl = slice(h * d, (h + 1) * d)
        p = jnp.exp2(s - jnp.max(s, axis=-1, keepdims=True))
        p = p * (1.0 / jnp.sum(p, axis=-1, keepdims=True))
        o_ref[:, sl] = jnp.dot(p.astype(BF16), v_ref[:, sl], preferred_element_type=F32).astype(o_ref.dtype)

    _tile_pipeline(XATTN_HEADS, scores, outputs)


def _xattn(q, kv, *, batch, s_len, n_mem, tq):
    d_model = q.shape[1]
    nq = s_len // tq
    return pl.pallas_call(
        _xattn_kernel,
        grid=(batch, nq),
        in_specs=[
            pl.BlockSpec((tq, d_model), lambda b, i: (b * nq + i, 0)),
            pl.BlockSpec((n_mem, d_model), lambda b, i: (b, 0)),
            pl.BlockSpec((n_mem, d_model), lambda b, i: (b, 1)),
        ],
        out_specs=pl.BlockSpec((tq, d_model), lambda b, i: (b * nq + i, 0)),
        out_shape=jax.ShapeDtypeStruct(q.shape, BF16),
        compiler_params=_cparams(("parallel", "parallel")),
        name="xattn",
    )(q, kv, kv)


def _mlp_first_kernel(x_ref, g_ref, wu_ref, wd_ref, o_ref, wub_ref, wdb_ref, xn_ref):
    f = pl.program_id(0)

    @pl.when(f == 0)
    def _():
        x = x_ref[...]
        xn_ref[...] = _rms_rows(x, g_ref[...]).astype(BF16)
        o_ref[...] = x

    wu = wu_ref[...].astype(BF16)
    wd = wd_ref[...].astype(BF16)
    wub_ref[...] = wu
    wdb_ref[...] = wd
    hid = jnp.dot(xn_ref[...], wu, preferred_element_type=F32)
    hid = jnp.square(jnp.maximum(hid, 0.0)).astype(BF16)
    o_ref[...] += jnp.dot(hid, wd, preferred_element_type=F32)


def _mlp_first(x, gain, w_up, w_down, *, tm, tf):
    d = x.shape[1]
    d_ff = w_up.shape[1]
    return pl.pallas_call(
        _mlp_first_kernel,
        grid=(d_ff // tf,),
        in_specs=[
            _resident((tm, d)),
            _resident((1, d)),
            pl.BlockSpec((d, tf), lambda f: (0, f)),
            pl.BlockSpec((tf, d), lambda f: (f, 0)),
        ],
        out_specs=[
            pl.BlockSpec((tm, d), lambda f: (0, 0)),
            pl.BlockSpec((d, tf), lambda f: (0, f)),
            pl.BlockSpec((tf, d), lambda f: (f, 0)),
        ],
        out_shape=[
            jax.ShapeDtypeStruct((tm, d), F32),
            jax.ShapeDtypeStruct((d, d_ff), BF16),
            jax.ShapeDtypeStruct((d_ff, d), BF16),
        ],
        scratch_shapes=[pltpu.VMEM((tm, d), BF16)],
        compiler_params=_cparams(("arbitrary",)),
        name="mlp_first",
    )(x, gain.reshape(1, d), w_up, w_down)


def _mlp_kernel(x_ref, g_ref, wu_ref, wd_ref, o_ref, xn_ref):
    i = pl.program_id(0)
    f = pl.program_id(1)

    @pl.when(f == 0)
    def _():
        x = x_ref[...]
        xn_ref[...] = _rms_rows(x, g_ref[...]).astype(BF16)
        o_ref[...] = x

    tc = MLP_CHUNK

    def up(j):
        return jnp.dot(xn_ref[...], wu_ref[:, j * tc:(j + 1) * tc], preferred_element_type=F32)

    def down(j, hid):
        hid = jnp.square(jnp.maximum(hid, 0.0)).astype(BF16)
        o_ref[...] += jnp.dot(hid, wd_ref[j * tc:(j + 1) * tc, :], preferred_element_type=F32)

    @pl.when(i > 0)
    def _():
        _tile_pipeline(wu_ref.shape[1] // tc, up, down)


def _mlp(x, gain, w_up, w_down, *, tm, tf):
    m, d = x.shape
    d_ff = w_up.shape[1]

    def tile(i, f):
        return jnp.where(i == 0, 0, f)

    return pl.pallas_call(
        _mlp_kernel,
        grid=(m // tm, d_ff // tf),
        in_specs=[
            pl.BlockSpec((tm, d), lambda i, f: (i, 0)),
            pl.BlockSpec((1, d), lambda i, f: (0, 0)),
            pl.BlockSpec((d, tf), lambda i, f: (0, tile(i, f))),
            pl.BlockSpec((tf, d), lambda i, f: (tile(i, f), 0)),
        ],
        out_specs=pl.BlockSpec((tm, d), lambda i, f: (i, 0)),
        out_shape=jax.ShapeDtypeStruct((m, d), F32),
        scratch_shapes=[pltpu.VMEM((tm, d), BF16)],
        compiler_params=_cparams(("parallel", "arbitrary")),
        name="mlp",
    )(x, gain.reshape(1, d), w_up, w_down)


def _layer(x, mem, mixer_norm, w_in, fox_f_bias, mlstm_i_bias, mlstm_f_bias, conv_w, conv_b, fox_q_norm,
           fox_k_norm, fox_out_norm, mlstm_out_norm, w_out, xattn_norm, mem_norm, w_xq, w_xkv, xq_norm,
           xk_norm, w_xo, mlp_norm, w_up, w_down, *, batch, s_len, n_mem):
    fox_w = FOX_HEADS * FOX_HEAD_DIM
    mqk_w = 2 * MLSTM_HEADS * MLSTM_QK_DIM
    mv_w = MLSTM_HEADS * MLSTM_V_DIM
    d_model = x.shape[1]

    o_ff = 3 * fox_w
    o_mqk = o_ff + FOX_HEADS
    o_mv = o_mqk + mqk_w
    o_mi = o_mv + mv_w
    o_mo = o_mi + 2 * MLSTM_HEADS
    w_main, w_gates = _w_in_prep(w_in.T, o_ff=o_ff, o_mi=o_mi, tb=512)
    gate_bias = jnp.concatenate([fox_f_bias, mlstm_i_bias, mlstm_f_bias]).reshape(N_GATES, 1)

    proj, vt, gates_raw = _in_proj(x, mixer_norm, w_main, jnp.stack([fox_q_norm, fox_k_norm]), w_gates, tm=256)
    grow, gcol, ka = _gates(gates_raw, gate_bias, batch=batch, s_len=s_len)
    fo = _fox_attention(proj, vt, ka, grow, fox_out_norm, batch=batch, s_len=s_len)
    mh = _mlstm(proj, gcol, grow, conv_w, conv_b, mlstm_out_norm, batch=batch, s_len=s_len)
    x = _proj_residual([fo, mh], w_out, x, tm=512, tn=512)

    xd = d_model // XATTN_HEADS
    q = _norm_proj(x, xattn_norm, w_xq, xq_norm, tm=512,
                   post_scale=xd ** -0.5 * LOG2E)
    kv = _kv_proj(mem, mem_norm, w_xkv, xk_norm, n_norm_tiles=XATTN_HEADS)
    co = _xattn(q, kv, batch=batch, s_len=s_len, n_mem=n_mem, tq=1024)
    x = _proj_residual([co], w_xo, x, tm=512, tn=512)

    tm = 1024
    first, w_up_b, w_down_b = _mlp_first(x, mlp_norm, w_up, w_down, tm=tm, tf=512)
    x = lax.dynamic_update_slice(x, first, (0, 0))
    return _mlp(x, mlp_norm, w_up_b, w_down_b, tm=tm, tf=1024)


def kernel(x, mem, mixer_norm, w_in, fox_f_bias, mlstm_i_bias, mlstm_f_bias, conv_w, conv_b, fox_q_norm,
           fox_k_norm, fox_out_norm, mlstm_out_norm, w_out, xattn_norm, mem_norm, w_xq, w_xkv, xq_norm,
           xk_norm, w_xo, mlp_norm, w_up, w_down):
    batch, s_len, d_model = x.shape
    n_mem = mem.shape[1]
    depth = w_in.shape[0]
    h = x.reshape(batch * s_len, d_model)
    mem2 = mem.reshape(batch * n_mem, d_model)
    for l in range(depth):
        h = _layer(h, mem2, mixer_norm[l], w_in[l], fox_f_bias[l], mlstm_i_bias[l], mlstm_f_bias[l],
                   conv_w[l], conv_b[l], fox_q_norm[l], fox_k_norm[l], fox_out_norm[l], mlstm_out_norm[l],
                   w_out[l], xattn_norm[l], mem_norm[l], w_xq[l], w_xkv[l], xq_norm[l], xk_norm[l], w_xo[l],
                   mlp_norm[l], w_up[l], w_down[l], batch=batch, s_len=s_len, n_mem=n_mem)
    return h.reshape(batch, s_len, d_model)
```

```python
import functools

import jax
import jax.numpy as jnp
from jax import lax
from jax.experimental import pallas as pl
from jax.experimental.pallas import tpu as pltpu

F32 = jnp.float32
BF16 = jnp.bfloat16

EPS = 1e-6
FOX_HEADS = 8
FOX_HEAD_DIM = 128
MLSTM_HEADS = 4
MLSTM_QK_DIM = 128
MLSTM_V_DIM = 256
CONV_WIDTH = 4
XATTN_HEADS = 4
N_GATES = 16
LANES = 128
SUBLANES = 8
BF16_SUBLANES = 16
NEG_BIG = -1e30
LOG2E = 1.4426950408889634
FOX_AUG_DEPTH = 256
FOX_ONES_ROWS = 16

MLSTM_CHUNK = 256
FOX_BLOCK = 512
FOX_KEY_SUB = 256
IN_PROJ_TILE = 512
MLP_CHUNK = 256
VMEM_LIMIT = 56 * 1024 * 1024


def _cparams(sem):
    return pltpu.CompilerParams(dimension_semantics=sem, vmem_limit_bytes=VMEM_LIMIT)


def _rms_rows(x, gain):
    ms = jnp.mean(x * x, axis=-1, keepdims=True)
    return x * lax.rsqrt(ms + EPS) * gain


def _sigmoid(x):
    return 1.0 / (1.0 + jnp.exp(-x))


def _split3(x):
    hi = x.astype(BF16).astype(F32)
    r = x - hi
    mid = r.astype(BF16).astype(F32)
    lo = (r - mid).astype(BF16).astype(F32)
    return hi, mid, lo


def _resident(block_shape):
    return pl.BlockSpec(block_shape, lambda i: (0,) * len(block_shape), pipeline_mode=pl.Buffered(1))


def _tile_pipeline(n_tiles, matmul, epilogue):
    pending = matmul(0)
    for j in range(1, n_tiles):
        nxt = matmul(j)
        epilogue(j - 1, pending)
        pending = nxt
    epilogue(n_tiles - 1, pending)


def _w_in_prep_kernel(wt_ref, gf_ref, gm_ref, main_ref, gates_ref):
    main_ref[...] = wt_ref[...].T.astype(BF16)

    @pl.when(pl.program_id(0) == 0)
    def _():
        pad = jnp.zeros((LANES - N_GATES, gf_ref.shape[1]), F32)
        gates_ref[...] = jnp.concatenate([gf_ref[...], gm_ref[...], pad], axis=0).T.astype(BF16)


def _in_proj_kernel(x_ref, g_ref, w_ref, qk_gain_ref, wg_ref, out_ref, vt_ref, gates_ref):
    seg_w = FOX_HEADS * FOX_HEAD_DIM
    tn = IN_PROJ_TILE
    xn = _rms_rows(x_ref[...], g_ref[...]).astype(BF16)
    sources = (2 * seg_w, 0, seg_w, 3 * seg_w, 4 * seg_w, 5 * seg_w)

    def matmul(j):
        seg, off = divmod(j * tn, seg_w)
        col0 = sources[seg] + off
        return jnp.dot(xn, w_ref[:, col0:col0 + tn], preferred_element_type=F32)

    def head_norm(acc, gain, post_scale, col0):
        for h in range(tn // FOX_HEAD_DIM):
            sl = slice(h * FOX_HEAD_DIM, (h + 1) * FOX_HEAD_DIM)
            y = _rms_rows(acc[:, sl], gain) * post_scale
            out_ref[:, col0 + h * FOX_HEAD_DIM:col0 + (h + 1) * FOX_HEAD_DIM] = y.astype(out_ref.dtype)

    def epilogue(j, acc):
        seg, off = divmod(j * tn, seg_w)
        if seg == 0:
            vt_ref[off:off + tn, :] = acc.T.astype(vt_ref.dtype)
        elif seg == 1:
            head_norm(acc, qk_gain_ref[0], FOX_HEAD_DIM ** -0.5 * LOG2E, off)
        elif seg == 2:
            head_norm(acc, qk_gain_ref[1], 1.0, seg_w + off)
        else:
            out_ref[:, (seg - 1) * seg_w + off:(seg - 1) * seg_w + off + tn] = acc.astype(out_ref.dtype)

    gates_ref[...] = jnp.dot(xn, wg_ref[...], preferred_element_type=F32)
    _tile_pipeline(len(sources) * seg_w // tn, matmul, epilogue)


def _in_proj(x, gain, w, qk_gains, w_gates, *, tm):
    m, k = x.shape
    tn = FOX_HEADS * FOX_HEAD_DIM
    n = w.shape[1]
    return pl.pallas_call(
        _in_proj_kernel,
        grid=(m // tm,),
        in_specs=[
            pl.BlockSpec((tm, k), lambda i: (i, 0)),
            _resident((1, k)),
            _resident((k, n)),
            _resident((2, 1, FOX_HEAD_DIM)),
            _resident((k, LANES)),
        ],
        out_specs=[
            pl.BlockSpec((tm, n - tn), lambda i: (i, 0)),
            pl.BlockSpec((tn, tm), lambda i: (0, i)),
            pl.BlockSpec((tm, LANES), lambda i: (i, 0)),
        ],
        out_shape=[
            jax.ShapeDtypeStruct((m, n - tn), BF16),
            jax.ShapeDtypeStruct((tn, m), BF16),
            jax.ShapeDtypeStruct((m, LANES), F32),
        ],
        compiler_params=_cparams(("parallel",)),
        name="in_proj",
    )(x, gain.reshape(1, k), w, qk_gains.reshape(2, 1, FOX_HEAD_DIM), w_gates)


def _bf16_copy(w_ref, wb_ref):
    @pl.when(pl.program_id(0) == 0)
    def _():
        wb_ref[...] = w_ref[...].astype(BF16)
    return wb_ref


def _norm_proj_kernel(x_ref, g_ref, w_ref, hg_ref, out_ref, wb_ref, *, post_scale):
    tn = hg_ref.shape[1]
    _bf16_copy(w_ref, wb_ref)
    xn = _rms_rows(x_ref[...], g_ref[...]).astype(BF16)

    def matmul(j):
        return jnp.dot(xn, wb_ref[:, j * tn:(j + 1) * tn], preferred_element_type=F32)

    def epilogue(j, acc):
        out_ref[:, j * tn:(j + 1) * tn] = (_rms_rows(acc, hg_ref[...]) * post_scale).astype(out_ref.dtype)

    _tile_pipeline(w_ref.shape[1] // tn, matmul, epilogue)


def _norm_proj(x, gain, w, head_gain, *, tm, post_scale):
    m, k = x.shape
    n = w.shape[1]
    tn = head_gain.shape[0]
    return pl.pallas_call(
        functools.partial(_norm_proj_kernel, post_scale=post_scale),
        grid=(m // tm,),
        in_specs=[
            pl.BlockSpec((tm, k), lambda i: (i, 0)),
            _resident((1, k)),
            _resident((k, n)),
            _resident((1, tn)),
        ],
        out_specs=pl.BlockSpec((tm, n), lambda i: (i, 0)),
        out_shape=jax.ShapeDtypeStruct((m, n), BF16),
        scratch_shapes=[pltpu.VMEM((k, n), BF16)],
        compiler_params=_cparams(("arbitrary",)),
        name="norm_proj",
    )(x, gain.reshape(1, k), w, head_gain.reshape(1, tn))


def _kv_proj_kernel(x_ref, g_ref, w_ref, hg_ref, out_ref, xn_ref, *, n_norm_tiles):
    j = pl.program_id(0)

    @pl.when(j == 0)
    def _():
        xn_ref[...] = _rms_rows(x_ref[...], g_ref[...]).astype(BF16)

    acc = jnp.dot(xn_ref[...], w_ref[...].astype(BF16), preferred_element_type=F32)

    @pl.when(j < n_norm_tiles)
    def _():
        out_ref[...] = _rms_rows(acc, hg_ref[...]).astype(out_ref.dtype)

    @pl.when(j >= n_norm_tiles)
    def _():
        out_ref[...] = acc.astype(out_ref.dtype)


def _prep_kv_kernel(wt_ref, gf_ref, gm_ref, x_ref, g_ref, w_ref, hg_ref, main_ref, gates_ref, kv_ref, xn_ref, *,
                    n_kv_tiles, n_norm_tiles):
    _w_in_prep_kernel(wt_ref, gf_ref, gm_ref, main_ref, gates_ref)

    @pl.when(pl.program_id(0) < n_kv_tiles)
    def _():
        _kv_proj_kernel(x_ref, g_ref, w_ref, hg_ref, kv_ref, xn_ref, n_norm_tiles=n_norm_tiles)


def _prep_kv(wt, mem, mem_gain, w_xkv, k_gain, *, o_ff, o_mi, tb, n_norm_tiles):
    n_all, k = wt.shape
    n_main = n_all - N_GATES
    n_fox, n_before_mi = o_ff // tb, (o_mi - FOX_HEADS) // tb
    assert o_ff % tb == 0 and (o_mi - FOX_HEADS) % tb == 0 and n_main % tb == 0
    m = mem.shape[0]
    n_kv = w_xkv.shape[1]
    tn = k_gain.shape[0]
    n_kv_tiles = n_kv // tn
    assert n_kv_tiles <= n_main // tb

    def src_row(b):
        return b * tb + jnp.where(b >= n_fox, FOX_HEADS, 0) + jnp.where(b >= n_before_mi, 2 * MLSTM_HEADS, 0)

    def kv_tile(b):
        return jnp.minimum(b, n_kv_tiles - 1)

    return pl.pallas_call(
        functools.partial(_prep_kv_kernel, n_kv_tiles=n_kv_tiles, n_norm_tiles=n_norm_tiles),
        grid=(n_main // tb,),
        in_specs=[
            pl.BlockSpec((pl.Element(tb), pl.Element(k)), lambda b: (pl.multiple_of(src_row(b), SUBLANES), 0)),
            pl.BlockSpec((pl.Element(FOX_HEADS), pl.Element(k)), lambda b: (o_ff, 0)),
            pl.BlockSpec((pl.Element(2 * MLSTM_HEADS), pl.Element(k)), lambda b: (o_mi, 0)),
            _resident((m, k)),
            _resident((1, k)),
            pl.BlockSpec((k, tn), lambda b: (0, kv_tile(b))),
            _resident((1, tn)),
        ],
        out_specs=[
            pl.BlockSpec((k, tb), lambda b: (0, b)),
            pl.BlockSpec((k, LANES), lambda b: (0, 0)),
            pl.BlockSpec((m, tn), lambda b: (0, kv_tile(b))),
        ],
        out_shape=[
            jax.ShapeDtypeStruct((k, n_main), BF16),
            jax.ShapeDtypeStruct((k, LANES), BF16),
            jax.ShapeDtypeStruct((m, n_kv), BF16),
        ],
        scratch_shapes=[pltpu.VMEM((m, k), BF16)],
        compiler_params=_cparams(("arbitrary",)),
        name="prep_kv",
    )(wt, wt, wt, mem, mem_gain.reshape(1, k), w_xkv, k_gain.reshape(1, tn))


def _proj_residual_kernel(*refs, n_lhs, tn):
    lhs = refs[:n_lhs]
    w_ref, res_ref, out_ref, wb_ref = refs[n_lhs:]
    kp = lhs[0].shape[1]
    _bf16_copy(w_ref, wb_ref)

    def matmul(j):
        cols = slice(j * tn, (j + 1) * tn)
        acc = jnp.dot(lhs[0][...], wb_ref[0:kp, cols], preferred_element_type=F32)
        for p in range(1, n_lhs):
            acc = acc + jnp.dot(lhs[p][...], wb_ref[p * kp:(p + 1) * kp, cols], preferred_element_type=F32)
        return acc

    def epilogue(j, acc):
        cols = slice(j * tn, (j + 1) * tn)
        out_ref[:, cols] = res_ref[:, cols] + acc

    _tile_pipeline(w_ref.shape[1] // tn, matmul, epilogue)


def _proj_residual(lhs_list, w, residual, *, tm, tn):
    m, n = residual.shape
    n_lhs = len(lhs_list)
    kp = lhs_list[0].shape[1]
    in_specs = [pl.BlockSpec((tm, kp), lambda i: (i, 0)) for _ in lhs_list]
    in_specs += [_resident(w.shape), pl.BlockSpec((tm, n), lambda i: (i, 0))]
    return pl.pallas_call(
        functools.partial(_proj_residual_kernel, n_lhs=n_lhs, tn=tn),
        grid=(m // tm,),
        in_specs=in_specs,
        out_specs=pl.BlockSpec((tm, n), lambda i: (i, 0)),
        out_shape=jax.ShapeDtypeStruct((m, n), F32),
        scratch_shapes=[pltpu.VMEM(w.shape, BF16)],
        compiler_params=_cparams(("arbitrary",)),
        name="proj_residual",
    )(*lhs_list, w, residual)


def _gates_kernel(g_ref, bias_ref, row_ref, col_ref, ka_ref):
    s_len = g_ref.shape[0]
    z = g_ref[...].T[0:N_GATES, :] + bias_ref[...]
    row = lax.broadcasted_iota(jnp.int32, z.shape, 0)
    is_forget = (row < FOX_HEADS) | (row >= FOX_HEADS + MLSTM_HEADS)
    logsig = jnp.minimum(z, 0.0) - jnp.log1p(jnp.exp(-jnp.abs(z)))
    c = jnp.where(is_forget, logsig, 0.0)
    lane = lax.broadcasted_iota(jnp.int32, z.shape, 1)
    shift = 1
    while shift < s_len:
        c = c + jnp.where(lane >= shift, pltpu.roll(c, shift, axis=1), 0.0)
        shift *= 2
    vals = jnp.where(is_forget, c, z)
    row_ref[...] = vals
    padded = jnp.concatenate([vals, jnp.zeros((LANES - N_GATES, s_len), F32)], axis=0)
    col_ref[...] = padded.T
    hi, mid, lo = _split3(c[0:FOX_HEADS, :] * LOG2E)
    aug = jnp.concatenate([jnp.ones((FOX_HEADS, s_len), F32), -hi, -mid, -lo,
                           jnp.zeros((LANES - 4 * FOX_HEADS, s_len), F32)], axis=0)
    ka_ref[...] = aug.T.astype(ka_ref.dtype)


def _gates(gates_raw, bias, *, batch, s_len):
    return pl.pallas_call(
        _gates_kernel,
        grid=(batch,),
        in_specs=[pl.BlockSpec((s_len, LANES), lambda b: (b, 0)),
                  pl.BlockSpec((N_GATES, 1), lambda b: (0, 0))],
        out_specs=[pl.BlockSpec((None, N_GATES, s_len), lambda b: (b, 0, 0)),
                   pl.BlockSpec((s_len, LANES), lambda b: (b, 0)),
                   pl.BlockSpec((s_len, LANES), lambda b: (b, 0))],
        out_shape=[jax.ShapeDtypeStruct((batch, N_GATES, s_len), F32),
                   jax.ShapeDtypeStruct((batch * s_len, LANES), F32),
                   jax.ShapeDtypeStruct((batch * s_len, LANES), BF16)],
        compiler_params=_cparams(("parallel",)),
        name="gates",
    )(gates_raw, bias)


def _fox_kernel(qi_ref, ki_ref, q_ref, k_ref, vt_ref, ka_ref, grow_ref, on_ref, o_ref, qt_ref, m_ref, acc_ref):
    pair = pl.program_id(1)
    qi = qi_ref[pair]
    ki = ki_ref[pair]
    tq, tk = q_ref.shape[0], k_ref.shape[0]
    d = FOX_HEAD_DIM

    @pl.when(ki == 0)
    def _():
        m_ref[...] = jnp.full(m_ref.shape, NEG_BIG, F32)
        acc_ref[...] = jnp.zeros(acc_ref.shape, F32)
        sub = lax.broadcasted_iota(jnp.int32, (SUBLANES, tq), 0)
        for h in range(FOX_HEADS):
            hi, mid, lo = _split3(grow_ref[h:h + 1, :] * LOG2E)
            c_rows = jnp.where(sub == 0, hi, jnp.where(sub == 1, mid, jnp.where(sub == 2, lo, 0.0)))
            pick = jnp.where(sub == h, 1.0, 0.0)
            qt = q_ref[:, h * d:(h + 1) * d].astype(F32).T
            pad = jnp.zeros((FOX_AUG_DEPTH - d - 4 * SUBLANES, tq), F32)
            qt_ref[h] = jnp.concatenate([qt, c_rows, pick, pick, pick, pad], axis=0).astype(BF16)

    def step(diagonal):
        sub = FOX_KEY_SUB
        ones = jnp.ones((FOX_ONES_ROWS, sub), BF16)
        if diagonal:
            keep = (lax.broadcasted_iota(jnp.int32, (sub, tq), 0)
                    <= lax.broadcasted_iota(jnp.int32, (sub, tq), 1))

        def scores(h, ks):
            t0 = ks * sub if diagonal else 0
            rows = slice(ks * sub, (ks + 1) * sub)
            k_aug = jnp.concatenate([k_ref[rows, h * d:(h + 1) * d], ka_ref[rows, :]], axis=1)
            st = jnp.dot(k_aug, qt_ref[h, :, t0:], preferred_element_type=F32)
            if diagonal:
                st = jnp.where(keep[:, 0:tq - t0], st, NEG_BIG)
            m_prev = m_ref[h:h + 1, t0:]
            m_new = jnp.maximum(m_prev, jnp.max(st, axis=0, keepdims=True))
            m_ref[h:h + 1, t0:] = m_new
            return t0, st, m_new, jnp.exp2(m_prev - m_new)

        def accumulate(h, ks, t0, st, m_new, alpha):
            p = jnp.exp2(st - m_new)
            vt_aug = jnp.concatenate([vt_ref[h * d:(h + 1) * d, ks * sub:(ks + 1) * sub], ones], axis=0)
            acc_ref[h, :, t0:] = alpha * acc_ref[h, :, t0:] + jnp.dot(vt_aug, p.astype(BF16),
                                                                      preferred_element_type=F32)

        items = [(h, ks) for h in range(FOX_HEADS) for ks in range(tk // sub)]
        lookahead = 2
        pending = [scores(*it) for it in items[:lookahead]]
        for n, it in enumerate(items):
            if n + lookahead < len(items):
                pending.append(scores(*items[n + lookahead]))
            accumulate(*it, *pending.pop(0))

    @pl.when(ki < qi)
    def _():
        step(False)

    @pl.when(ki == qi)
    def _():
        step(True)
        for h in range(FOX_HEADS):
            sl = slice(h * d, (h + 1) * d)
            o = (acc_ref[h, 0:d, :] / acc_ref[h, d:d + 1, :]).T
            o_ref[:, sl] = _rms_rows(o, on_ref[:, sl]).astype(o_ref.dtype)


def _fox_attention(proj, vt, ka, grow, out_norm, *, batch, s_len):
    t = FOX_BLOCK
    nq = s_len // t
    width = FOX_HEADS * FOX_HEAD_DIM
    pairs = [(qi, ki) for qi in range(nq) for ki in range(qi + 1)]
    qi_tab = jnp.asarray([p[0] for p in pairs], jnp.int32)
    ki_tab = jnp.asarray([p[1] for p in pairs], jnp.int32)

    def q_row(b, p, qi_tab, ki_tab):
        return b * nq + qi_tab[p]

    def k_row(b, p, qi_tab, ki_tab):
        return b * nq + ki_tab[p]

    grid_spec = pltpu.PrefetchScalarGridSpec(
        num_scalar_prefetch=2,
        grid=(batch, len(pairs)),
        in_specs=[
            pl.BlockSpec((t, width), lambda *a: (q_row(*a), 0)),
            pl.BlockSpec((t, width), lambda *a: (k_row(*a), 1)),
            pl.BlockSpec((width, t), lambda *a: (0, k_row(*a))),
            pl.BlockSpec((t, LANES), lambda *a: (k_row(*a), 0)),
            pl.BlockSpec((None, N_GATES, t), lambda b, p, qi_tab, ki_tab: (b, 0, qi_tab[p])),
            pl.BlockSpec((1, width), lambda *a: (0, 0)),
        ],
        out_specs=pl.BlockSpec((t, width), lambda *a: (q_row(*a), 0)),
        scratch_shapes=[
            pltpu.VMEM((FOX_HEADS, FOX_AUG_DEPTH, t), BF16),
            pltpu.VMEM((FOX_HEADS, t), F32),
            pltpu.VMEM((FOX_HEADS, FOX_HEAD_DIM + FOX_ONES_ROWS, t), F32),
        ],
    )
    return pl.pallas_call(
        _fox_kernel,
        grid_spec=grid_spec,
        out_shape=jax.ShapeDtypeStruct((batch * s_len, width), BF16),
        compiler_params=_cparams(("parallel", "arbitrary")),
        name="fox_attention",
    )(qi_tab, ki_tab, proj, proj, vt, ka, grow, out_norm.reshape(1, width))


def _mlstm_kernel(mqk_ref, mv_ref, mo_ref, gcol_ref, grow_ref, cw_ref, cb_ref, on_ref, out_ref,
                  ct_ref, mg_ref, ext_ref, shift_ref, qk_ref):
    c = pl.program_id(1)
    length = mqk_ref.shape[0]
    dk, dv = MLSTM_QK_DIM, MLSTM_V_DIM
    n_qk = MLSTM_HEADS * dk
    head = ext_ref.shape[0] - length
    tail = BF16_SUBLANES

    @pl.when(c == 0)
    def _():
        ct_ref[...] = jnp.zeros(ct_ref.shape, F32)
        mg_ref[...] = jnp.zeros(mg_ref.shape, F32)
        ext_ref[0:head, :] = jnp.zeros((head, ext_ref.shape[1]), BF16)
        row = lax.broadcasted_iota(jnp.int32, shift_ref.shape[1:], 0)
        col = lax.broadcasted_iota(jnp.int32, shift_ref.shape[1:], 1)
        for s in range(1, CONV_WIDTH):
            shift_ref[s - 1] = jnp.where(col == row + head - s, 1.0, 0.0).astype(BF16)

    u = mqk_ref[...]
    ext_ref[head:, :] = u
    ext = ext_ref[...]
    y = cb_ref[...] + cw_ref[CONV_WIDTH - 1:CONV_WIDTH, :] * u.astype(F32)
    for s in range(1, CONV_WIDTH):
        tap = CONV_WIDTH - 1 - s
        y = y + cw_ref[tap:tap + 1, :] * jnp.dot(shift_ref[s - 1], ext, preferred_element_type=F32)
    ext_ref[head - tail:head, :] = u[length - tail:, :]
    y = y * _sigmoid(y)
    qk_ref[:, 0:n_qk] = y[:, 0:n_qk].astype(BF16)
    qk_ref[:, n_qk:] = (y[:, n_qk:] * dk ** -0.5).astype(BF16)

    keep = (lax.broadcasted_iota(jnp.int32, (length, length), 0)
            >= lax.broadcasted_iota(jnp.int32, (length, length), 1))
    ones = jnp.ones((length, LANES), BF16)

    def scores(h):
        qb = qk_ref[:, h * dk:(h + 1) * dk]
        kb = qk_ref[:, n_qk + h * dk:n_qk + (h + 1) * dk]
        gi, gf = FOX_HEADS + h, FOX_HEADS + MLSTM_HEADS + h
        u_row = grow_ref[gi:gi + 1, :] - grow_ref[gf:gf + 1, :]
        mg = mg_ref[h:h + 1, 0:1]
        um = jnp.where(keep, u_row, NEG_BIG)
        a = jnp.maximum(jnp.max(um, axis=-1, keepdims=True), mg)
        sqk = lax.dot_general(qb, kb, (((1,), (1,)), ((), ())), preferred_element_type=F32) * jnp.exp(um - a)
        mg_new = jnp.maximum(mg, jnp.max(u_row, axis=-1, keepdims=True))
        return sqk, a, mg, mg_new

    def outputs(h, sqk, a, mg, mg_new):
        qb = qk_ref[:, h * dk:(h + 1) * dk]
        kb = qk_ref[:, n_qk + h * dk:n_qk + (h + 1) * dk]
        sl = slice(h * dv, (h + 1) * dv)
        v_aug = jnp.concatenate([mv_ref[:, sl], ones], axis=1)
        gi, gf = FOX_HEADS + h, FOX_HEADS + MLSTM_HEADS + h
        f_col = gcol_ref[:, gf:gf + 1]
        u_col = gcol_ref[:, gi:gi + 1] - f_col
        ct = ct_ref[h]
        inter = jnp.dot(qb, ct.astype(BF16), preferred_element_type=F32)
        nd = jnp.dot(sqk.astype(BF16), v_aug, preferred_element_type=F32) + jnp.exp(mg - a) * inter
        den = jnp.maximum(jnp.abs(nd[:, dv:dv + 1]), jnp.exp(-(f_col + a)))
        hid = nd[:, :dv] * (1.0 / den)
        gate = _sigmoid(mo_ref[:, sl].astype(F32))
        out_ref[:, sl] = (_rms_rows(hid, on_ref[:, sl]) * gate).astype(out_ref.dtype)

        wav = (jnp.exp(u_col - mg_new) * v_aug.astype(F32)).astype(BF16)
        kt = kb.astype(F32).T.astype(BF16)
        ct_ref[h] = jnp.exp(mg - mg_new) * ct + jnp.dot(kt, wav, preferred_element_type=F32)
        mg_ref[h:h + 1, :] = jnp.broadcast_to(mg_new, (1, LANES))

    pending = scores(0)
    for h in range(1, MLSTM_HEADS):
        nxt = scores(h)
        outputs(h - 1, *pending)
        pending = nxt
    outputs(MLSTM_HEADS - 1, *pending)


def _mlstm(proj, gcol, grow, conv_w, conv_b, out_norm, *, batch, s_len):
    length = MLSTM_CHUNK
    nc = s_len // length
    width = MLSTM_HEADS * MLSTM_V_DIM
    qk_width = 2 * MLSTM_HEADS * MLSTM_QK_DIM
    return pl.pallas_call(
        _mlstm_kernel,
        grid=(batch, nc),
        in_specs=[
            pl.BlockSpec((length, qk_width), lambda b, c: (b * nc + c, 2)),
            pl.BlockSpec((length, width), lambda b, c: (b * nc + c, 3)),
            pl.BlockSpec((length, width), lambda b, c: (b * nc + c, 4)),
            pl.BlockSpec((length, LANES), lambda b, c: (b * nc + c, 0)),
            pl.BlockSpec((None, N_GATES, length), lambda b, c: (b, 0, c)),
            pl.BlockSpec((CONV_WIDTH, qk_width), lambda b, c: (0, 0)),
            pl.BlockSpec((1, qk_width), lambda b, c: (0, 0)),
            pl.BlockSpec((1, width), lambda b, c: (0, 0)),
        ],
        out_specs=pl.BlockSpec((length, width), lambda b, c: (b * nc + c, 0)),
        out_shape=jax.ShapeDtypeStruct((batch * s_len, width), BF16),
        scratch_shapes=[
            pltpu.VMEM((MLSTM_HEADS, MLSTM_QK_DIM, MLSTM_V_DIM + LANES), F32),
            pltpu.VMEM((SUBLANES, LANES), F32),
            pltpu.VMEM((LANES + length, qk_width), BF16),
            pltpu.VMEM((CONV_WIDTH - 1, length, LANES + length), BF16),
            pltpu.VMEM((length, qk_width), BF16),
        ],
        compiler_params=_cparams(("parallel", "arbitrary")),
        name="mlstm",
    )(proj, proj, proj, gcol, grow, conv_w, conv_b.reshape(1, qk_width), out_norm.reshape(1, width))


def _xattn_kernel(q_ref, k_ref, v_ref, o_ref):
    d = q_ref.shape[1] // XATTN_HEADS

    def scores(h):
        sl = slice(h * d, (h + 1) * d)
        return lax.dot_general(q_ref[:, sl], k_ref[:, sl], (((1,), (1,)), ((), ())), preferred_element_type=F32)

    def outputs(h, s):
        sl = slice(h * d, (h + 1) * d)
        p = jnp.exp2(s - jnp.max(s, axis=-1, keepdims=True))
        p = p * (1.0 / jnp.sum(p, axis=-1, keepdims=True))
        o_ref[:, sl] = jnp.dot(p.astype(BF16), v_ref[:, sl], preferred_element_type=F32).astype(o_ref.dtype)

    _tile_pipeline(XATTN_HEADS, scores, outputs)


def _xattn(q, kv, *, batch, s_len, n_mem, tq):
    d_model = q.shape[1]
    nq = s_len // tq
    return pl.pallas_call(
        _xattn_kernel,
        grid=(batch, nq),
        in_specs=[
            pl.BlockSpec((tq, d_model), lambda b, i: (b * nq + i, 0)),
            pl.BlockSpec((n_mem, d_model), lambda b, i: (b, 0)),
            pl.BlockSpec((n_mem, d_model), lambda b, i: (b, 1)),
        ],
        out_specs=pl.BlockSpec((tq, d_model), lambda b, i: (b * nq + i, 0)),
        out_shape=jax.ShapeDtypeStruct(q.shape, BF16),
        compiler_params=_cparams(("parallel", "parallel")),
        name="xattn",
    )(q, kv, kv)


def _mlp_first_kernel(x_ref, g_ref, wu_ref, wd_ref, o_ref, wub_ref, wdb_ref, xn_ref):
    f = pl.program_id(0)

    @pl.when(f == 0)
    def _():
        x = x_ref[...]
        xn_ref[...] = _rms_rows(x, g_ref[...]).astype(BF16)
        o_ref[...] = x

    wu = wu_ref[...].astype(BF16)
    wd = wd_ref[...].astype(BF16)
    wub_ref[...] = wu
    wdb_ref[...] = wd
    hid = jnp.dot(xn_ref[...], wu, preferred_element_type=F32)
    hid = jnp.square(jnp.maximum(hid, 0.0)).astype(BF16)
    o_ref[...] += jnp.dot(hid, wd, preferred_element_type=F32)


def _mlp_first(x, gain, w_up, w_down, *, tm, tf):
    d = x.shape[1]
    d_ff = w_up.shape[1]
    return pl.pallas_call(
        _mlp_first_kernel,
        grid=(d_ff // tf,),
        in_specs=[
            _resident((tm, d)),
            _resident((1, d)),
            pl.BlockSpec((d, tf), lambda f: (0, f)),
            pl.BlockSpec((tf, d), lambda f: (f, 0)),
        ],
        out_specs=[
            pl.BlockSpec((tm, d), lambda f: (0, 0)),
            pl.BlockSpec((d, tf), lambda f: (0, f)),
            pl.BlockSpec((tf, d), lambda f: (f, 0)),
        ],
        out_shape=[
            jax.ShapeDtypeStruct((tm, d), F32),
            jax.ShapeDtypeStruct((d, d_ff), BF16),
            jax.ShapeDtypeStruct((d_ff, d), BF16),
        ],
        scratch_shapes=[pltpu.VMEM((tm, d), BF16)],
        compiler_params=_cparams(("arbitrary",)),
        name="mlp_first",
    )(x, gain.reshape(1, d), w_up, w_down)


def _mlp_kernel(x_ref, g_ref, wu_ref, wd_ref, o_ref, xn_ref):
    i = pl.program_id(0)
    f = pl.program_id(1)

    @pl.when(f == 0)
    def _():
        x = x_ref[...]
        xn_ref[...] = _rms_rows(x, g_ref[...]).astype(BF16)
        o_ref[...] = x

    tc = MLP_CHUNK

    def up(j):
        return jnp.dot(xn_ref[...], wu_ref[:, j * tc:(j + 1) * tc], preferred_element_type=F32)

    def down(j, hid):
        hid = jnp.square(jnp.maximum(hid, 0.0)).astype(BF16)
        o_ref[...] += jnp.dot(hid, wd_ref[j * tc:(j + 1) * tc, :], preferred_element_type=F32)

    @pl.when(i > 0)
    def _():
        _tile_pipeline(wu_ref.shape[1] // tc, up, down)


def _mlp(x, gain, w_up, w_down, *, tm, tf):
    m, d = x.shape
    d_ff = w_up.shape[1]

    def tile(i, f):
        return jnp.where(i == 0, 0, f)

    return pl.pallas_call(
        _mlp_kernel,
        grid=(m // tm, d_ff // tf),
        in_specs=[
            pl.BlockSpec((tm, d), lambda i, f: (i, 0)),
            pl.BlockSpec((1, d), lambda i, f: (0, 0)),
            pl.BlockSpec((d, tf), lambda i, f: (0, tile(i, f))),
            pl.BlockSpec((tf, d), lambda i, f: (tile(i, f), 0)),
        ],
        out_specs=pl.BlockSpec((tm, d), lambda i, f: (i, 0)),
        out_shape=jax.ShapeDtypeStruct((m, d), F32),
        scratch_shapes=[pltpu.VMEM((tm, d), BF16)],
        compiler_params=_cparams(("parallel", "arbitrary")),
        name="mlp",
    )(x, gain.reshape(1, d), w_up, w_down)


def _layer(x, mem, mixer_norm, w_in, fox_f_bias, mlstm_i_bias, mlstm_f_bias, conv_w, conv_b, fox_q_norm,
           fox_k_norm, fox_out_norm, mlstm_out_norm, w_out, xattn_norm, mem_norm, w_xq, w_xkv, xq_norm,
           xk_norm, w_xo, mlp_norm, w_up, w_down, *, batch, s_len, n_mem):
    fox_w = FOX_HEADS * FOX_HEAD_DIM
    mqk_w = 2 * MLSTM_HEADS * MLSTM_QK_DIM
    mv_w = MLSTM_HEADS * MLSTM_V_DIM
    d_model = x.shape[1]

    o_ff = 3 * fox_w
    o_mqk = o_ff + FOX_HEADS
    o_mv = o_mqk + mqk_w
    o_mi = o_mv + mv_w
    o_mo = o_mi + 2 * MLSTM_HEADS
    w_main, w_gates, kv = _prep_kv(w_in.T, mem, mem_norm, w_xkv, xk_norm, o_ff=o_ff, o_mi=o_mi, tb=512,
                                   n_norm_tiles=XATTN_HEADS)
    gate_bias = jnp.concatenate([fox_f_bias, mlstm_i_bias, mlstm_f_bias]).reshape(N_GATES, 1)

    proj, vt, gates_raw = _in_proj(x, mixer_norm, w_main, jnp.stack([fox_q_norm, fox_k_norm]), w_gates, tm=256)
    grow, gcol, ka = _gates(gates_raw, gate_bias, batch=batch, s_len=s_len)
    fo = _fox_attention(proj, vt, ka, grow, fox_out_norm, batch=batch, s_len=s_len)
    mh = _mlstm(proj, gcol, grow, conv_w, conv_b, mlstm_out_norm, batch=batch, s_len=s_len)
    x = _proj_residual([fo, mh], w_out, x, tm=512, tn=512)

    xd = d_model // XATTN_HEADS
    q = _norm_proj(x, xattn_norm, w_xq, xq_norm, tm=512,
                   post_scale=xd ** -0.5 * LOG2E)
    co = _xattn(q, kv, batch=batch, s_len=s_len, n_mem=n_mem, tq=1024)
    x = _proj_residual([co], w_xo, x, tm=512, tn=512)

    tm = 1024
    first, w_up_b, w_down_b = _mlp_first(x, mlp_norm, w_up, w_down, tm=tm, tf=512)
    x = lax.dynamic_update_slice(x, first, (0, 0))
    return _mlp(x, mlp_norm, w_up_b, w_down_b, tm=tm, tf=1024)


def kernel(x, mem, mixer_norm, w_in, fox_f_bias, mlstm_i_bias, mlstm_f_bias, conv_w, conv_b, fox_q_norm,
           fox_k_norm, fox_out_norm, mlstm_out_norm, w_out, xattn_norm, mem_norm, w_xq, w_xkv, xq_norm,
           xk_norm, w_xo, mlp_norm, w_up, w_down):
    batch, s_len, d_model = x.shape
    n_mem = mem.shape[1]
    depth = w_in.shape[0]
    h = x.reshape(batch * s_len, d_model)
    mem2 = mem.reshape(batch * n_mem, d_model)
    for l in range(depth):
        h = _layer(h, mem2, mixer_norm[l], w_in[l], fox_f_bias[l], mlstm_i_bias[l], mlstm_f_bias[l],
                   conv_w[l], conv_b[l], fox_q_norm[l], fox_k_norm[l], fox_out_norm[l], mlstm_out_norm[l],
                   w_out[l], xattn_norm[l], mem_norm[l], w_xq[l], w_xkv[l], xq_norm[l], xk_norm[l], w_xo[l],
                   mlp_norm[l], w_up[l], w_down[l], batch=batch, s_len=s_len, n_mem=n_mem)
    return h.reshape(batch, s_len, d_model)
```

```python
import functools

import jax
import jax.numpy as jnp
from jax import lax
from jax.experimental import pallas as pl
from jax.experimental.pallas import tpu as pltpu

F32 = jnp.float32
BF16 = jnp.bfloat16

EPS = 1e-6
FOX_HEADS = 8
FOX_HEAD_DIM = 128
MLSTM_HEADS = 4
MLSTM_QK_DIM = 128
MLSTM_V_DIM = 256
CONV_WIDTH = 4
XATTN_HEADS = 4
N_GATES = 16
LANES = 128
SUBLANES = 8
BF16_SUBLANES = 16
NEG_BIG = -1e30
LOG2E = 1.4426950408889634
FOX_AUG_DEPTH = 256
FOX_ONES_ROWS = 16

MLSTM_CHUNK = 256
FOX_BLOCK = 512
FOX_KEY_SUB = 256
IN_PROJ_TILE = 512
MLP_CHUNK = 256
VMEM_LIMIT = 56 * 1024 * 1024


def _cparams(sem):
    return pltpu.CompilerParams(dimension_semantics=sem, vmem_limit_bytes=VMEM_LIMIT)


def _rms_rows(x, gain):
    ms = jnp.mean(x * x, axis=-1, keepdims=True)
    return x * lax.rsqrt(ms + EPS) * gain


def _sigmoid(x):
    return 1.0 / (1.0 + jnp.exp(-x))


def _split3(x):
    hi = x.astype(BF16).astype(F32)
    r = x - hi
    mid = r.astype(BF16).astype(F32)
    lo = (r - mid).astype(BF16).astype(F32)
    return hi, mid, lo


def _resident(block_shape):
    return pl.BlockSpec(block_shape, lambda i: (0,) * len(block_shape), pipeline_mode=pl.Buffered(1))


def _tile_pipeline(n_tiles, matmul, epilogue):
    pending = matmul(0)
    for j in range(1, n_tiles):
        nxt = matmul(j)
        epilogue(j - 1, pending)
        pending = nxt
    epilogue(n_tiles - 1, pending)


def _w_in_prep_kernel(wt_ref, gf_ref, gm_ref, main_ref, gates_ref):
    main_ref[...] = wt_ref[...].T.astype(BF16)

    @pl.when(pl.program_id(0) == 0)
    def _():
        pad = jnp.zeros((LANES - N_GATES, gf_ref.shape[1]), F32)
        gates_ref[...] = jnp.concatenate([gf_ref[...], gm_ref[...], pad], axis=0).T.astype(BF16)


def _w_in_prep(wt, *, o_ff, o_mi, tb):
    n_all, k = wt.shape
    n_main = n_all - N_GATES
    n_fox, n_before_mi = o_ff // tb, (o_mi - FOX_HEADS) // tb
    assert o_ff % tb == 0 and (o_mi - FOX_HEADS) % tb == 0 and n_main % tb == 0

    def src_row(b):
        return b * tb + jnp.where(b >= n_fox, FOX_HEADS, 0) + jnp.where(b >= n_before_mi, 2 * MLSTM_HEADS, 0)

    return pl.pallas_call(
        _w_in_prep_kernel,
        grid=(n_main // tb,),
        in_specs=[
            pl.BlockSpec((pl.Element(tb), pl.Element(k)), lambda b: (pl.multiple_of(src_row(b), SUBLANES), 0)),
            pl.BlockSpec((pl.Element(FOX_HEADS), pl.Element(k)), lambda b: (o_ff, 0)),
            pl.BlockSpec((pl.Element(2 * MLSTM_HEADS), pl.Element(k)), lambda b: (o_mi, 0)),
        ],
        out_specs=[pl.BlockSpec((k, tb), lambda b: (0, b)), pl.BlockSpec((k, LANES), lambda b: (0, 0))],
        out_shape=[jax.ShapeDtypeStruct((k, n_main), BF16), jax.ShapeDtypeStruct((k, LANES), BF16)],
        compiler_params=_cparams(("arbitrary",)),
        name="w_in_prep",
    )(wt, wt, wt)


def _in_proj_kernel(x_ref, g_ref, w_ref, qk_gain_ref, wg_ref, out_ref, vt_ref, gates_ref):
    seg_w = FOX_HEADS * FOX_HEAD_DIM
    tn = IN_PROJ_TILE
    xn = _rms_rows(x_ref[...], g_ref[...]).astype(BF16)
    sources = (2 * seg_w, 0, seg_w, 3 * seg_w, 4 * seg_w, 5 * seg_w)

    def matmul(j):
        seg, off = divmod(j * tn, seg_w)
        col0 = sources[seg] + off
        return jnp.dot(xn, w_ref[:, col0:col0 + tn], preferred_element_type=F32)

    def head_norm(acc, gain, post_scale, col0):
        for h in range(tn // FOX_HEAD_DIM):
            sl = slice(h * FOX_HEAD_DIM, (h + 1) * FOX_HEAD_DIM)
            y = _rms_rows(acc[:, sl], gain) * post_scale
            out_ref[:, col0 + h * FOX_HEAD_DIM:col0 + (h + 1) * FOX_HEAD_DIM] = y.astype(out_ref.dtype)

    def epilogue(j, acc):
        seg, off = divmod(j * tn, seg_w)
        if seg == 0:
            vt_ref[off:off + tn, :] = acc.T.astype(vt_ref.dtype)
        elif seg == 1:
            head_norm(acc, qk_gain_ref[0], FOX_HEAD_DIM ** -0.5 * LOG2E, off)
        elif seg == 2:
            head_norm(acc, qk_gain_ref[1], 1.0, seg_w + off)
        else:
            out_ref[:, (seg - 1) * seg_w + off:(seg - 1) * seg_w + off + tn] = acc.astype(out_ref.dtype)

    gates_ref[...] = jnp.dot(xn, wg_ref[...], preferred_element_type=F32)
    _tile_pipeline(len(sources) * seg_w // tn, matmul, epilogue)


def _in_proj(x, gain, w, qk_gains, w_gates, *, tm):
    m, k = x.shape
    tn = FOX_HEADS * FOX_HEAD_DIM
    n = w.shape[1]
    return pl.pallas_call(
        _in_proj_kernel,
        grid=(m // tm,),
        in_specs=[
            pl.BlockSpec((tm, k), lambda i: (i, 0)),
            _resident((1, k)),
            _resident((k, n)),
            _resident((2, 1, FOX_HEAD_DIM)),
            _resident((k, LANES)),
        ],
        out_specs=[
            pl.BlockSpec((tm, n - tn), lambda i: (i, 0)),
            pl.BlockSpec((tn, tm), lambda i: (0, i)),
            pl.BlockSpec((tm, LANES), lambda i: (i, 0)),
        ],
        out_shape=[
            jax.ShapeDtypeStruct((m, n - tn), BF16),
            jax.ShapeDtypeStruct((tn, m), BF16),
            jax.ShapeDtypeStruct((m, LANES), F32),
        ],
        compiler_params=_cparams(("parallel",)),
        name="in_proj",
    )(x, gain.reshape(1, k), w, qk_gains.reshape(2, 1, FOX_HEAD_DIM), w_gates)


def _bf16_copy(w_ref, wb_ref):
    @pl.when(pl.program_id(0) == 0)
    def _():
        wb_ref[...] = w_ref[...].astype(BF16)
    return wb_ref


def _norm_proj_kernel(x_ref, g_ref, w_ref, hg_ref, out_ref, wb_ref, *, post_scale):
    tn = hg_ref.shape[1]
    _bf16_copy(w_ref, wb_ref)
    xn = _rms_rows(x_ref[...], g_ref[...]).astype(BF16)

    def matmul(j):
        return jnp.dot(xn, wb_ref[:, j * tn:(j + 1) * tn], preferred_element_type=F32)

    def epilogue(j, acc):
        out_ref[:, j * tn:(j + 1) * tn] = (_rms_rows(acc, hg_ref[...]) * post_scale).astype(out_ref.dtype)

    _tile_pipeline(w_ref.shape[1] // tn, matmul, epilogue)


def _norm_proj(x, gain, w, head_gain, *, tm, post_scale):
    m, k = x.shape
    n = w.shape[1]
    tn = head_gain.shape[0]
    return pl.pallas_call(
        functools.partial(_norm_proj_kernel, post_scale=post_scale),
        grid=(m // tm,),
        in_specs=[
            pl.BlockSpec((tm, k), lambda i: (i, 0)),
            _resident((1, k)),
            _resident((k, n)),
            _resident((1, tn)),
        ],
        out_specs=pl.BlockSpec((tm, n), lambda i: (i, 0)),
        out_shape=jax.ShapeDtypeStruct((m, n), BF16),
        scratch_shapes=[pltpu.VMEM((k, n), BF16)],
        compiler_params=_cparams(("arbitrary",)),
        name="norm_proj",
    )(x, gain.reshape(1, k), w, head_gain.reshape(1, tn))


def _kv_proj_kernel(x_ref, g_ref, w_ref, hg_ref, out_ref, xn_ref, *, n_norm_tiles):
    j = pl.program_id(0)

    @pl.when(j == 0)
    def _():
        xn_ref[...] = _rms_rows(x_ref[...], g_ref[...]).astype(BF16)

    acc = jnp.dot(xn_ref[...], w_ref[...].astype(BF16), preferred_element_type=F32)

    @pl.when(j < n_norm_tiles)
    def _():
        out_ref[...] = _rms_rows(acc, hg_ref[...]).astype(out_ref.dtype)

    @pl.when(j >= n_norm_tiles)
    def _():
        out_ref[...] = acc.astype(out_ref.dtype)


def _kv_proj(x, gain, w, head_gain, *, n_norm_tiles):
    m, k = x.shape
    n = w.shape[1]
    tn = head_gain.shape[0]
    return pl.pallas_call(
        functools.partial(_kv_proj_kernel, n_norm_tiles=n_norm_tiles),
        grid=(n // tn,),
        in_specs=[
            _resident((m, k)),
            _resident((1, k)),
            pl.BlockSpec((k, tn), lambda j: (0, j)),
            _resident((1, tn)),
        ],
        out_specs=pl.BlockSpec((m, tn), lambda j: (0, j)),
        out_shape=jax.ShapeDtypeStruct((m, n), BF16),
        scratch_shapes=[pltpu.VMEM((m, k), BF16)],
        compiler_params=_cparams(("arbitrary",)),
        name="kv_proj",
    )(x, gain.reshape(1, k), w, head_gain.reshape(1, tn))


def _proj_residual_kernel(*refs, n_lhs, tn):
    lhs = refs[:n_lhs]
    w_ref, res_ref, out_ref, wb_ref = refs[n_lhs:]
    kp = lhs[0].shape[1]
    _bf16_copy(w_ref, wb_ref)

    def matmul(j):
        cols = slice(j * tn, (j + 1) * tn)
        acc = jnp.dot(lhs[0][...], wb_ref[0:kp, cols], preferred_element_type=F32)
        for p in range(1, n_lhs):
            acc = acc + jnp.dot(lhs[p][...], wb_ref[p * kp:(p + 1) * kp, cols], preferred_element_type=F32)
        return acc

    def epilogue(j, acc):
        cols = slice(j * tn, (j + 1) * tn)
        out_ref[:, cols] = res_ref[:, cols] + acc

    _tile_pipeline(w_ref.shape[1] // tn, matmul, epilogue)


def _proj_residual(lhs_list, w, residual, *, tm, tn):
    m, n = residual.shape
    n_lhs = len(lhs_list)
    kp = lhs_list[0].shape[1]
    in_specs = [pl.BlockSpec((tm, kp), lambda i: (i, 0)) for _ in lhs_list]
    in_specs += [_resident(w.shape), pl.BlockSpec((tm, n), lambda i: (i, 0))]
    return pl.pallas_call(
        functools.partial(_proj_residual_kernel, n_lhs=n_lhs, tn=tn),
        grid=(m // tm,),
        in_specs=in_specs,
        out_specs=pl.BlockSpec((tm, n), lambda i: (i, 0)),
        out_shape=jax.ShapeDtypeStruct((m, n), F32),
        scratch_shapes=[pltpu.VMEM(w.shape, BF16)],
        compiler_params=_cparams(("arbitrary",)),
        name="proj_residual",
    )(*lhs_list, w, residual)


def _gates_kernel(g_ref, bias_ref, row_ref, col_ref, ka_ref):
    s_len = g_ref.shape[0]
    z = g_ref[...].T[0:N_GATES, :] + bias_ref[...]
    row = lax.broadcasted_iota(jnp.int32, z.shape, 0)
    is_forget = (row < FOX_HEADS) | (row >= FOX_HEADS + MLSTM_HEADS)
    logsig = jnp.minimum(z, 0.0) - jnp.log1p(jnp.exp(-jnp.abs(z)))
    c = jnp.where(is_forget, logsig, 0.0)
    lane = lax.broadcasted_iota(jnp.int32, z.shape, 1)
    shift = 1
    while shift < s_len:
        c = c + jnp.where(lane >= shift, pltpu.roll(c, shift, axis=1), 0.0)
        shift *= 2
    vals = jnp.where(is_forget, c, z)
    row_ref[...] = vals
    padded = jnp.concatenate([vals, jnp.zeros((LANES - N_GATES, s_len), F32)], axis=0)
    col_ref[...] = padded.T
    hi, mid, lo = _split3(c[0:FOX_HEADS, :] * LOG2E)
    aug = jnp.concatenate([jnp.ones((FOX_HEADS, s_len), F32), -hi, -mid, -lo,
                           jnp.zeros((LANES - 4 * FOX_HEADS, s_len), F32)], axis=0)
    ka_ref[...] = aug.T.astype(ka_ref.dtype)


def _gates(gates_raw, bias, *, batch, s_len):
    return pl.pallas_call(
        _gates_kernel,
        grid=(batch,),
        in_specs=[pl.BlockSpec((s_len, LANES), lambda b: (b, 0)),
                  pl.BlockSpec((N_GATES, 1), lambda b: (0, 0))],
        out_specs=[pl.BlockSpec((None, N_GATES, s_len), lambda b: (b, 0, 0)),
                   pl.BlockSpec((s_len, LANES), lambda b: (b, 0)),
                   pl.BlockSpec((s_len, LANES), lambda b: (b, 0))],
        out_shape=[jax.ShapeDtypeStruct((batch, N_GATES, s_len), F32),
                   jax.ShapeDtypeStruct((batch * s_len, LANES), F32),
                   jax.ShapeDtypeStruct((batch * s_len, LANES), BF16)],
        compiler_params=_cparams(("parallel",)),
        name="gates",
    )(gates_raw, bias)


def _fox_kernel(qi_ref, ki_ref, q_ref, k_ref, vt_ref, ka_ref, grow_ref, on_ref, o_ref, qt_ref, m_ref, acc_ref):
    pair = pl.program_id(1)
    qi = qi_ref[pair]
    ki = ki_ref[pair]
    tq, tk = q_ref.shape[0], k_ref.shape[0]
    d = FOX_HEAD_DIM

    @pl.when(ki == 0)
    def _():
        m_ref[...] = jnp.full(m_ref.shape, NEG_BIG, F32)
        acc_ref[...] = jnp.zeros(acc_ref.shape, F32)
        sub = lax.broadcasted_iota(jnp.int32, (SUBLANES, tq), 0)
        for h in range(FOX_HEADS):
            hi, mid, lo = _split3(grow_ref[h:h + 1, :] * LOG2E)
            c_rows = jnp.where(sub == 0, hi, jnp.where(sub == 1, mid, jnp.where(sub == 2, lo, 0.0)))
            pick = jnp.where(sub == h, 1.0, 0.0)
            qt = q_ref[:, h * d:(h + 1) * d].astype(F32).T
            pad = jnp.zeros((FOX_AUG_DEPTH - d - 4 * SUBLANES, tq), F32)
            qt_ref[h] = jnp.concatenate([qt, c_rows, pick, pick, pick, pad], axis=0).astype(BF16)

    def step(diagonal):
        sub = FOX_KEY_SUB
        ones = jnp.ones((FOX_ONES_ROWS, sub), BF16)
        if diagonal:
            keep = (lax.broadcasted_iota(jnp.int32, (sub, tq), 0)
                    <= lax.broadcasted_iota(jnp.int32, (sub, tq), 1))

        def scores(h, ks):
            t0 = ks * sub if diagonal else 0
            rows = slice(ks * sub, (ks + 1) * sub)
            k_aug = jnp.concatenate([k_ref[rows, h * d:(h + 1) * d], ka_ref[rows, :]], axis=1)
            st = jnp.dot(k_aug, qt_ref[h, :, t0:], preferred_element_type=F32)
            if diagonal:
                st = jnp.where(keep[:, 0:tq - t0], st, NEG_BIG)
            m_prev = m_ref[h:h + 1, t0:]
            m_new = jnp.maximum(m_prev, jnp.max(st, axis=0, keepdims=True))
            m_ref[h:h + 1, t0:] = m_new
            return t0, st, m_new, jnp.exp2(m_prev - m_new)

        def accumulate(h, ks, t0, st, m_new, alpha):
            p = jnp.exp2(st - m_new)
            vt_aug = jnp.concatenate([vt_ref[h * d:(h + 1) * d, ks * sub:(ks + 1) * sub], ones], axis=0)
            acc_ref[h, :, t0:] = alpha * acc_ref[h, :, t0:] + jnp.dot(vt_aug, p.astype(BF16),
                                                                      preferred_element_type=F32)

        items = [(h, ks) for h in range(FOX_HEADS) for ks in range(tk // sub)]
        lookahead = 2
        pending = [scores(*it) for it in items[:lookahead]]
        for n, it in enumerate(items):
            if n + lookahead < len(items):
                pending.append(scores(*items[n + lookahead]))
            accumulate(*it, *pending.pop(0))

    @pl.when(ki < qi)
    def _():
        step(False)

    @pl.when(ki == qi)
    def _():
        step(True)
        for h in range(FOX_HEADS):
            sl = slice(h * d, (h + 1) * d)
            o = (acc_ref[h, 0:d, :] / acc_ref[h, d:d + 1, :]).T
            o_ref[:, sl] = _rms_rows(o, on_ref[:, sl]).astype(o_ref.dtype)


def _fox_attention(proj, vt, ka, grow, out_norm, *, batch, s_len):
    t = FOX_BLOCK
    nq = s_len // t
    width = FOX_HEADS * FOX_HEAD_DIM
    pairs = [(qi, ki) for qi in range(nq) for ki in range(qi + 1)]
    qi_tab = jnp.asarray([p[0] for p in pairs], jnp.int32)
    ki_tab = jnp.asarray([p[1] for p in pairs], jnp.int32)

    def q_row(b, p, qi_tab, ki_tab):
        return b * nq + qi_tab[p]

    def k_row(b, p, qi_tab, ki_tab):
        return b * nq + ki_tab[p]

    grid_spec = pltpu.PrefetchScalarGridSpec(
        num_scalar_prefetch=2,
        grid=(batch, len(pairs)),
        in_specs=[
            pl.BlockSpec((t, width), lambda *a: (q_row(*a), 0)),
            pl.BlockSpec((t, width), lambda *a: (k_row(*a), 1)),
            pl.BlockSpec((width, t), lambda *a: (0, k_row(*a))),
            pl.BlockSpec((t, LANES), lambda *a: (k_row(*a), 0)),
            pl.BlockSpec((None, N_GATES, t), lambda b, p, qi_tab, ki_tab: (b, 0, qi_tab[p])),
            pl.BlockSpec((1, width), lambda *a: (0, 0)),
        ],
        out_specs=pl.BlockSpec((t, width), lambda *a: (q_row(*a), 0)),
        scratch_shapes=[
            pltpu.VMEM((FOX_HEADS, FOX_AUG_DEPTH, t), BF16),
            pltpu.VMEM((FOX_HEADS, t), F32),
            pltpu.VMEM((FOX_HEADS, FOX_HEAD_DIM + FOX_ONES_ROWS, t), F32),
        ],
    )
    return pl.pallas_call(
        _fox_kernel,
        grid_spec=grid_spec,
        out_shape=jax.ShapeDtypeStruct((batch * s_len, width), BF16),
        compiler_params=_cparams(("parallel", "arbitrary")),
        name="fox_attention",
    )(qi_tab, ki_tab, proj, proj, vt, ka, grow, out_norm.reshape(1, width))


def _mlstm_kernel(mqk_ref, mv_ref, mo_ref, gcol_ref, grow_ref, cw_ref, cb_ref, on_ref, out_ref,
                  ct_ref, mg_ref, ext_ref, shift_ref, qk_ref):
    c = pl.program_id(1)
    length = mqk_ref.shape[0]
    dk, dv = MLSTM_QK_DIM, MLSTM_V_DIM
    n_qk = MLSTM_HEADS * dk
    head = ext_ref.shape[0] - length
    tail = BF16_SUBLANES

    @pl.when(c == 0)
    def _():
        ct_ref[...] = jnp.zeros(ct_ref.shape, F32)
        mg_ref[...] = jnp.zeros(mg_ref.shape, F32)
        ext_ref[0:head, :] = jnp.zeros((head, ext_ref.shape[1]), BF16)
        row = lax.broadcasted_iota(jnp.int32, shift_ref.shape[1:], 0)
        col = lax.broadcasted_iota(jnp.int32, shift_ref.shape[1:], 1)
        for s in range(1, CONV_WIDTH):
            shift_ref[s - 1] = jnp.where(col == row + head - s, 1.0, 0.0).astype(BF16)

    u = mqk_ref[...]
    ext_ref[head:, :] = u
    ext = ext_ref[...]
    y = cb_ref[...] + cw_ref[CONV_WIDTH - 1:CONV_WIDTH, :] * u.astype(F32)
    for s in range(1, CONV_WIDTH):
        tap = CONV_WIDTH - 1 - s
        y = y + cw_ref[tap:tap + 1, :] * jnp.dot(shift_ref[s - 1], ext, preferred_element_type=F32)
    ext_ref[head - tail:head, :] = u[length - tail:, :]
    y = y * _sigmoid(y)
    qk_ref[:, 0:n_qk] = y[:, 0:n_qk].astype(BF16)
    qk_ref[:, n_qk:] = (y[:, n_qk:] * dk ** -0.5).astype(BF16)

    keep = (lax.broadcasted_iota(jnp.int32, (length, length), 0)
            >= lax.broadcasted_iota(jnp.int32, (length, length), 1))
    ones = jnp.ones((length, LANES), BF16)

    def scores(h):
        qb = qk_ref[:, h * dk:(h + 1) * dk]
        kb = qk_ref[:, n_qk + h * dk:n_qk + (h + 1) * dk]
        gi, gf = FOX_HEADS + h, FOX_HEADS + MLSTM_HEADS + h
        u_row = grow_ref[gi:gi + 1, :] - grow_ref[gf:gf + 1, :]
        mg = mg_ref[h:h + 1, 0:1]
        um = jnp.where(keep, u_row, NEG_BIG)
        a = jnp.maximum(jnp.max(um, axis=-1, keepdims=True), mg)
        sqk = lax.dot_general(qb, kb, (((1,), (1,)), ((), ())), preferred_element_type=F32) * jnp.exp(um - a)
        mg_new = jnp.maximum(mg, jnp.max(u_row, axis=-1, keepdims=True))
        return sqk, a, mg, mg_new

    def outputs(h, sqk, a, mg, mg_new):
        qb = qk_ref[:, h * dk:(h + 1) * dk]
        kb = qk_ref[:, n_qk + h * dk:n_qk + (h + 1) * dk]
        sl = slice(h * dv, (h + 1) * dv)
        v_aug = jnp.concatenate([mv_ref[:, sl], ones], axis=1)
        gi, gf = FOX_HEADS + h, FOX_HEADS + MLSTM_HEADS + h
        f_col = gcol_ref[:, gf:gf + 1]
        u_col = gcol_ref[:, gi:gi + 1] - f_col
        ct = ct_ref[h]
        inter = jnp.dot(qb, ct.astype(BF16), preferred_element_type=F32)
        nd = jnp.dot(sqk.astype(BF16), v_aug, preferred_element_type=F32) + jnp.exp(mg - a) * inter
        den = jnp.maximum(jnp.abs(nd[:, dv:dv + 1]), jnp.exp(-(f_col + a)))
        hid = nd[:, :dv] * (1.0 / den)
        gate = _sigmoid(mo_ref[:, sl].astype(F32))
        out_ref[:, sl] = (_rms_rows(hid, on_ref[:, sl]) * gate).astype(out_ref.dtype)

        wav = (jnp.exp(u_col - mg_new) * v_aug.astype(F32)).astype(BF16)
        kt = kb.astype(F32).T.astype(BF16)
        ct_ref[h] = jnp.exp(mg - mg_new) * ct + jnp.dot(kt, wav, preferred_element_type=F32)
        mg_ref[h:h + 1, :] = jnp.broadcast_to(mg_new, (1, LANES))

    pending = scores(0)
    for h in range(1, MLSTM_HEADS):
        nxt = scores(h)
        outputs(h - 1, *pending)
        pending = nxt
    outputs(MLSTM_HEADS - 1, *pending)


def _mlstm(proj, gcol, grow, conv_w, conv_b, out_norm, *, batch, s_len):
    length = MLSTM_CHUNK
    nc = s_len // length
    width = MLSTM_HEADS * MLSTM_V_DIM
    qk_width = 2 * MLSTM_HEADS * MLSTM_QK_DIM
    return pl.pallas_call(
        _mlstm_kernel,
        grid=(batch, nc),
        in_specs=[
            pl.BlockSpec((length, qk_width), lambda b, c: (b * nc + c, 2)),
            pl.BlockSpec((length, width), lambda b, c: (b * nc + c, 3)),
            pl.BlockSpec((length, width), lambda b, c: (b * nc + c, 4)),
            pl.BlockSpec((length, LANES), lambda b, c: (b * nc + c, 0)),
            pl.BlockSpec((None, N_GATES, length), lambda b, c: (b, 0, c)),
            pl.BlockSpec((CONV_WIDTH, qk_width), lambda b, c: (0, 0)),
            pl.BlockSpec((1, qk_width), lambda b, c: (0, 0)),
            pl.BlockSpec((1, width), lambda b, c: (0, 0)),
        ],
        out_specs=pl.BlockSpec((length, width), lambda b, c: (b * nc + c, 0)),
        out_shape=jax.ShapeDtypeStruct((batch * s_len, width), BF16),
        scratch_shapes=[
            pltpu.VMEM((MLSTM_HEADS, MLSTM_QK_DIM, MLSTM_V_DIM + LANES), F32),
            pltpu.VMEM((SUBLANES, LANES), F32),
            pltpu.VMEM((LANES + length, qk_width), BF16),
            pltpu.VMEM((CONV_WIDTH - 1, length, LANES + length), BF16),
            pltpu.VMEM((length, qk_width), BF16),
        ],
        compiler_params=_cparams(("parallel", "arbitrary")),
        name="mlstm",
    )(proj, proj, proj, gcol, grow, conv_w, conv_b.reshape(1, qk_width), out_norm.reshape(1, width))


def _xattn_kernel(q_ref, k_ref, v_ref, o_ref):
    d = q_ref.shape[1] // XATTN_HEADS

    def scores(h):
        sl = slice(h * d, (h + 1) * d)
        return lax.dot_general(q_ref[:, sl], k_ref[:, sl], (((1,), (1,)), ((), ())), preferred_element_type=F32)

    def outputs(h, s):
        sl = slice(h * d, (h + 1) * d)
        p = jnp.exp2(s - jnp.max(s, axis=-1, keepdims=True))
        p = p * (1.0 / jnp.sum(p, axis=-1, keepdims=True))
        o_ref[:, sl] = jnp.dot(p.astype(BF16), v_ref[:, sl], preferred_element_type=F32).astype(o_ref.dtype)

    _tile_pipeline(XATTN_HEADS, scores, outputs)


def _xattn(q, kv, *, batch, s_len, n_mem, tq):
    d_model = q.shape[1]
    nq = s_len // tq
    return pl.pallas_call(
        _xattn_kernel,
        grid=(batch, nq),
        in_specs=[
            pl.BlockSpec((tq, d_model), lambda b, i: (b * nq + i, 0)),
            pl.BlockSpec((n_mem, d_model), lambda b, i: (b, 0)),
            pl.BlockSpec((n_mem, d_model), lambda b, i: (b, 1)),
        ],
        out_specs=pl.BlockSpec((tq, d_model), lambda b, i: (b * nq + i, 0)),
        out_shape=jax.ShapeDtypeStruct(q.shape, BF16),
        compiler_params=_cparams(("parallel", "parallel")),
        name="xattn",
    )(q, kv, kv)


def _mlp_first_kernel(x_ref, g_ref, wu_ref, wd_ref, o_ref, wub_ref, wdb_ref, xn_ref):
    f = pl.program_id(0)

    @pl.when(f == 0)
    def _():
        x = x_ref[...]
        xn_ref[...] = _rms_rows(x, g_ref[...]).astype(BF16)
        o_ref[...] = x

    wu = wu_ref[...].astype(BF16)
    wd = wd_ref[...].astype(BF16)
    wub_ref[...] = wu
    wdb_ref[...] = wd
    hid = jnp.dot(xn_ref[...], wu, preferred_element_type=F32)
    hid = jnp.square(jnp.maximum(hid, 0.0)).astype(BF16)
    o_ref[...] += jnp.dot(hid, wd, preferred_element_type=F32)


def _mlp_first(x, gain, w_up, w_down, *, tm, tf):
    m, d = x.shape
    d_ff = w_up.shape[1]
    return pl.pallas_call(
        _mlp_first_kernel,
        grid=(d_ff // tf,),
        in_specs=[
            _resident((tm, d)),
            _resident((1, d)),
            pl.BlockSpec((d, tf), lambda f: (0, f)),
            pl.BlockSpec((tf, d), lambda f: (f, 0)),
        ],
        out_specs=[
            pl.BlockSpec((tm, d), lambda f: (0, 0)),
            pl.BlockSpec((d, tf), lambda f: (0, f)),
            pl.BlockSpec((tf, d), lambda f: (f, 0)),
        ],
        out_shape=[
            jax.ShapeDtypeStruct((m, d), F32),
            jax.ShapeDtypeStruct((d, d_ff), BF16),
            jax.ShapeDtypeStruct((d_ff, d), BF16),
        ],
        input_output_aliases={0: 0},
        scratch_shapes=[pltpu.VMEM((tm, d), BF16)],
        compiler_params=_cparams(("arbitrary",)),
        name="mlp_first",
    )(x, gain.reshape(1, d), w_up, w_down)


def _mlp_kernel(x_ref, g_ref, wu_ref, wd_ref, o_ref, xn_ref):
    i = pl.program_id(0)
    f = pl.program_id(1)

    @pl.when(f == 0)
    def _():
        x = x_ref[...]
        xn_ref[...] = _rms_rows(x, g_ref[...]).astype(BF16)
        o_ref[...] = x

    tc = MLP_CHUNK

    def up(j):
        return jnp.dot(xn_ref[...], wu_ref[:, j * tc:(j + 1) * tc], preferred_element_type=F32)

    def down(j, hid):
        hid = jnp.square(jnp.maximum(hid, 0.0)).astype(BF16)
        o_ref[...] += jnp.dot(hid, wd_ref[j * tc:(j + 1) * tc, :], preferred_element_type=F32)

    @pl.when(i > 0)
    def _():
        _tile_pipeline(wu_ref.shape[1] // tc, up, down)


def _mlp(x, gain, w_up, w_down, *, tm, tf):
    m, d = x.shape
    d_ff = w_up.shape[1]

    def tile(i, f):
        return jnp.where(i == 0, 0, f)

    return pl.pallas_call(
        _mlp_kernel,
        grid=(m // tm, d_ff // tf),
        in_specs=[
            pl.BlockSpec((tm, d), lambda i, f: (i, 0)),
            pl.BlockSpec((1, d), lambda i, f: (0, 0)),
            pl.BlockSpec((d, tf), lambda i, f: (0, tile(i, f))),
            pl.BlockSpec((tf, d), lambda i, f: (tile(i, f), 0)),
        ],
        out_specs=pl.BlockSpec((tm, d), lambda i, f: (i, 0)),
        out_shape=jax.ShapeDtypeStruct((m, d), F32),
        scratch_shapes=[pltpu.VMEM((tm, d), BF16)],
        compiler_params=_cparams(("parallel", "arbitrary")),
        name="mlp",
    )(x, gain.reshape(1, d), w_up, w_down)


def _layer(x, mem, mixer_norm, w_in, fox_f_bias, mlstm_i_bias, mlstm_f_bias, conv_w, conv_b, fox_q_norm,
           fox_k_norm, fox_out_norm, mlstm_out_norm, w_out, xattn_norm, mem_norm, w_xq, w_xkv, xq_norm,
           xk_norm, w_xo, mlp_norm, w_up, w_down, *, batch, s_len, n_mem):
    fox_w = FOX_HEADS * FOX_HEAD_DIM
    mqk_w = 2 * MLSTM_HEADS * MLSTM_QK_DIM
    mv_w = MLSTM_HEADS * MLSTM_V_DIM
    d_model = x.shape[1]

    o_ff = 3 * fox_w
    o_mqk = o_ff + FOX_HEADS
    o_mv = o_mqk + mqk_w
    o_mi = o_mv + mv_w
    o_mo = o_mi + 2 * MLSTM_HEADS
    w_main, w_gates = _w_in_prep(w_in.T, o_ff=o_ff, o_mi=o_mi, tb=512)
    gate_bias = jnp.concatenate([fox_f_bias, mlstm_i_bias, mlstm_f_bias]).reshape(N_GATES, 1)

    proj, vt, gates_raw = _in_proj(x, mixer_norm, w_main, jnp.stack([fox_q_norm, fox_k_norm]), w_gates, tm=256)
    grow, gcol, ka = _gates(gates_raw, gate_bias, batch=batch, s_len=s_len)
    fo = _fox_attention(proj, vt, ka, grow, fox_out_norm, batch=batch, s_len=s_len)
    mh = _mlstm(proj, gcol, grow, conv_w, conv_b, mlstm_out_norm, batch=batch, s_len=s_len)
    x = _proj_residual([fo, mh], w_out, x, tm=512, tn=512)

    xd = d_model // XATTN_HEADS
    q = _norm_proj(x, xattn_norm, w_xq, xq_norm, tm=512,
                   post_scale=xd ** -0.5 * LOG2E)
    kv = _kv_proj(mem, mem_norm, w_xkv, xk_norm, n_norm_tiles=XATTN_HEADS)
    co = _xattn(q, kv, batch=batch, s_len=s_len, n_mem=n_mem, tq=1024)
    x = _proj_residual([co], w_xo, x, tm=512, tn=512)

    tm = 1024
    x, w_up_b, w_down_b = _mlp_first(x, mlp_norm, w_up, w_down, tm=tm, tf=512)
    return _mlp(x, mlp_norm, w_up_b, w_down_b, tm=tm, tf=1024)


def kernel(x, mem, mixer_norm, w_in, fox_f_bias, mlstm_i_bias, mlstm_f_bias, conv_w, conv_b, fox_q_norm,
           fox_k_norm, fox_out_norm, mlstm_out_norm, w_out, xattn_norm, mem_norm, w_xq, w_xkv, xq_norm,
           xk_norm, w_xo, mlp_norm, w_up, w_down):
    batch, s_len, d_model = x.shape
    n_mem = mem.shape[1]
    depth = w_in.shape[0]
    h = x.reshape(batch * s_len, d_model)
    mem2 = mem.reshape(batch * n_mem, d_model)
    for l in range(depth):
        h = _layer(h, mem2, mixer_norm[l], w_in[l], fox_f_bias[l], mlstm_i_bias[l], mlstm_f_bias[l],
                   conv_w[l], conv_b[l], fox_q_norm[l], fox_k_norm[l], fox_out_norm[l], mlstm_out_norm[l],
                   w_out[l], xattn_norm[l], mem_norm[l], w_xq[l], w_xkv[l], xq_norm[l], xk_norm[l], w_xo[l],
                   mlp_norm[l], w_up[l], w_down[l], batch=batch, s_len=s_len, n_mem=n_mem)
    return h.reshape(batch, s_len, d_model)
```

```python
import functools

import jax
import jax.numpy as jnp
from jax import lax
from jax.experimental import pallas as pl
from jax.experimental.pallas import tpu as pltpu

F32 = jnp.float32
BF16 = jnp.bfloat16

EPS = 1e-6
FOX_HEADS = 8
FOX_HEAD_DIM = 128
MLSTM_HEADS = 4
MLSTM_QK_DIM = 128
MLSTM_V_DIM = 256
CONV_WIDTH = 4
XATTN_HEADS = 4
N_GATES = 16
LANES = 128
SUBLANES = 8
BF16_SUBLANES = 16
NEG_BIG = -1e30
LOG2E = 1.4426950408889634
FOX_AUG_DEPTH = 256
FOX_ONES_ROWS = 16

MLSTM_CHUNK = 256
FOX_BLOCK = 512
FOX_KEY_SUB = 256
IN_PROJ_TILE = 512
MLP_CHUNK = 256
VMEM_LIMIT = 56 * 1024 * 1024


def _cparams(sem):
    return pltpu.CompilerParams(dimension_semantics=sem, vmem_limit_bytes=VMEM_LIMIT)


def _rms_rows(x, gain):
    ms = jnp.mean(x * x, axis=-1, keepdims=True)
    return x * lax.rsqrt(ms + EPS) * gain


def _sigmoid(x):
    return 1.0 / (1.0 + jnp.exp(-x))


def _split3(x):
    hi = x.astype(BF16).astype(F32)
    r = x - hi
    mid = r.astype(BF16).astype(F32)
    lo = (r - mid).astype(BF16).astype(F32)
    return hi, mid, lo


def _resident(block_shape):
    return pl.BlockSpec(block_shape, lambda i: (0,) * len(block_shape), pipeline_mode=pl.Buffered(1))


def _tile_pipeline(n_tiles, matmul, epilogue):
    pending = matmul(0)
    for j in range(1, n_tiles):
        nxt = matmul(j)
        epilogue(j - 1, pending)
        pending = nxt
    epilogue(n_tiles - 1, pending)


def _w_in_prep_kernel(wt_ref, gf_ref, gm_ref, main_ref, gates_ref):
    main_ref[...] = wt_ref[...].T.astype(BF16)

    @pl.when(pl.program_id(0) == 0)
    def _():
        pad = jnp.zeros((LANES - N_GATES, gf_ref.shape[1]), F32)
        gates_ref[...] = jnp.concatenate([gf_ref[...], gm_ref[...], pad], axis=0).T.astype(BF16)


def _w_in_prep(wt, *, o_ff, o_mi, tb):
    n_all, k = wt.shape
    n_main = n_all - N_GATES
    n_fox, n_before_mi = o_ff // tb, (o_mi - FOX_HEADS) // tb
    assert o_ff % tb == 0 and (o_mi - FOX_HEADS) % tb == 0 and n_main % tb == 0

    def src_row(b):
        return b * tb + jnp.where(b >= n_fox, FOX_HEADS, 0) + jnp.where(b >= n_before_mi, 2 * MLSTM_HEADS, 0)

    return pl.pallas_call(
        _w_in_prep_kernel,
        grid=(n_main // tb,),
        in_specs=[
            pl.BlockSpec((pl.Element(tb), pl.Element(k)), lambda b: (pl.multiple_of(src_row(b), SUBLANES), 0)),
            pl.BlockSpec((pl.Element(FOX_HEADS), pl.Element(k)), lambda b: (o_ff, 0)),
            pl.BlockSpec((pl.Element(2 * MLSTM_HEADS), pl.Element(k)), lambda b: (o_mi, 0)),
        ],
        out_specs=[pl.BlockSpec((k, tb), lambda b: (0, b)), pl.BlockSpec((k, LANES), lambda b: (0, 0))],
        out_shape=[jax.ShapeDtypeStruct((k, n_main), BF16), jax.ShapeDtypeStruct((k, LANES), BF16)],
        compiler_params=_cparams(("arbitrary",)),
        name="w_in_prep",
    )(wt, wt, wt)


def _in_proj_kernel(x_ref, g_ref, w_ref, qk_gain_ref, wg_ref, out_ref, vt_ref, gates_ref):
    seg_w = FOX_HEADS * FOX_HEAD_DIM
    tn = IN_PROJ_TILE
    xn = _rms_rows(x_ref[...], g_ref[...]).astype(BF16)
    sources = (2 * seg_w, 0, seg_w, 3 * seg_w, 4 * seg_w, 5 * seg_w)

    def matmul(j):
        seg, off = divmod(j * tn, seg_w)
        col0 = sources[seg] + off
        return jnp.dot(xn, w_ref[:, col0:col0 + tn], preferred_element_type=F32)

    def head_norm(acc, gain, post_scale, col0):
        for h in range(tn // FOX_HEAD_DIM):
            sl = slice(h * FOX_HEAD_DIM, (h + 1) * FOX_HEAD_DIM)
            y = _rms_rows(acc[:, sl], gain) * post_scale
            out_ref[:, col0 + h * FOX_HEAD_DIM:col0 + (h + 1) * FOX_HEAD_DIM] = y.astype(out_ref.dtype)

    def epilogue(j, acc):
        seg, off = divmod(j * tn, seg_w)
        if seg == 0:
            vt_ref[off:off + tn, :] = acc.T.astype(vt_ref.dtype)
        elif seg == 1:
            head_norm(acc, qk_gain_ref[0], FOX_HEAD_DIM ** -0.5 * LOG2E, off)
        elif seg == 2:
            head_norm(acc, qk_gain_ref[1], 1.0, seg_w + off)
        else:
            out_ref[:, (seg - 1) * seg_w + off:(seg - 1) * seg_w + off + tn] = acc.astype(out_ref.dtype)

    gates_ref[...] = jnp.dot(xn, wg_ref[...], preferred_element_type=F32)
    _tile_pipeline(len(sources) * seg_w // tn, matmul, epilogue)


def _in_proj(x, gain, w, qk_gains, w_gates, *, tm):
    m, k = x.shape
    tn = FOX_HEADS * FOX_HEAD_DIM
    n = w.shape[1]
    return pl.pallas_call(
        _in_proj_kernel,
        grid=(m // tm,),
        in_specs=[
            pl.BlockSpec((tm, k), lambda i: (i, 0)),
            _resident((1, k)),
            _resident((k, n)),
            _resident((2, 1, FOX_HEAD_DIM)),
            _resident((k, LANES)),
        ],
        out_specs=[
            pl.BlockSpec((tm, n - tn), lambda i: (i, 0)),
            pl.BlockSpec((tn, tm), lambda i: (0, i)),
            pl.BlockSpec((tm, LANES), lambda i: (i, 0)),
        ],
        out_shape=[
            jax.ShapeDtypeStruct((m, n - tn), BF16),
            jax.ShapeDtypeStruct((tn, m), BF16),
            jax.ShapeDtypeStruct((m, LANES), F32),
        ],
        compiler_params=_cparams(("parallel",)),
        name="in_proj",
    )(x, gain.reshape(1, k), w, qk_gains.reshape(2, 1, FOX_HEAD_DIM), w_gates)


def _bf16_copy(w_ref, wb_ref):
    @pl.when(pl.program_id(0) == 0)
    def _():
        wb_ref[...] = w_ref[...].astype(BF16)
    return wb_ref


def _norm_proj_kernel(x_ref, g_ref, w_ref, hg_ref, out_ref, wb_ref, *, post_scale):
    tn = hg_ref.shape[1]
    _bf16_copy(w_ref, wb_ref)
    xn = _rms_rows(x_ref[...], g_ref[...]).astype(BF16)

    def matmul(j):
        return jnp.dot(xn, wb_ref[:, j * tn:(j + 1) * tn], preferred_element_type=F32)

    def epilogue(j, acc):
        out_ref[:, j * tn:(j + 1) * tn] = (_rms_rows(acc, hg_ref[...]) * post_scale).astype(out_ref.dtype)

    _tile_pipeline(w_ref.shape[1] // tn, matmul, epilogue)


def _norm_proj(x, gain, w, head_gain, *, tm, post_scale):
    m, k = x.shape
    n = w.shape[1]
    tn = head_gain.shape[0]
    return pl.pallas_call(
        functools.partial(_norm_proj_kernel, post_scale=post_scale),
        grid=(m // tm,),
        in_specs=[
            pl.BlockSpec((tm, k), lambda i: (i, 0)),
            _resident((1, k)),
            _resident((k, n)),
            _resident((1, tn)),
        ],
        out_specs=pl.BlockSpec((tm, n), lambda i: (i, 0)),
        out_shape=jax.ShapeDtypeStruct((m, n), BF16),
        scratch_shapes=[pltpu.VMEM((k, n), BF16)],
        compiler_params=_cparams(("arbitrary",)),
        name="norm_proj",
    )(x, gain.reshape(1, k), w, head_gain.reshape(1, tn))


def _kv_proj_kernel(x_ref, g_ref, w_ref, hg_ref, out_ref, xn_ref, *, n_norm_tiles):
    j = pl.program_id(0)

    @pl.when(j == 0)
    def _():
        xn_ref[...] = _rms_rows(x_ref[...], g_ref[...]).astype(BF16)

    acc = jnp.dot(xn_ref[...], w_ref[...].astype(BF16), preferred_element_type=F32)

    @pl.when(j < n_norm_tiles)
    def _():
        out_ref[...] = _rms_rows(acc, hg_ref[...]).astype(out_ref.dtype)

    @pl.when(j >= n_norm_tiles)
    def _():
        out_ref[...] = acc.astype(out_ref.dtype)


def _kv_proj(x, gain, w, head_gain, *, n_norm_tiles):
    m, k = x.shape
    n = w.shape[1]
    tn = head_gain.shape[0]
    return pl.pallas_call(
        functools.partial(_kv_proj_kernel, n_norm_tiles=n_norm_tiles),
        grid=(n // tn,),
        in_specs=[
            _resident((m, k)),
            _resident((1, k)),
            pl.BlockSpec((k, tn), lambda j: (0, j)),
            _resident((1, tn)),
        ],
        out_specs=pl.BlockSpec((m, tn), lambda j: (0, j)),
        out_shape=jax.ShapeDtypeStruct((m, n), BF16),
        scratch_shapes=[pltpu.VMEM((m, k), BF16)],
        compiler_params=_cparams(("arbitrary",)),
        name="kv_proj",
    )(x, gain.reshape(1, k), w, head_gain.reshape(1, tn))


def _proj_residual_kernel(*refs, n_lhs, tn):
    lhs = refs[:n_lhs]
    w_ref, res_ref, out_ref, wb_ref = refs[n_lhs:]
    kp = lhs[0].shape[1]
    _bf16_copy(w_ref, wb_ref)

    def matmul(j):
        cols = slice(j * tn, (j + 1) * tn)
        acc = jnp.dot(lhs[0][...], wb_ref[0:kp, cols], preferred_element_type=F32)
        for p in range(1, n_lhs):
            acc = acc + jnp.dot(lhs[p][...], wb_ref[p * kp:(p + 1) * kp, cols], preferred_element_type=F32)
        return acc

    def epilogue(j, acc):
        cols = slice(j * tn, (j + 1) * tn)
        out_ref[:, cols] = res_ref[:, cols] + acc

    _tile_pipeline(w_ref.shape[1] // tn, matmul, epilogue)


def _proj_residual(lhs_list, w, residual, *, tm, tn):
    m, n = residual.shape
    n_lhs = len(lhs_list)
    kp = lhs_list[0].shape[1]
    in_specs = [pl.BlockSpec((tm, kp), lambda i: (i, 0)) for _ in lhs_list]
    in_specs += [_resident(w.shape), pl.BlockSpec((tm, n), lambda i: (i, 0))]
    return pl.pallas_call(
        functools.partial(_proj_residual_kernel, n_lhs=n_lhs, tn=tn),
        grid=(m // tm,),
        in_specs=in_specs,
        out_specs=pl.BlockSpec((tm, n), lambda i: (i, 0)),
        out_shape=jax.ShapeDtypeStruct((m, n), F32),
        scratch_shapes=[pltpu.VMEM(w.shape, BF16)],
        compiler_params=_cparams(("arbitrary",)),
        name="proj_residual",
    )(*lhs_list, w, residual)


def _gates_kernel(g_ref, bias_ref, row_ref, col_ref, ka_ref):
    s_len = g_ref.shape[0]
    z = g_ref[...].T[0:N_GATES, :] + bias_ref[...]
    row = lax.broadcasted_iota(jnp.int32, z.shape, 0)
    is_forget = (row < FOX_HEADS) | (row >= FOX_HEADS + MLSTM_HEADS)
    logsig = jnp.minimum(z, 0.0) - jnp.log1p(jnp.exp(-jnp.abs(z)))
    c = jnp.where(is_forget, logsig, 0.0)
    lane = lax.broadcasted_iota(jnp.int32, z.shape, 1)
    shift = 1
    while shift < s_len:
        c = c + jnp.where(lane >= shift, pltpu.roll(c, shift, axis=1), 0.0)
        shift *= 2
    vals = jnp.where(is_forget, c, z)
    row_ref[...] = vals
    padded = jnp.concatenate([vals, jnp.zeros((LANES - N_GATES, s_len), F32)], axis=0)
    col_ref[...] = padded.T
    hi, mid, lo = _split3(c[0:FOX_HEADS, :] * LOG2E)
    aug = jnp.concatenate([jnp.ones((FOX_HEADS, s_len), F32), -hi, -mid, -lo,
                           jnp.zeros((LANES - 4 * FOX_HEADS, s_len), F32)], axis=0)
    ka_ref[...] = aug.T.astype(ka_ref.dtype)


def _gates(gates_raw, bias, *, batch, s_len):
    return pl.pallas_call(
        _gates_kernel,
        grid=(batch,),
        in_specs=[pl.BlockSpec((s_len, LANES), lambda b: (b, 0)),
                  pl.BlockSpec((N_GATES, 1), lambda b: (0, 0))],
        out_specs=[pl.BlockSpec((None, N_GATES, s_len), lambda b: (b, 0, 0)),
                   pl.BlockSpec((s_len, LANES), lambda b: (b, 0)),
                   pl.BlockSpec((s_len, LANES), lambda b: (b, 0))],
        out_shape=[jax.ShapeDtypeStruct((batch, N_GATES, s_len), F32),
                   jax.ShapeDtypeStruct((batch * s_len, LANES), F32),
                   jax.ShapeDtypeStruct((batch * s_len, LANES), BF16)],
        compiler_params=_cparams(("parallel",)),
        name="gates",
    )(gates_raw, bias)


def _fox_kernel(qi_ref, ki_ref, q_ref, k_ref, vt_ref, ka_ref, grow_ref, on_ref, o_ref, qt_ref, m_ref, acc_ref):
    pair = pl.program_id(1)
    qi = qi_ref[pair]
    ki = ki_ref[pair]
    tq, tk = q_ref.shape[0], k_ref.shape[0]
    d = FOX_HEAD_DIM

    @pl.when(ki == 0)
    def _():
        m_ref[...] = jnp.full(m_ref.shape, NEG_BIG, F32)
        acc_ref[...] = jnp.zeros(acc_ref.shape, F32)
        sub = lax.broadcasted_iota(jnp.int32, (SUBLANES, tq), 0)
        for h in range(FOX_HEADS):
            hi, mid, lo = _split3(grow_ref[h:h + 1, :] * LOG2E)
            c_rows = jnp.where(sub == 0, hi, jnp.where(sub == 1, mid, jnp.where(sub == 2, lo, 0.0)))
            pick = jnp.where(sub == h, 1.0, 0.0)
            qt = q_ref[:, h * d:(h + 1) * d].astype(F32).T
            pad = jnp.zeros((FOX_AUG_DEPTH - d - 4 * SUBLANES, tq), F32)
            qt_ref[h] = jnp.concatenate([qt, c_rows, pick, pick, pick, pad], axis=0).astype(BF16)

    def step(diagonal):
        sub = FOX_KEY_SUB
        ones = jnp.ones((FOX_ONES_ROWS, sub), BF16)
        if diagonal:
            keep = (lax.broadcasted_iota(jnp.int32, (sub, tq), 0)
                    <= lax.broadcasted_iota(jnp.int32, (sub, tq), 1))

        def scores(h, ks):
            t0 = ks * sub if diagonal else 0
            rows = slice(ks * sub, (ks + 1) * sub)
            k_aug = jnp.concatenate([k_ref[rows, h * d:(h + 1) * d], ka_ref[rows, :]], axis=1)
            st = jnp.dot(k_aug, qt_ref[h, :, t0:], preferred_element_type=F32)
            if diagonal:
                st = jnp.where(keep[:, 0:tq - t0], st, NEG_BIG)
            m_prev = m_ref[h:h + 1, t0:]
            m_new = jnp.maximum(m_prev, jnp.max(st, axis=0, keepdims=True))
            m_ref[h:h + 1, t0:] = m_new
            return t0, st, m_new, jnp.exp2(m_prev - m_new)

        def accumulate(h, ks, t0, st, m_new, alpha):
            p = jnp.exp2(st - m_new)
            vt_aug = jnp.concatenate([vt_ref[h * d:(h + 1) * d, ks * sub:(ks + 1) * sub], ones], axis=0)
            acc_ref[h, :, t0:] = alpha * acc_ref[h, :, t0:] + jnp.dot(vt_aug, p.astype(BF16),
                                                                      preferred_element_type=F32)

        items = [(h, ks) for h in range(FOX_HEADS) for ks in range(tk // sub)]
        lookahead = 2
        pending = [scores(*it) for it in items[:lookahead]]
        for n, it in enumerate(items):
            if n + lookahead < len(items):
                pending.append(scores(*items[n + lookahead]))
            accumulate(*it, *pending.pop(0))

    @pl.when(ki < qi)
    def _():
        step(False)

    @pl.when(ki == qi)
    def _():
        step(True)
        for h in range(FOX_HEADS):
            sl = slice(h * d, (h + 1) * d)
            o = (acc_ref[h, 0:d, :] / acc_ref[h, d:d + 1, :]).T
            o_ref[:, sl] = _rms_rows(o, on_ref[:, sl]).astype(o_ref.dtype)


def _fox_attention(proj, vt, ka, grow, out_norm, *, batch, s_len):
    t = FOX_BLOCK
    nq = s_len // t
    width = FOX_HEADS * FOX_HEAD_DIM
    pairs = [(qi, ki) for qi in range(nq) for ki in range(qi + 1)]
    qi_tab = jnp.asarray([p[0] for p in pairs], jnp.int32)
    ki_tab = jnp.asarray([p[1] for p in pairs], jnp.int32)

    def q_row(b, p, qi_tab, ki_tab):
        return b * nq + qi_tab[p]

    def k_row(b, p, qi_tab, ki_tab):
        return b * nq + ki_tab[p]

    grid_spec = pltpu.PrefetchScalarGridSpec(
        num_scalar_prefetch=2,
        grid=(batch, len(pairs)),
        in_specs=[
            pl.BlockSpec((t, width), lambda *a: (q_row(*a), 0)),
            pl.BlockSpec((t, width), lambda *a: (k_row(*a), 1)),
            pl.BlockSpec((width, t), lambda *a: (0, k_row(*a))),
            pl.BlockSpec((t, LANES), lambda *a: (k_row(*a), 0)),
            pl.BlockSpec((None, N_GATES, t), lambda b, p, qi_tab, ki_tab: (b, 0, qi_tab[p])),
            pl.BlockSpec((1, width), lambda *a: (0, 0)),
        ],
        out_specs=pl.BlockSpec((t, width), lambda *a: (q_row(*a), 0)),
        scratch_shapes=[
            pltpu.VMEM((FOX_HEADS, FOX_AUG_DEPTH, t), BF16),
            pltpu.VMEM((FOX_HEADS, t), F32),
            pltpu.VMEM((FOX_HEADS, FOX_HEAD_DIM + FOX_ONES_ROWS, t), F32),
        ],
    )
    return pl.pallas_call(
        _fox_kernel,
        grid_spec=grid_spec,
        out_shape=jax.ShapeDtypeStruct((batch * s_len, width), BF16),
        compiler_params=_cparams(("parallel", "arbitrary")),
        name="fox_attention",
    )(qi_tab, ki_tab, proj, proj, vt, ka, grow, out_norm.reshape(1, width))


def _mlstm_kernel(mqk_ref, mv_ref, mo_ref, gcol_ref, grow_ref, cw_ref, cb_ref, on_ref, out_ref,
                  ct_ref, mg_ref, ext_ref, shift_ref, qk_ref):
    c = pl.program_id(1)
    length = mqk_ref.shape[0]
    dk, dv = MLSTM_QK_DIM, MLSTM_V_DIM
    n_qk = MLSTM_HEADS * dk
    head = ext_ref.shape[0] - length
    tail = BF16_SUBLANES

    @pl.when(c == 0)
    def _():
        ct_ref[...] = jnp.zeros(ct_ref.shape, F32)
        mg_ref[...] = jnp.zeros(mg_ref.shape, F32)
        ext_ref[0:head, :] = jnp.zeros((head, ext_ref.shape[1]), BF16)
        row = lax.broadcasted_iota(jnp.int32, shift_ref.shape[1:], 0)
        col = lax.broadcasted_iota(jnp.int32, shift_ref.shape[1:], 1)
        for s in range(1, CONV_WIDTH):
            shift_ref[s - 1] = jnp.where(col == row + head - s, 1.0, 0.0).astype(BF16)

    u = mqk_ref[...]
    ext_ref[head:, :] = u
    ext = ext_ref[...]
    y = cb_ref[...] + cw_ref[CONV_WIDTH - 1:CONV_WIDTH, :] * u.astype(F32)
    for s in range(1, CONV_WIDTH):
        tap = CONV_WIDTH - 1 - s
        y = y + cw_ref[tap:tap + 1, :] * jnp.dot(shift_ref[s - 1], ext, preferred_element_type=F32)
    ext_ref[head - tail:head, :] = u[length - tail:, :]
    y = y * _sigmoid(y)
    qk_ref[:, 0:n_qk] = y[:, 0:n_qk].astype(BF16)
    qk_ref[:, n_qk:] = (y[:, n_qk:] * dk ** -0.5).astype(BF16)

    keep = (lax.broadcasted_iota(jnp.int32, (length, length), 0)
            >= lax.broadcasted_iota(jnp.int32, (length, length), 1))
    ones = jnp.ones((length, LANES), BF16)

    def scores(h):
        qb = qk_ref[:, h * dk:(h + 1) * dk]
        kb = qk_ref[:, n_qk + h * dk:n_qk + (h + 1) * dk]
        gi, gf = FOX_HEADS + h, FOX_HEADS + MLSTM_HEADS + h
        u_row = grow_ref[gi:gi + 1, :] - grow_ref[gf:gf + 1, :]
        mg = mg_ref[h:h + 1, 0:1]
        um = jnp.where(keep, u_row, NEG_BIG)
        a = jnp.maximum(jnp.max(um, axis=-1, keepdims=True), mg)
        sqk = lax.dot_general(qb, kb, (((1,), (1,)), ((), ())), preferred_element_type=F32) * jnp.exp(um - a)
        mg_new = jnp.maximum(mg, jnp.max(u_row, axis=-1, keepdims=True))
        return sqk, a, mg, mg_new

    def outputs(h, sqk, a, mg, mg_new):
        qb = qk_ref[:, h * dk:(h + 1) * dk]
        kb = qk_ref[:, n_qk + h * dk:n_qk + (h + 1) * dk]
        sl = slice(h * dv, (h + 1) * dv)
        v_aug = jnp.concatenate([mv_ref[:, sl], ones], axis=1)
        gi, gf = FOX_HEADS + h, FOX_HEADS + MLSTM_HEADS + h
        f_col = gcol_ref[:, gf:gf + 1]
        u_col = gcol_ref[:, gi:gi + 1] - f_col
        ct = ct_ref[h]
        inter = jnp.dot(qb, ct.astype(BF16), preferred_element_type=F32)
        nd = jnp.dot(sqk.astype(BF16), v_aug, preferred_element_type=F32) + jnp.exp(mg - a) * inter
        den = jnp.maximum(jnp.abs(nd[:, dv:dv + 1]), jnp.exp(-(f_col + a)))
        hid = nd[:, :dv] * (1.0 / den)
        gate = _sigmoid(mo_ref[:, sl].astype(F32))
        out_ref[:, sl] = (_rms_rows(hid, on_ref[:, sl]) * gate).astype(out_ref.dtype)

        wav = (jnp.exp(u_col - mg_new) * v_aug.astype(F32)).astype(BF16)
        kt = kb.astype(F32).T.astype(BF16)
        ct_ref[h] = jnp.exp(mg - mg_new) * ct + jnp.dot(kt, wav, preferred_element_type=F32)
        mg_ref[h:h + 1, :] = jnp.broadcast_to(mg_new, (1, LANES))

    pending = scores(0)
    for h in range(1, MLSTM_HEADS):
        nxt = scores(h)
        outputs(h - 1, *pending)
        pending = nxt
    outputs(MLSTM_HEADS - 1, *pending)


def _mlstm(proj, gcol, grow, conv_w, conv_b, out_norm, *, batch, s_len):
    length = MLSTM_CHUNK
    nc = s_len // length
    width = MLSTM_HEADS * MLSTM_V_DIM
    qk_width = 2 * MLSTM_HEADS * MLSTM_QK_DIM
    return pl.pallas_call(
        _mlstm_kernel,
        grid=(batch, nc),
        in_specs=[
            pl.BlockSpec((length, qk_width), lambda b, c: (b * nc + c, 2)),
            pl.BlockSpec((length, width), lambda b, c: (b * nc + c, 3)),
            pl.BlockSpec((length, width), lambda b, c: (b * nc + c, 4)),
            pl.BlockSpec((length, LANES), lambda b, c: (b * nc + c, 0)),
            pl.BlockSpec((None, N_GATES, length), lambda b, c: (b, 0, c)),
            pl.BlockSpec((CONV_WIDTH, qk_width), lambda b, c: (0, 0)),
            pl.BlockSpec((1, qk_width), lambda b, c: (0, 0)),
            pl.BlockSpec((1, width), lambda b, c: (0, 0)),
        ],
        out_specs=pl.BlockSpec((length, width), lambda b, c: (b * nc + c, 0)),
        out_shape=jax.ShapeDtypeStruct((batch * s_len, width), BF16),
        scratch_shapes=[
            pltpu.VMEM((MLSTM_HEADS, MLSTM_QK_DIM, MLSTM_V_DIM + LANES), F32),
            pltpu.VMEM((SUBLANES, LANES), F32),
            pltpu.VMEM((LANES + length, qk_width), BF16),
            pltpu.VMEM((CONV_WIDTH - 1, length, LANES + length), BF16),
            pltpu.VMEM((length, qk_width), BF16),
        ],
        compiler_params=_cparams(("parallel", "arbitrary")),
        name="mlstm",
    )(proj, proj, proj, gcol, grow, conv_w, conv_b.reshape(1, qk_width), out_norm.reshape(1, width))


def _xattn_kernel(q_ref, k_ref, v_ref, o_ref):
    d = q_ref.shape[1] // XATTN_HEADS

    def scores(h):
        sl = slice(h * d, (h + 1) * d)
        return lax.dot_general(q_ref[:, sl], k_ref[:, sl], (((1,), (1,)), ((), ())), preferred_element_type=F32)

    def outputs(h, s):
        sl = slice(h * d, (h + 1) * d)
        p = jnp.exp2(s - jnp.max(s, axis=-1, keepdims=True))
        p = p * (1.0 / jnp.sum(p, axis=-1, keepdims=True))
        o_ref[:, sl] = jnp.dot(p.astype(BF16), v_ref[:, sl], preferred_element_type=F32).astype(o_ref.dtype)

    _tile_pipeline(XATTN_HEADS, scores, outputs)


def _xattn(q, kv, *, batch, s_len, n_mem, tq):
    d_model = q.shape[1]
    nq = s_len // tq
    return pl.pallas_call(
        _xattn_kernel,
        grid=(batch, nq),
        in_specs=[
            pl.BlockSpec((tq, d_model), lambda b, i: (b * nq + i, 0)),
            pl.BlockSpec((n_mem, d_model), lambda b, i: (b, 0)),
            pl.BlockSpec((n_mem, d_model), lambda b, i: (b, 1)),
        ],
        out_specs=pl.BlockSpec((tq, d_model), lambda b, i: (b * nq + i, 0)),
        out_shape=jax.ShapeDtypeStruct(q.shape, BF16),
        compiler_params=_cparams(("parallel", "parallel")),
        name="xattn",
    )(q, kv, kv)


def _mlp_first_kernel(x_ref, g_ref, wu_ref, wd_ref, o_ref, wub_ref, wdb_ref, xn_ref):
    f = pl.program_id(0)

    @pl.when(f == 0)
    def _():
        x = x_ref[...]
        xn_ref[...] = _rms_rows(x, g_ref[...]).astype(BF16)
        o_ref[...] = x

    wu = wu_ref[...].astype(BF16)
    wd = wd_ref[...].astype(BF16)
    wub_ref[...] = wu
    wdb_ref[...] = wd
    hid = jnp.dot(xn_ref[...], wu, preferred_element_type=F32)
    hid = jnp.square(jnp.maximum(hid, 0.0)).astype(BF16)
    o_ref[...] += jnp.dot(hid, wd, preferred_element_type=F32)


def _mlp_first(x, gain, w_up, w_down, *, tm, tf):
    d = x.shape[1]
    d_ff = w_up.shape[1]
    return pl.pallas_call(
        _mlp_first_kernel,
        grid=(d_ff // tf,),
        in_specs=[
            _resident((tm, d)),
            _resident((1, d)),
            pl.BlockSpec((d, tf), lambda f: (0, f)),
            pl.BlockSpec((tf, d), lambda f: (f, 0)),
        ],
        out_specs=[
            pl.BlockSpec((tm, d), lambda f: (0, 0)),
            pl.BlockSpec((d, tf), lambda f: (0, f)),
            pl.BlockSpec((tf, d), lambda f: (f, 0)),
        ],
        out_shape=[
            jax.ShapeDtypeStruct((tm, d), F32),
            jax.ShapeDtypeStruct((d, d_ff), BF16),
            jax.ShapeDtypeStruct((d_ff, d), BF16),
        ],
        scratch_shapes=[pltpu.VMEM((tm, d), BF16)],
        compiler_params=_cparams(("arbitrary",)),
        name="mlp_first",
    )(x, gain.reshape(1, d), w_up, w_down)


def _mlp_kernel(x_ref, g_ref, wu_ref, wd_ref, o_ref, xn_ref):
    i = pl.program_id(0)
    f = pl.program_id(1)

    @pl.when(f == 0)
    def _():
        x = x_ref[...]
        xn_ref[...] = _rms_rows(x, g_ref[...]).astype(BF16)
        o_ref[...] = x

    tc = MLP_CHUNK

    def up(j):
        return jnp.dot(xn_ref[...], wu_ref[:, j * tc:(j + 1) * tc], preferred_element_type=F32)

    def down(j, hid):
        hid = jnp.square(jnp.maximum(hid, 0.0)).astype(BF16)
        o_ref[...] += jnp.dot(hid, wd_ref[j * tc:(j + 1) * tc, :], preferred_element_type=F32)

    @pl.when(i > 0)
    def _():
        _tile_pipeline(wu_ref.shape[1] // tc, up, down)


def _mlp(x, gain, w_up, w_down, *, tm, tf):
    m, d = x.shape
    d_ff = w_up.shape[1]

    def tile(i, f):
        return jnp.where(i == 0, 0, f)

    return pl.pallas_call(
        _mlp_kernel,
        grid=(m // tm, d_ff // tf),
        in_specs=[
            pl.BlockSpec((tm, d), lambda i, f: (i, 0)),
            pl.BlockSpec((1, d), lambda i, f: (0, 0)),
            pl.BlockSpec((d, tf), lambda i, f: (0, tile(i, f))),
            pl.BlockSpec((tf, d), lambda i, f: (tile(i, f), 0)),
        ],
        out_specs=pl.BlockSpec((tm, d), lambda i, f: (i, 0)),
        out_shape=jax.ShapeDtypeStruct((m, d), F32),
        scratch_shapes=[pltpu.VMEM((tm, d), BF16)],
        compiler_params=_cparams(("parallel", "arbitrary")),
        name="mlp",
    )(x, gain.reshape(1, d), w_up, w_down)


def _layer(x, mem, mixer_norm, w_in, fox_f_bias, mlstm_i_bias, mlstm_f_bias, conv_w, conv_b, fox_q_norm,
           fox_k_norm, fox_out_norm, mlstm_out_norm, w_out, xattn_norm, mem_norm, w_xq, w_xkv, xq_norm,
           xk_norm, w_xo, mlp_norm, w_up, w_down, *, batch, s_len, n_mem):
    fox_w = FOX_HEADS * FOX_HEAD_DIM
    mqk_w = 2 * MLSTM_HEADS * MLSTM_QK_DIM
    mv_w = MLSTM_HEADS * MLSTM_V_DIM
    d_model = x.shape[1]

    o_ff = 3 * fox_w
    o_mqk = o_ff + FOX_HEADS
    o_mv = o_mqk + mqk_w
    o_mi = o_mv + mv_w
    o_mo = o_mi + 2 * MLSTM_HEADS
    w_main, w_gates = _w_in_prep(w_in.T, o_ff=o_ff, o_mi=o_mi, tb=512)
    gate_bias = jnp.concatenate([fox_f_bias, mlstm_i_bias, mlstm_f_bias]).reshape(N_GATES, 1)

    proj, vt, gates_raw = _in_proj(x, mixer_norm, w_main, jnp.stack([fox_q_norm, fox_k_norm]), w_gates, tm=512)
    grow, gcol, ka = _gates(gates_raw, gate_bias, batch=batch, s_len=s_len)
    fo = _fox_attention(proj, vt, ka, grow, fox_out_norm, batch=batch, s_len=s_len)
    mh = _mlstm(proj, gcol, grow, conv_w, conv_b, mlstm_out_norm, batch=batch, s_len=s_len)
    x = _proj_residual([fo, mh], w_out, x, tm=512, tn=512)

    xd = d_model // XATTN_HEADS
    q = _norm_proj(x, xattn_norm, w_xq, xq_norm, tm=512,
                   post_scale=xd ** -0.5 * LOG2E)
    kv = _kv_proj(mem, mem_norm, w_xkv, xk_norm, n_norm_tiles=XATTN_HEADS)
    co = _xattn(q, kv, batch=batch, s_len=s_len, n_mem=n_mem, tq=1024)
    x = _proj_residual([co], w_xo, x, tm=512, tn=512)

    tm = 1024
    first, w_up_b, w_down_b = _mlp_first(x, mlp_norm, w_up, w_down, tm=tm, tf=512)
    x = lax.dynamic_update_slice(x, first, (0, 0))
    return _mlp(x, mlp_norm, w_up_b, w_down_b, tm=tm, tf=1024)


def kernel(x, mem, mixer_norm, w_in, fox_f_bias, mlstm_i_bias, mlstm_f_bias, conv_w, conv_b, fox_q_norm,
           fox_k_norm, fox_out_norm, mlstm_out_norm, w_out, xattn_norm, mem_norm, w_xq, w_xkv, xq_norm,
           xk_norm, w_xo, mlp_norm, w_up, w_down):
    batch, s_len, d_model = x.shape
    n_mem = mem.shape[1]
    depth = w_in.shape[0]
    h = x.reshape(batch * s_len, d_model)
    mem2 = mem.reshape(batch * n_mem, d_model)
    for l in range(depth):
        h = _layer(h, mem2, mixer_norm[l], w_in[l], fox_f_bias[l], mlstm_i_bias[l], mlstm_f_bias[l],
                   conv_w[l], conv_b[l], fox_q_norm[l], fox_k_norm[l], fox_out_norm[l], mlstm_out_norm[l],
                   w_out[l], xattn_norm[l], mem_norm[l], w_xq[l], w_xkv[l], xq_norm[l], xk_norm[l], w_xo[l],
                   mlp_norm[l], w_up[l], w_down[l], batch=batch, s_len=s_len, n_mem=n_mem)
    return h.reshape(batch, s_len, d_model)
```
